```python
import math
import jax
import jax.numpy as jnp
from jax import lax
import numpy as np

D_MODEL = 1024
BATCH = 8
SEQ = 2048
DEPTH = 1
DEC_BATCH = 32
DEC_SEQ = 4
PAST_LEN = 8192
PAGE_SIZE = 128

MIX_WIDTH = D_MODEL
A_HEAD_DIM = 64
A_HEADS = MIX_WIDTH // 2 // A_HEAD_DIM
DILATED_BRANCHES = ((128, 1), (512, 4), (2048, 16))
MAX_WINDOW = 2048
BAND_BLOCK = 128
ROPE_THETA = 10000.0
B_KEY_DIM = 128
B_VAL_DIM = 128
B_HEADS = MIX_WIDTH // 2 // B_VAL_DIM
CONV_WIDTH = 4
DELTA_CHUNK = 64
B_QKV = B_HEADS * (2 * B_KEY_DIM + B_VAL_DIM)
SPLITS = (A_HEADS * A_HEAD_DIM,) * 3 + (B_HEADS * B_KEY_DIM, B_HEADS * B_KEY_DIM, B_HEADS * B_VAL_DIM, B_HEADS * B_VAL_DIM, B_HEADS, B_HEADS)
SPLIT_POINTS = tuple(sum(SPLITS[:i + 1]) for i in range(len(SPLITS) - 1))
IN_WIDTH = sum(SPLITS)
PEER_HEADS = 8
PEER_KEYS = 128
PEER_EXPERTS = PEER_KEYS * PEER_KEYS
PEER_QUERY_DIM = 256
PEER_TOPK = 16
PEER_BLOCK = 256
EPS = 1e-6

kernel_name = 'hybrid_dilated_attn_gdn_peer_step'


def rms_norm(x, w):
    x32 = x.astype(jnp.float32)
    y = x32 * lax.rsqrt(jnp.mean(x32 * x32, axis=-1, keepdims=True) + EPS)
    return (y * w.astype(jnp.float32)).astype(x.dtype)


def l2_normalize(x):
    x32 = x.astype(jnp.float32)
    return x32 * lax.rsqrt(jnp.sum(x32 * x32, axis=-1, keepdims=True) + EPS)


def rope(x, pos):
    half = x.shape[-1] // 2
    inv_freq = ROPE_THETA ** (-jnp.arange(half, dtype=jnp.float32) / half)
    ang = pos.astype(jnp.float32)[:, None] * inv_freq[None, :]
    cos = jnp.cos(ang)[None, :, None, :]
    sin = jnp.sin(ang)[None, :, None, :]
    x32 = x.astype(jnp.float32)
    x1, x2 = x32[..., :half], x32[..., half:]
    return jnp.concatenate([x1 * cos - x2 * sin, x2 * cos + x1 * sin], axis=-1).astype(x.dtype)


def masked_softmax(scores, mask):
    scores = jnp.where(mask, scores, -jnp.inf)
    mx = jnp.max(scores, axis=-1, keepdims=True)
    p = jnp.exp(scores - mx)
    l = jnp.sum(p, axis=-1, keepdims=True)
    return p / l, (mx + jnp.log(l))[..., 0]


def dilated_branch_prompt(q, k, v, window, dilation):
    b, s, h, dh = q.shape
    n_back = window // dilation
    m = -(-s // dilation)
    mp = -(-m // BAND_BLOCK) * BAND_BLOCK
    nb = mp // BAND_BLOCK

    def by_residue(t):
        t = jnp.pad(t, ((0, 0), (0, m * dilation - s), (0, 0), (0, 0)))
        t = t.reshape(b, m, dilation, h, dh).transpose(0, 2, 1, 3, 4)
        t = jnp.pad(t, ((0, 0), (0, 0), (0, mp - m), (0, 0), (0, 0)))
        return t.reshape(b, dilation, nb, BAND_BLOCK, h, dh)

    def band(t):
        prev = jnp.pad(t, ((0, 0), (0, 0), (1, 0), (0, 0), (0, 0), (0, 0)))[:, :, :-1]
        return jnp.concatenate([prev, t], axis=3)

    qb = by_residue(q)
    kk = band(by_residue(k))
    vv = band(by_residue(v))
    scores = jnp.einsum('brnqhe,brnkhe->brnhqk', qb, kk) / math.sqrt(dh)
    qi = jnp.arange(BAND_BLOCK)[:, None]
    kj = jnp.arange(2 * BAND_BLOCK)[None, :]
    dist = BAND_BLOCK + qi - kj
    in_band = (dist >= 0) & (dist <= n_back)
    has_prev = (jnp.arange(nb) > 0)[:, None, None] | (kj >= BAND_BLOCK)[None]
    mask = (in_band[None] & has_prev)[None, None, :, None]
    probs, lse = masked_softmax(scores, mask)
    out = jnp.einsum('brnhqk,brnkhe->brnqhe', probs, vv)
    lse = jnp.swapaxes(lse, 3, 4)

    def back(t):
        t = t.reshape((b, dilation, mp) + t.shape[4:])[:, :, :m]
        return jnp.moveaxis(t, 1, 2).reshape((b, m * dilation) + t.shape[3:])[:, :s]

    return back(out), back(lse)


def dilated_branch_sample(q, kc, vc, window, dilation):
    b, l, h, dh = q.shape
    w_len = kc.shape[1] - l
    n_back = window // dilation
    idx = w_len + jnp.arange(l)[:, None] - dilation * jnp.arange(n_back + 1)[None, :]
    valid = idx >= 0
    idx = jnp.maximum(idx, 0)
    kg = kc[:, idx].astype(jnp.float32)
    vg = vc[:, idx].astype(jnp.float32)
    scores = jnp.einsum('blhe,blnhe->blhn', q, kg) / math.sqrt(dh)
    probs, lse = masked_softmax(scores, valid[None, :, None, :])
    out = jnp.einsum('blhn,blnhe->blhe', probs, vg)
    return out, lse


def merge_branches(outs, lses):
    w = jax.nn.softmax(jnp.stack(lses), axis=0)
    return jnp.sum(w[..., None] * jnp.stack(outs), axis=0)


def short_conv(x_ext, conv_w):
    y = lax.conv_general_dilated(x_ext, conv_w[:, None, :].astype(x_ext.dtype), window_strides=(1,), padding='VALID', dimension_numbers=('NWC', 'WIO', 'NWC'), feature_group_count=x_ext.shape[-1])
    return jax.nn.silu(y)


def gated_delta_rule(q, k, v, g, beta, s0):
    b, t, h, _ = q.shape
    dv = v.shape[-1]
    c = DELTA_CHUNK
    nc = -(-t // c)
    pad = nc * c - t

    def chunks(x):
        x = jnp.pad(x, ((0, 0), (0, pad)) + ((0, 0),) * (x.ndim - 2))
        x = x.reshape((b, nc, c) + x.shape[2:])
        return jnp.moveaxis(x, 3, 1)

    qc, kc, vc, gc, bc = chunks(q), chunks(k), chunks(v), chunks(g), chunks(beta)
    gcum = jnp.cumsum(gc, axis=-1)
    ii = jnp.arange(c)[:, None]
    jj = jnp.arange(c)[None, :]
    causal = ii >= jj
    strict = ii > jj
    decay = jnp.exp(jnp.where(causal, gcum[..., :, None] - gcum[..., None, :], -jnp.inf))
    kb = kc * bc[..., None]
    lower = jnp.where(strict, jnp.einsum('bhnid,bhnjd->bhnij', kb, kc) * decay, 0.0)
    a_mat = jnp.eye(c, dtype=jnp.float32) + lower
    rhs = jnp.concatenate([vc * bc[..., None], kb * jnp.exp(gcum)[..., None]], axis=-1)
    sol = lax.linalg.triangular_solve(a_mat, rhs, left_side=True, lower=True, unit_diagonal=True)
    u, w = sol[..., :dv], sol[..., dv:]
    intra = jnp.where(causal, jnp.einsum('bhnid,bhnjd->bhnij', qc, kc) * decay, 0.0)
    q_dec = qc * jnp.exp(gcum)[..., None]
    k_dec = kc * jnp.exp(gcum[..., -1:] - gcum)[..., None]
    chunk_decay = jnp.exp(gcum[..., -1])
    xs = (jnp.moveaxis(q_dec, 2, 0), jnp.moveaxis(k_dec, 2, 0), jnp.moveaxis(u, 2, 0), jnp.moveaxis(w, 2, 0), jnp.moveaxis(intra, 2, 0), jnp.moveaxis(chunk_decay, 2, 0))

    def step(state, inp):
        qd, kd, uc, wc, att, cd = inp
        v_new = uc - jnp.einsum('bhck,bhkv->bhcv', wc, state)
        out = jnp.einsum('bhck,bhkv->bhcv', qd, state) + jnp.einsum('bhij,bhjv->bhiv', att, v_new)
        state = state * cd[..., None, None] + jnp.einsum('bhck,bhcv->bhkv', kd, v_new)
        return state, out

    s_fin, outs = lax.scan(step, s0, xs)
    o = jnp.moveaxis(outs, 0, 2).reshape(b, h, nc * c, dv)[:, :, :t]
    return jnp.moveaxis(o, 1, 2), s_fin


def peer(xn, w_pq, sub_keys, expert_u, expert_v):
    n_tok, d = xn.shape
    n_blk = -(-n_tok // PEER_BLOCK)
    xb = jnp.pad(xn, ((0, n_blk * PEER_BLOCK - n_tok), (0, 0))).reshape(n_blk, PEER_BLOCK, d)
    half = PEER_QUERY_DIM // 2

    def block(xt):
        qh = (xt @ w_pq).astype(jnp.float32).reshape(PEER_BLOCK, PEER_HEADS, 2, half)
        sub = jnp.einsum('thcd,ckd->thck', qh, sub_keys.astype(jnp.float32))
        top_s, top_i = lax.top_k(sub, PEER_TOPK)
        cand = (top_s[:, :, 0, :, None] + top_s[:, :, 1, None, :]).reshape(PEER_BLOCK, PEER_HEADS, PEER_TOPK * PEER_TOPK)
        best_s, best_c = lax.top_k(cand, PEER_TOPK)
        i1 = jnp.take_along_axis(top_i[:, :, 0], best_c // PEER_TOPK, axis=-1)
        i2 = jnp.take_along_axis(top_i[:, :, 1], best_c % PEER_TOPK, axis=-1)
        experts = i1 * PEER_KEYS + i2
        gate = jax.nn.softmax(best_s, axis=-1)
        pre = jnp.einsum('td,thkd->thk', xt, expert_u[experts]).astype(jnp.float32)
        act = jax.nn.gelu(pre, approximate=False)
        return jnp.einsum('thk,thkd->td', (gate * act).astype(xt.dtype), expert_v[experts])

    return lax.map(block, xb).reshape(n_blk * PEER_BLOCK, d)[:n_tok]


def decoder_layer(x, pos, k_past, v_past, conv_buf, s0, norm_mix_w, w_in, q_norm_w, k_norm_w, conv_w, a_log, dt_bias, delta_norm_w, w_out, norm_ffn_w, w_pq, sub_keys, expert_u, expert_v):
    b, t, _ = x.shape
    xn = rms_norm(x, norm_mix_w)
    aq, ak, av, bq, bk, bv, bz, bb, ba = jnp.split(xn @ w_in, SPLIT_POINTS, axis=-1)

    q = rope(rms_norm(aq.reshape(b, t, A_HEADS, A_HEAD_DIM), q_norm_w), pos)
    k = rope(rms_norm(ak.reshape(b, t, A_HEADS, A_HEAD_DIM), k_norm_w), pos)
    v = av.reshape(b, t, A_HEADS, A_HEAD_DIM)
    q32 = q.astype(jnp.float32)
    if k_past is None:
        k32, v32 = k.astype(jnp.float32), v.astype(jnp.float32)
        branches = [dilated_branch_prompt(q32, k32, v32, wnd, dil) for wnd, dil in DILATED_BRANCHES]
        win = min(MAX_WINDOW, t)
        k_buf, v_buf = k[:, t - win:], v[:, t - win:]
    else:
        kc = jnp.concatenate([k_past, k.astype(k_past.dtype)], axis=1)
        vc = jnp.concatenate([v_past, v.astype(v_past.dtype)], axis=1)
        branches = [dilated_branch_sample(q32, kc, vc, wnd, dil) for wnd, dil in DILATED_BRANCHES]
        win = k_past.shape[1]
        k_buf, v_buf = kc[:, kc.shape[1] - win:], vc[:, vc.shape[1] - win:]
    a_out = merge_branches([o for o, _ in branches], [l for _, l in branches])
    a_out = a_out.reshape(b, t, A_HEADS * A_HEAD_DIM).astype(x.dtype)

    raw = jnp.concatenate([bq, bk, bv], axis=-1)
    ext = jnp.concatenate([conv_buf.astype(raw.dtype), raw], axis=1)
    conv_state = ext[:, ext.shape[1] - (CONV_WIDTH - 1):]
    cq, ck, cv = jnp.split(short_conv(ext, conv_w), (B_HEADS * B_KEY_DIM, 2 * B_HEADS * B_KEY_DIM), axis=-1)
    dq = l2_normalize(cq.reshape(b, t, B_HEADS, B_KEY_DIM)) * (B_KEY_DIM ** -0.5)
    dk = l2_normalize(ck.reshape(b, t, B_HEADS, B_KEY_DIM))
    dv = cv.reshape(b, t, B_HEADS, B_VAL_DIM).astype(jnp.float32)
    beta = jax.nn.sigmoid(bb.astype(jnp.float32))
    g = -jnp.exp(a_log.astype(jnp.float32)) * jax.nn.softplus(ba.astype(jnp.float32) + dt_bias.astype(jnp.float32))
    o, s_fin = gated_delta_rule(dq, dk, dv, g, beta, s0.astype(jnp.float32))
    o = rms_norm(o, delta_norm_w) * jax.nn.silu(bz.reshape(b, t, B_HEADS, B_VAL_DIM).astype(jnp.float32))
    b_out = o.reshape(b, t, B_HEADS * B_VAL_DIM).astype(x.dtype)

    h = x + jnp.concatenate([a_out, b_out], axis=-1) @ w_out
    hn = rms_norm(h, norm_ffn_w)
    y = h + peer(hn.reshape(b * t, -1), w_pq, sub_keys, expert_u, expert_v).reshape(b, t, -1)
    return y, k_buf, v_buf, conv_state, s_fin.astype(s0.dtype)


def setup_inputs(seed: int = 0) -> dict:
    key = jax.random.key(seed)
    ks = jax.random.split(key, 24)
    win_buf = min(MAX_WINDOW, PAST_LEN)

    def nrm(k, shape, scale):
        return jax.random.normal(k, shape, jnp.float32) * scale

    dt = jnp.exp(jax.random.uniform(ks[9], (DEPTH, B_HEADS), jnp.float32, math.log(1e-3), math.log(1e-1)))
    return {
        'x_prompt': nrm(ks[0], (BATCH, SEQ, D_MODEL), 1.0),
        'x_sample': nrm(ks[1], (DEC_BATCH, DEC_SEQ, D_MODEL), 1.0),
        'cache_k_win': nrm(ks[2], (DEPTH, DEC_BATCH, win_buf, A_HEADS, A_HEAD_DIM), 1.0),
        'cache_v_win': nrm(ks[3], (DEPTH, DEC_BATCH, win_buf, A_HEADS, A_HEAD_DIM), 1.0),
        'state_conv': nrm(ks[4], (DEPTH, DEC_BATCH, CONV_WIDTH - 1, B_QKV), 1.0),
        'state_delta': nrm(ks[5], (DEPTH, DEC_BATCH, B_HEADS, B_KEY_DIM, B_VAL_DIM), 0.05),
        'norm_mix_w': 1.0 + nrm(ks[6], (DEPTH, D_MODEL), 0.02),
        'w_in': nrm(ks[7], (DEPTH, D_MODEL, IN_WIDTH), D_MODEL ** -0.5),
        'q_norm_w': 1.0 + nrm(ks[8], (DEPTH, A_HEAD_DIM), 0.02),
        'k_norm_w': 1.0 + nrm(ks[10], (DEPTH, A_HEAD_DIM), 0.02),
        'conv_w': nrm(ks[11], (DEPTH, CONV_WIDTH, B_QKV), CONV_WIDTH ** -0.5),
        'a_log': jnp.log(jax.random.uniform(ks[12], (DEPTH, B_HEADS), jnp.float32, 1.0, 16.0)),
        'dt_bias': dt + jnp.log(-jnp.expm1(-dt)),
        'delta_norm_w': 1.0 + nrm(ks[13], (DEPTH, B_VAL_DIM), 0.02),
        'w_out': nrm(ks[14], (DEPTH, MIX_WIDTH, D_MODEL), MIX_WIDTH ** -0.5),
        'norm_ffn_w': 1.0 + nrm(ks[15], (DEPTH, D_MODEL), 0.02),
        'w_pq': nrm(ks[16], (DEPTH, D_MODEL, PEER_HEADS * PEER_QUERY_DIM), D_MODEL ** -0.5),
        'sub_keys': nrm(ks[17], (DEPTH, 2, PEER_KEYS, PEER_QUERY_DIM // 2), (PEER_QUERY_DIM // 2) ** -0.5),
        'expert_u': nrm(ks[18], (DEPTH, PEER_EXPERTS, D_MODEL), D_MODEL ** -0.5),
        'expert_v': nrm(ks[19], (DEPTH, PEER_EXPERTS, D_MODEL), PEER_HEADS ** -0.5),
    }


def reference(x_prompt, x_sample, cache_k_win, cache_v_win, state_conv, state_delta, norm_mix_w, w_in, q_norm_w, k_norm_w, conv_w, a_log, dt_bias, delta_norm_w, w_out, norm_ffn_w, w_pq, sub_keys, expert_u, expert_v):
    bp, sp, _ = x_prompt.shape
    pos_prompt = jnp.arange(sp)
    pos_sample = PAST_LEN + jnp.arange(x_sample.shape[1])
    yp, ys = x_prompt, x_sample
    kp_l, vp_l, cp_l, dp_l, ks_l, vs_l, cs_l, ds_l = [], [], [], [], [], [], [], []
    for layer in range(DEPTH):
        weights = (norm_mix_w[layer], w_in[layer], q_norm_w[layer], k_norm_w[layer], conv_w[layer], a_log[layer], dt_bias[layer], delta_norm_w[layer], w_out[layer], norm_ffn_w[layer], w_pq[layer], sub_keys[layer], expert_u[layer], expert_v[layer])
        conv0 = jnp.zeros((bp, CONV_WIDTH - 1, B_QKV), x_prompt.dtype)
        s_zero = jnp.zeros((bp, B_HEADS, B_KEY_DIM, B_VAL_DIM), jnp.float32)
        yp, kb, vb, cb, sb = decoder_layer(yp, pos_prompt, None, None, conv0, s_zero, *weights)
        kp_l.append(kb)
        vp_l.append(vb)
        cp_l.append(cb)
        dp_l.append(sb)
        ys, kb, vb, cb, sb = decoder_layer(ys, pos_sample, cache_k_win[layer], cache_v_win[layer], state_conv[layer], state_delta[layer], *weights)
        ks_l.append(kb)
        vs_l.append(vb)
        cs_l.append(cb)
        ds_l.append(sb)
    k_win_prompt, v_win_prompt = jnp.stack(kp_l), jnp.stack(vp_l)
    conv_prompt, delta_prompt = jnp.stack(cp_l), jnp.stack(dp_l)
    k_win_sample, v_win_sample = jnp.stack(ks_l), jnp.stack(vs_l)
    conv_sample, delta_sample = jnp.stack(cs_l), jnp.stack(ds_l)
    return (yp, ys, k_win_prompt, v_win_prompt, conv_prompt, delta_prompt, k_win_sample, v_win_sample, conv_sample, delta_sample)
```

```python
import functools
import math

import jax
import jax.numpy as jnp
from jax import lax
from jax.experimental import pallas as pl
from jax.experimental.pallas import tpu as pltpu

F32 = jnp.float32
BF16 = jnp.bfloat16
HIGHEST = lax.Precision.HIGHEST

D_MODEL = 1024
A_HEADS = 8
A_HEAD_DIM = 64
A_WIDTH = A_HEADS * A_HEAD_DIM
DILATIONS = (1, 4, 16)
BAND = 128
ROPE_THETA = 10000.0
PAST_LEN = 8192
B_HEADS = 4
B_DIM = 128
B_QKV = 3 * B_HEADS * B_DIM
CONV_WIDTH = 4
PEER_HEADS = 8
PEER_KEYS = 128
PEER_TOPK = 16
EPS = 1e-6
NEG_INF = float("-inf")
VMEM_LIMIT = 56 * 1024 * 1024


def _params(*sem):
    return pltpu.CompilerParams(dimension_semantics=sem, vmem_limit_bytes=VMEM_LIMIT)


def _dot(a, b, **kw):
    return jnp.dot(a, b, preferred_element_type=F32, **kw)


def _dot_nt(a, b, **kw):
    return lax.dot_general(a, b, (((1,), (1,)), ((), ())), preferred_element_type=F32, **kw)


def _dot_tn(a, b, **kw):
    return lax.dot_general(a, b, (((0,), (0,)), ((), ())), preferred_element_type=F32, **kw)


def _sigmoid(x):
    return 1.0 / (1.0 + jnp.exp(-x))


def _proj_in_kernel(x_ref, nw_ref, wa_ref, wb_ref, wz_ref, wg_ref, qnw_ref, knw_ref, cos_ref, sin_ref,
                    hm_ref, alog_ref, dtb_ref, q_ref, k_ref, v_ref, raw_ref, z_ref, gb_ref):
    x = x_ref[...]
    ms = jnp.mean(x * x, axis=-1, keepdims=True)
    xn = (x * lax.rsqrt(ms + EPS) * nw_ref[...]).astype(BF16)
    a = _dot(xn, wa_ref[...])
    hm = hm_ref[...]
    lane = lax.broadcasted_iota(jnp.int32, (1, A_WIDTH), 1)
    first_half = (lane & (A_HEAD_DIM // 2)) == 0
    cos = cos_ref[...]
    sin = sin_ref[...]

    def norm_rope(t, w):
        t2 = t * t
        hi = t2.astype(BF16)
        lo = (t2 - hi.astype(F32)).astype(BF16)
        ss = _dot(hi, hm) + _dot(lo, hm)
        tn = t * lax.rsqrt(ss * (1.0 / A_HEAD_DIM) + EPS) * w
        partner = jnp.where(first_half, pltpu.roll(tn, A_WIDTH - A_HEAD_DIM // 2, 1), pltpu.roll(tn, A_HEAD_DIM // 2, 1))
        return tn * cos + partner * sin

    q_ref[...] = norm_rope(a[:, :A_WIDTH], qnw_ref[...])
    k_ref[...] = norm_rope(a[:, A_WIDTH:2 * A_WIDTH], knw_ref[...])
    v_ref[...] = a[:, 2 * A_WIDTH:]
    raw_ref[...] = _dot(xn, wb_ref[...])
    z_ref[...] = _dot(xn, wz_ref[...])
    zg = _dot(xn, wg_ref[...])
    beta = _sigmoid(zg)
    t = zg + dtb_ref[...]
    softplus = jnp.maximum(t, 0.0) + jnp.log1p(jnp.exp(-jnp.abs(t)))
    g = -jnp.exp(alog_ref[...]) * softplus
    glane = lax.broadcasted_iota(jnp.int32, (1, B_HEADS * B_DIM), 1) & (B_DIM - 1)
    gb_ref[...] = jnp.where(glane == 0, beta, jnp.where(glane == 1, g, 0.0))


def _proj_in(x, pos_cos, pos_sin, pos_blocks, wts, tm):
    t = x.shape[0]
    n = t // tm
    row = lambda i: (i, 0)
    const = lambda i: (0, 0)
    posmap = lambda i: (i % pos_blocks, 0)
    full = lambda a: pl.BlockSpec(a.shape, const)
    outs = [(A_WIDTH, "q"), (A_WIDTH, "k"), (A_WIDTH, "v"), (B_QKV, "raw"), (B_HEADS * B_DIM, "z"), (B_HEADS * B_DIM, "gb")]
    return pl.pallas_call(
        _proj_in_kernel,
        grid=(n,),
        in_specs=[pl.BlockSpec((tm, D_MODEL), row), full(wts["norm_mix"]), full(wts["wa"]), full(wts["wb"]), full(wts["wz"]),
                  full(wts["wg"]), full(wts["qnw"]), full(wts["knw"]),
                  pl.BlockSpec((tm, A_WIDTH), posmap), pl.BlockSpec((tm, A_WIDTH), posmap),
                  full(wts["headmat"]), full(wts["alog"]), full(wts["dtb"])],
        out_specs=[pl.BlockSpec((tm, w), row) for w, _ in outs],
        out_shape=[jax.ShapeDtypeStruct((t, w), F32) for w, _ in outs],
        compiler_params=_params("parallel"),
        name="proj_in",
    )(x, wts["norm_mix"], wts["wa"], wts["wb"], wts["wz"], wts["wg"], wts["qnw"], wts["knw"], pos_cos, pos_sin,
      wts["headmat"], wts["alog"], wts["dtb"])


def _attn_prompt_kernel(q_ref, k_ref, v_ref, o_ref, ob_ref, lb_ref, *, seq):
    lane = lax.broadcasted_iota(jnp.int32, (BAND, 2 * A_HEAD_DIM), 1)
    head0 = lane < A_HEAD_DIM
    qi = lax.broadcasted_iota(jnp.int32, (BAND, 2 * BAND), 0)
    kj = lax.broadcasted_iota(jnp.int32, (BAND, 2 * BAND), 1)
    dist = BAND + qi - kj
    in_band = (dist >= 0) & (dist <= BAND)
    causal = in_band[:, BAND:]
    scale = 1.0 / math.sqrt(A_HEAD_DIM)

    for br, dil in enumerate(DILATIONS):
        nblk = seq // (BAND * dil)
        shift = dil.bit_length() - 1

        def rows(ref, start):
            if dil == 1:
                return ref[pl.ds(start, BAND), :]
            return ref[pl.ds(start, BAND, stride=dil), :]

        def unit(u, carry):
            r = u & (dil - 1)
            n = u >> shift
            start = r + n * (BAND * dil)
            qb = rows(q_ref, start)
            k_own = rows(k_ref, start).astype(BF16)
            v_own = rows(v_ref, start).astype(BF16)
            if nblk > 1:
                pstart = jnp.maximum(start - BAND * dil, r)
                kk = jnp.concatenate([rows(k_ref, pstart).astype(BF16), k_own], axis=0)
                vv = jnp.concatenate([rows(v_ref, pstart).astype(BF16), v_own], axis=0)
                mask = in_band & (kj >= jnp.where(n > 0, 0, BAND))
            else:
                kk, vv, mask = k_own, v_own, causal
            outs = []
            lses = []
            for hmask in (head0, ~head0):
                qh = jnp.where(hmask, qb, 0.0).astype(BF16)
                s = _dot_nt(qh, kk) * scale
                s = jnp.where(mask, s, NEG_INF)
                mx = jnp.max(s, axis=-1, keepdims=True)
                p = jnp.exp(s - mx)
                l = jnp.sum(p, axis=-1, keepdims=True)
                outs.append(_dot(p.astype(BF16), vv) / l)
                lses.append(mx + jnp.log(l))
            out = jnp.where(head0, outs[0], outs[1])
            lse = jnp.where(head0, lses[0], lses[1])
            if dil == 1:
                ob_ref[br, pl.ds(start, BAND), :] = out
                lb_ref[br, pl.ds(start, BAND), :] = lse
            else:
                ob_ref[br, pl.ds(start, BAND, stride=dil), :] = out
                lb_ref[br, pl.ds(start, BAND, stride=dil), :] = lse
            return carry

        lax.fori_loop(0, seq // BAND, unit, 0)

    def merge(c, carry):
        rs = pl.ds(pl.multiple_of(c * 256, 256), 256)
        l0, l1, l2 = lb_ref[0, rs, :], lb_ref[1, rs, :], lb_ref[2, rs, :]
        m = jnp.maximum(jnp.maximum(l0, l1), l2)
        w0, w1, w2 = jnp.exp(l0 - m), jnp.exp(l1 - m), jnp.exp(l2 - m)
        o_ref[rs, :] = (w0 * ob_ref[0, rs, :] + w1 * ob_ref[1, rs, :] + w2 * ob_ref[2, rs, :]) / (w0 + w1 + w2)
        return carry

    lax.fori_loop(0, seq // 256, merge, 0)


def _attn_prompt(q, k, v):
    b, s, _ = q.shape
    spec = pl.BlockSpec((None, s, 2 * A_HEAD_DIM), lambda i, j: (i, 0, j))
    return pl.pallas_call(
        functools.partial(_attn_prompt_kernel, seq=s),
        grid=(b, A_HEADS // 2),
        in_specs=[spec, spec, spec],
        out_specs=spec,
        out_shape=jax.ShapeDtypeStruct((b, s, A_WIDTH), F32),
        scratch_shapes=[pltpu.VMEM((len(DILATIONS), s, 2 * A_HEAD_DIM), F32),
                        pltpu.VMEM((len(DILATIONS), s, 2 * A_HEAD_DIM), F32)],
        compiler_params=_params("parallel", "parallel"),
        name="attn_prompt",
    )(q, k, v)


def _attn_sample_kernel(q_ref, kn_ref, vn_ref, kc_ref, vc_ref, hs_ref, he_ref, o_ref, ko_ref, vo_ref, kbuf, vbuf,
                        *, win, new):
    nlb = A_WIDTH // 128
    for buf, cache, fresh, shifted in ((kbuf, kc_ref, kn_ref, ko_ref), (vbuf, vc_ref, vn_ref, vo_ref)):
        for j in range(nlb):
            ls = slice(j * 128, (j + 1) * 128)
            buf[j, 0:win, :] = cache[:, ls]
            buf[j, win:win + new, :] = fresh[:, ls]
            shifted[:, ls] = buf[j, new:new + win, :]

    def gather(buf, start, size, stride):
        if stride == 1:
            return jnp.concatenate([buf[j, pl.ds(start, size), :] for j in range(nlb)], axis=1)
        return jnp.concatenate([buf[j, pl.ds(start, size, stride=stride), :] for j in range(nlb)], axis=1)

    hs = hs_ref[...]
    he = he_ref[...]
    scale = 1.0 / math.sqrt(A_HEAD_DIM)
    for l in range(new):
        ql = q_ref[l:l + 1, :]
        outs, lses = [], []
        for dil in DILATIONS:
            start = win + l - BAND * dil
            ks = gather(kbuf, start, BAND, dil)
            vs = gather(vbuf, start, BAND, dil)
            kself = gather(kbuf, win + l, 1, 1)
            vself = gather(vbuf, win + l, 1, 1)
            s = _dot(ks * ql, hs, precision=HIGHEST) * scale
            sself = _dot(kself * ql, hs, precision=HIGHEST) * scale
            mx = jnp.maximum(jnp.max(s, axis=0, keepdims=True), sself)
            p = jnp.exp(s - mx)
            pself = jnp.exp(sself - mx)
            lsum = jnp.sum(p, axis=0, keepdims=True) + pself
            pe = _dot(p, he, precision=HIGHEST)
            o = jnp.sum(pe * vs, axis=0, keepdims=True) + _dot(pself, he, precision=HIGHEST) * vself
            outs.append(o / _dot(lsum, he, precision=HIGHEST))
            lses.append(_dot(mx + jnp.log(lsum), he, precision=HIGHEST))
        m = jnp.maximum(jnp.maximum(lses[0], lses[1]), lses[2])
        w = [jnp.exp(ls - m) for ls in lses]
        o_ref[l:l + 1, :] = (w[0] * outs[0] + w[1] * outs[1] + w[2] * outs[2]) / (w[0] + w[1] + w[2])


def _attn_sample(q, kn, vn, kc, vc, hs, he):
    b, new, _ = q.shape
    win = kc.shape[1]
    assert win >= BAND * max(DILATIONS)
    small = pl.BlockSpec((None, new, A_WIDTH), lambda i: (i, 0, 0))
    big = pl.BlockSpec((None, win, A_WIDTH), lambda i: (i, 0, 0))
    const = lambda a: pl.BlockSpec(a.shape, lambda i: (0, 0))
    return pl.pallas_call(
        functools.partial(_attn_sample_kernel, win=win, new=new),
        grid=(b,),
        in_specs=[small, small, small, big, big, const(hs), const(he)],
        out_specs=[small, big, big],
        out_shape=[jax.ShapeDtypeStruct((b, new, A_WIDTH), F32), jax.ShapeDtypeStruct((b, win, A_WIDTH), F32),
                   jax.ShapeDtypeStruct((b, win, A_WIDTH), F32)],
        scratch_shapes=[pltpu.VMEM((A_WIDTH // 128, win + 8, 128), F32), pltpu.VMEM((A_WIDTH // 128, win + 8, 128), F32)],
        compiler_params=_params("parallel"),
        name="attn_sample",
    )(q, kn, vn, kc, vc, hs, he)


def _delta_kernel(rq_ref, rk_ref, rv_ref, hq_ref, hk_ref, hv_ref, wq_ref, wk_ref, wv_ref, gb_ref, z_ref, dnw_ref, s0_ref,
                  o_ref, sf_ref, ext, s_scr, *, seq, chunk):
    c = chunk
    for i, (r_ref, h_ref) in enumerate(((rq_ref, hq_ref), (rk_ref, hk_ref), (rv_ref, hv_ref))):
        ext[i, 0:8, :] = h_ref[...]
        ext[i, 8:8 + seq, :] = r_ref[...]
    s_scr[...] = s0_ref[...]
    ii = lax.broadcasted_iota(jnp.int32, (c, c), 0)
    jj = lax.broadcasted_iota(jnp.int32, (c, c), 1)
    causal = ii >= jj
    strict = ii > jj
    eye = (ii == jj).astype(F32)
    tri = causal.astype(F32)
    ones = jnp.ones((c, c), F32)
    dnw = dnw_ref[...]

    def conv_silu(i, w_ref, base):
        y = w_ref[0:1, :] * ext[i, pl.ds(base + 5, c), :]
        for t in range(1, CONV_WIDTH):
            y = y + w_ref[t:t + 1, :] * ext[i, pl.ds(base + 5 + t, c), :]
        return y * _sigmoid(y)

    def l2n(t):
        return t * lax.rsqrt(jnp.sum(t * t, axis=-1, keepdims=True) + EPS)

    def body(n, carry):
        base = pl.multiple_of(n * c, c)
        qc = l2n(conv_silu(0, wq_ref, base)) * (B_DIM ** -0.5)
        kc = l2n(conv_silu(1, wk_ref, base))
        vc = conv_silu(2, wv_ref, base)
        gbc = gb_ref[pl.ds(base, c), :]
        beta = gbc[:, 0:1]
        g = gbc[:, 1:2]
        gsq = jnp.broadcast_to(g, (c, c))
        gcol = _dot(tri, gsq, precision=HIGHEST)
        grow = _dot(ones, jnp.where(ii <= jj, gsq, 0.0), precision=HIGHEST)
        gcum = gcol[:, 0:1]
        glast = gcol[c - 1:c, 0:1]
        decay = jnp.exp(jnp.where(causal, gcol - grow, NEG_INF))
        kb = kc * beta
        kbh, kch, qch = kb.astype(BF16), kc.astype(BF16), qc.astype(BF16)
        lower = jnp.where(strict, _dot_nt(kbh, kch) * decay, 0.0)
        tinv = eye - lower
        lp = lower
        span = 2
        while span < c:
            lp = _dot(lp, lp, precision=HIGHEST)
            tinv = tinv + _dot(tinv, lp, precision=HIGHEST)
            span *= 2
        eg = jnp.exp(gcum)
        u = _dot(tinv, vc * beta, precision=HIGHEST)
        w = _dot(tinv, kb * eg, precision=HIGHEST)
        intra = jnp.where(causal, _dot_nt(qch, kch) * decay, 0.0)
        state = s_scr[...]
        sh = state.astype(BF16)
        v_new = u - _dot(w.astype(BF16), sh)
        out = _dot((qc * eg).astype(BF16), sh) + _dot(intra.astype(BF16), v_new.astype(BF16))
        kdec = kc * jnp.exp(glast - gcum)
        s_scr[...] = state * jnp.exp(glast) + _dot_tn(kdec.astype(BF16), v_new.astype(BF16))
        on = out * lax.rsqrt(jnp.mean(out * out, axis=-1, keepdims=True) + EPS) * dnw
        zc = z_ref[pl.ds(base, c), :]
        o_ref[pl.ds(base, c), :] = on * (zc * _sigmoid(zc))
        return carry

    lax.fori_loop(0, seq // c, body, 0)
    sf_ref[...] = s_scr[...]


def _delta(raw, hist, conv_w, gb, z, dnw, s0, chunk):
    b, seq, _ = raw.shape
    rspec = lambda off: pl.BlockSpec((None, seq, B_DIM), lambda i, h: (i, 0, off + h))
    hspec = lambda off: pl.BlockSpec((None, 8, B_DIM), lambda i, h: (i, 0, off + h))
    wspec = lambda off: pl.BlockSpec((CONV_WIDTH, B_DIM), lambda i, h: (0, off + h))
    tok = pl.BlockSpec((None, seq, B_DIM), lambda i, h: (i, 0, h))
    st = pl.BlockSpec((None, None, B_DIM, B_DIM), lambda i, h: (i, h, 0, 0))
    return pl.pallas_call(
        functools.partial(_delta_kernel, seq=seq, chunk=chunk),
        grid=(b, B_HEADS),
        in_specs=[rspec(0), rspec(B_HEADS), rspec(2 * B_HEADS), hspec(0), hspec(B_HEADS), hspec(2 * B_HEADS),
                  wspec(0), wspec(B_HEADS), wspec(2 * B_HEADS), tok, tok, pl.BlockSpec((1, B_DIM), lambda i, h: (0, 0)), st],
        out_specs=[tok, st],
        out_shape=[jax.ShapeDtypeStruct((b, seq, B_HEADS * B_DIM), F32), jax.ShapeDtypeStruct((b, B_HEADS, B_DIM, B_DIM), F32)],
        scratch_shapes=[pltpu.VMEM((3, seq + 8, B_DIM), F32), pltpu.VMEM((B_DIM, B_DIM), F32)],
        compiler_params=_params("parallel", "parallel"),
        name="delta",
    )(raw, raw, raw, hist, hist, hist, conv_w, conv_w, conv_w, gb, z, dnw, s0)


def _proj_out_kernel(x_ref, a_ref, b_ref, wa_ref, wb_ref, nw_ref, h_ref, hnt_ref):
    h = x_ref[...] + _dot(a_ref[...].astype(BF16), wa_ref[...]) + _dot(b_ref[...].astype(BF16), wb_ref[...])
    h_ref[...] = h
    hn = h * lax.rsqrt(jnp.mean(h * h, axis=-1, keepdims=True) + EPS) * nw_ref[...]
    hnt_ref[...] = hn.T.astype(BF16)


def _proj_out(x, a, b, wts, tm):
    t = x.shape[0]
    row = lambda i: (i, 0)
    full = lambda arr: pl.BlockSpec(arr.shape, lambda i: (0, 0))
    return pl.pallas_call(
        _proj_out_kernel,
        grid=(t // tm,),
        in_specs=[pl.BlockSpec((tm, D_MODEL), row), pl.BlockSpec((tm, A_WIDTH), row), pl.BlockSpec((tm, A_WIDTH), row),
                  full(wts["wo_a"]), full(wts["wo_b"]), full(wts["norm_ffn"])],
        out_specs=[pl.BlockSpec((tm, D_MODEL), row), pl.BlockSpec((D_MODEL, tm), lambda i: (0, i))],
        out_shape=[jax.ShapeDtypeStruct((t, D_MODEL), F32), jax.ShapeDtypeStruct((D_MODEL, t), BF16)],
        compiler_params=_params("parallel"),
        name="proj_out",
    )(x, a, b, wts["wo_a"], wts["wo_b"], wts["norm_ffn"])


def _top16(x):
    t = x.shape[1]
    slot = lax.broadcasted_iota(jnp.int32, (PEER_TOPK, t), 0).astype(F32)
    vals = jnp.full((PEER_TOPK, t), NEG_INF, F32)
    filled = jnp.zeros((1, t), F32)
    for _ in range(PEER_TOPK):
        m = jnp.max(x, axis=0, keepdims=True)
        eq = x == m
        cnt = jnp.sum(jnp.where(eq, 1.0, 0.0), axis=0, keepdims=True)
        vals = jnp.where((slot >= filled) & (slot < filled + cnt), m, vals)
        filled = filled + cnt
        x = jnp.where(eq, NEG_INF, x)
    return vals


def _kth_largest(x, k):
    t = x.shape[1]
    kth = jnp.full((1, t), NEG_INF, F32)
    filled = jnp.zeros((1, t), F32)
    for _ in range(k):
        m = jnp.max(x, axis=0, keepdims=True)
        eq = x == m
        cnt = jnp.sum(jnp.where(eq, 1.0, 0.0), axis=0, keepdims=True)
        kth = jnp.where((filled < k) & (filled + cnt >= k), m, kth)
        filled = filled + cnt
        x = jnp.where(eq, NEG_INF, x)
    return kth


def _peer_select_kernel(hnt_ref, wpq_ref, sk_ref, n1_ref, q1_ref, r2_ref, q2_ref, sub_scr, top_scr):
    hnt = hnt_ref[...]

    def half(hc, carry):
        row0 = pl.multiple_of(hc * PEER_KEYS, PEER_KEYS)
        qt = _dot(wpq_ref[pl.ds(row0, PEER_KEYS), :], hnt)
        sub = _dot(sk_ref[hc & 1], qt.astype(BF16))
        sub_scr[hc] = sub
        top_scr[hc] = _top16(sub)
        return carry

    lax.fori_loop(0, 2 * PEER_HEADS, half, 0)

    def head(h, carry):
        a = top_scr[2 * h]
        b = top_scr[2 * h + 1]
        pieces = [a[0:1] + b] + [a[i:i + 1] + b[0:8] for i in range(1, 8)] + [a[8:16] + b[0:1]]
        cand = jnp.concatenate(pieces, axis=0)
        tau = _kth_largest(cand, PEER_TOPK)
        top = a[0:1] + b[0:1]
        zsum = jnp.sum(jnp.where(cand >= tau, jnp.exp(cand - top), 0.0), axis=0, keepdims=True)
        s1 = sub_scr[2 * h]
        s2 = sub_scr[2 * h + 1]
        n1 = jnp.zeros_like(s1)
        r2 = jnp.zeros_like(s2)
        for j in range(PEER_TOPK):
            bj = b[j:j + 1]
            n1 = n1 + jnp.where(s1 + bj >= tau, 1.0, 0.0)
            r2 = r2 + jnp.where(bj > s2, 1.0, 0.0)
        n1_ref[h] = n1
        r2_ref[h] = r2
        q1_ref[h] = jnp.exp(s1 - a[0:1]) / zsum
        q2_ref[h] = jnp.exp(s2 - b[0:1])
        return carry

    lax.fori_loop(0, PEER_HEADS, head, 0)


def _peer_select(hnt, wts, tt):
    t = hnt.shape[1]
    out = pl.BlockSpec((PEER_HEADS, PEER_KEYS, tt), lambda i: (0, 0, i))
    shape = jax.ShapeDtypeStruct((PEER_HEADS, PEER_KEYS, t), F32)
    return pl.pallas_call(
        _peer_select_kernel,
        grid=(t // tt,),
        in_specs=[pl.BlockSpec((D_MODEL, tt), lambda i: (0, i)),
                  pl.BlockSpec(wts["wpq_t"].shape, lambda i: (0, 0)),
                  pl.BlockSpec(wts["sub_keys"].shape, lambda i: (0, 0, 0))],
        out_specs=[out, out, out, out],
        out_shape=[shape, shape, shape, shape],
        scratch_shapes=[pltpu.VMEM((2 * PEER_HEADS, PEER_KEYS, tt), F32), pltpu.VMEM((2 * PEER_HEADS, PEER_TOPK, tt), F32)],
        compiler_params=_params("parallel"),
        name="peer_select",
    )(hnt, wts["wpq_t"], wts["sub_keys"])


A_PER_TILE = 8


def _peer_dense_kernel(hnt_ref, u_ref, vt_ref, n1_ref, q1_ref, r2_ref, q2_ref, h_ref, y_ref, acc_ref, pre_ref, w_ref, *, tt):
    j = pl.program_id(1)

    @pl.when(j == 0)
    def _():
        acc_ref[...] = jnp.zeros_like(acc_ref)

    pre_ref[...] = _dot(u_ref[...], hnt_ref[...])
    for ia in range(A_PER_TILE):
        rows = slice(ia * PEER_KEYS, (ia + 1) * PEER_KEYS)
        for tc in range(tt // 128):
            ls = slice(tc * 128, (tc + 1) * 128)
            gate = jnp.zeros((PEER_KEYS, 128), F32)
            for h in range(PEER_HEADS):
                sel = r2_ref[h, :, ls] < n1_ref[h, ia:ia + 1, ls]
                gate = gate + jnp.where(sel, q2_ref[h, :, ls], 0.0) * q1_ref[h, ia:ia + 1, ls]
            p = pre_ref[rows, ls]
            act = 0.5 * p * (1.0 + lax.erf(p * (1.0 / math.sqrt(2.0))))
            w_ref[rows, ls] = (gate * act).astype(BF16)
    acc_ref[...] += _dot(vt_ref[...], w_ref[...])

    @pl.when(j == pl.num_programs(1) - 1)
    def _():
        y_ref[...] = h_ref[...] + acc_ref[...].T


def _peer_dense(hnt, h, sel, wts, tt):
    t = hnt.shape[1]
    n1, q1, r2, q2 = sel
    et = A_PER_TILE * PEER_KEYS
    a_spec = pl.BlockSpec((PEER_HEADS, A_PER_TILE, tt), lambda i, j: (0, j, i))
    b_spec = pl.BlockSpec((PEER_HEADS, PEER_KEYS, tt), lambda i, j: (0, 0, i))
    return pl.pallas_call(
        functools.partial(_peer_dense_kernel, tt=tt),
        grid=(t // tt, PEER_KEYS // A_PER_TILE),
        in_specs=[pl.BlockSpec((D_MODEL, tt), lambda i, j: (0, i)),
                  pl.BlockSpec((et, D_MODEL), lambda i, j: (j, 0)),
                  pl.BlockSpec((D_MODEL, et), lambda i, j: (0, j)),
                  a_spec, a_spec, b_spec, b_spec,
                  pl.BlockSpec((tt, D_MODEL), lambda i, j: (i, 0))],
        out_specs=pl.BlockSpec((tt, D_MODEL), lambda i, j: (i, 0)),
        out_shape=jax.ShapeDtypeStruct((t, D_MODEL), F32),
        scratch_shapes=[pltpu.VMEM((D_MODEL, tt), F32), pltpu.VMEM((et, tt), F32), pltpu.VMEM((et, tt), BF16)],
        compiler_params=_params("parallel", "arbitrary"),
        name="peer_dense",
    )(hnt, wts["u"], wts["v_t"], n1, q1, r2, q2, h)


def _rope_tables(pos, reps):
    half = A_HEAD_DIM // 2
    inv_freq = ROPE_THETA ** (-jnp.arange(half, dtype=F32) / half)
    ang = pos.astype(F32)[:, None] * inv_freq[None, :]
    cos = jnp.cos(ang)
    sin = jnp.sin(ang)
    cos_t = jnp.tile(jnp.concatenate([cos, cos], axis=-1), (reps, A_HEADS))
    sin_t = jnp.tile(jnp.concatenate([-sin, sin], axis=-1), (reps, A_HEADS))
    return cos_t, sin_t


def _prep_weights(norm_mix_w, w_in, q_norm_w, k_norm_w, a_log, dt_bias, w_out, norm_ffn_w, w_pq, sub_keys, expert_u, expert_v):
    na = 3 * A_WIDTH
    hb = B_HEADS * B_DIM
    wgate = w_in[:, na + B_QKV + hb:]
    wg = jnp.zeros((D_MODEL, B_HEADS, B_DIM), F32)
    wg = wg.at[:, :, 0].set(wgate[:, :B_HEADS]).at[:, :, 1].set(wgate[:, B_HEADS:])
    lane_head = jnp.arange(A_WIDTH) // A_HEAD_DIM
    put1 = lambda vec: jnp.zeros((B_HEADS, B_DIM), F32).at[:, 1].set(vec).reshape(1, hb)
    return {
        "norm_mix": norm_mix_w.reshape(1, D_MODEL),
        "wa": w_in[:, :na].astype(BF16),
        "wb": w_in[:, na:na + B_QKV].astype(BF16),
        "wz": w_in[:, na + B_QKV:na + B_QKV + hb].astype(BF16),
        "wg": wg.reshape(D_MODEL, hb).astype(BF16),
        "qnw": jnp.tile(q_norm_w, A_HEADS).reshape(1, A_WIDTH),
        "knw": jnp.tile(k_norm_w, A_HEADS).reshape(1, A_WIDTH),
        "headmat": (lane_head[:, None] == lane_head[None, :]).astype(BF16),
        "alog": put1(a_log),
        "dtb": put1(dt_bias),
        "wo_a": w_out[:A_WIDTH].astype(BF16),
        "wo_b": w_out[A_WIDTH:].astype(BF16),
        "norm_ffn": norm_ffn_w.reshape(1, D_MODEL),
        "wpq_t": w_pq.T.astype(BF16),
        "sub_keys": sub_keys.astype(BF16),
        "u": expert_u.astype(BF16),
        "v_t": expert_v.T.astype(BF16),
    }


def _layer(x, pos_cos, pos_sin, pos_blocks, wts, conv_w, dnw, mixer_a, hist, s0, chunk, tm, tt_sel, tt_dense):
    b, t, _ = x.shape
    xf = x.reshape(b * t, D_MODEL)
    q, k, v, raw, z, gb = _proj_in(xf, pos_cos, pos_sin, pos_blocks, wts, tm)
    a_out, k_buf, v_buf = mixer_a(q.reshape(b, t, A_WIDTH), k.reshape(b, t, A_WIDTH), v.reshape(b, t, A_WIDTH))
    raw = raw.reshape(b, t, B_QKV)
    conv_state = raw[:, t - (CONV_WIDTH - 1):]
    pad = (-t) % chunk
    pad3 = lambda arr: jnp.pad(arr.reshape(b, t, -1), ((0, 0), (0, pad), (0, 0)))
    b_out, s_fin = _delta(pad3(raw), hist, conv_w, pad3(gb), pad3(z), dnw, s0, chunk)
    b_out = b_out[:, :t].reshape(b * t, B_HEADS * B_DIM)
    h, hnt = _proj_out(xf, a_out.reshape(b * t, A_WIDTH), b_out, wts, tm)
    sel = _peer_select(hnt, wts, tt_sel)
    y = _peer_dense(hnt, h, sel, wts, tt_dense)
    return y.reshape(b, t, D_MODEL), k_buf, v_buf, conv_state, s_fin


def kernel(x_prompt, x_sample, cache_k_win, cache_v_win, state_conv, state_delta, norm_mix_w, w_in, q_norm_w, k_norm_w, conv_w, a_log, dt_bias, delta_norm_w, w_out, norm_ffn_w, w_pq, sub_keys, expert_u, expert_v):
    assert w_in.shape[0] == 1, "one layer"
    bp, sp, _ = x_prompt.shape
    bs, ss, _ = x_sample.shape
    wts = _prep_weights(norm_mix_w[0], w_in[0], q_norm_w[0], k_norm_w[0], a_log[0], dt_bias[0], w_out[0], norm_ffn_w[0],
                        w_pq[0], sub_keys[0], expert_u[0], expert_v[0])
    cw = conv_w[0]
    dnw = delta_norm_w[0].reshape(1, B_DIM)
    lane_head = jnp.arange(A_WIDTH) // A_HEAD_DIM
    head_sum = (lane_head[:, None] == jnp.arange(128)[None, :]).astype(F32)
    head_expand = head_sum.T

    tm_p = 256
    cos_p, sin_p = _rope_tables(jnp.arange(sp), 1)

    def mixer_prompt(q, k, v):
        return _attn_prompt(q, k, v), k, v

    yp, kp, vp, cp, dp = _layer(
        x_prompt, cos_p, sin_p, sp // tm_p, wts, cw, dnw, mixer_prompt,
        jnp.zeros((bp, 8, B_QKV), F32), jnp.zeros((bp, B_HEADS, B_DIM, B_DIM), F32),
        chunk=64, tm=tm_p, tt_sel=256, tt_dense=512)

    n_s = bs * ss
    cos_s, sin_s = _rope_tables(PAST_LEN + jnp.arange(ss), bs)

    def mixer_sample(q, k, v):
        kc = cache_k_win[0].reshape(bs, -1, A_WIDTH)
        vc = cache_v_win[0].reshape(bs, -1, A_WIDTH)
        return _attn_sample(q, k, v, kc, vc, head_sum, head_expand)

    hist_s = jnp.pad(state_conv[0], ((0, 0), (8 - (CONV_WIDTH - 1), 0), (0, 0)))
    ys, ks, vs, cs, ds = _layer(
        x_sample, cos_s, sin_s, 1, wts, cw, dnw, mixer_sample, hist_s, state_delta[0],
        chunk=8, tm=n_s, tt_sel=n_s, tt_dense=n_s)

    win_p = min(BAND * max(DILATIONS), sp)
    shape_kv = lambda arr, b: arr.reshape(1, b, -1, A_HEADS, A_HEAD_DIM)
    return (yp, ys,
            shape_kv(kp[:, sp - win_p:], bp), shape_kv(vp[:, sp - win_p:], bp), cp[None], dp[None],
            shape_kv(ks, bs), shape_kv(vs, bs), cs[None], ds[None])
```

```python
import functools
import math

import jax
import jax.numpy as jnp
from jax import lax
from jax.experimental import pallas as pl
from jax.experimental.pallas import tpu as pltpu

F32 = jnp.float32
BF16 = jnp.bfloat16
U32 = jnp.uint32
HIGHEST = lax.Precision.HIGHEST

D_MODEL = 1024
A_HEADS = 8
A_HEAD_DIM = 64
A_WIDTH = A_HEADS * A_HEAD_DIM
DILATIONS = (1, 4, 16)
BAND = 128
ROPE_THETA = 10000.0
PAST_LEN = 8192
B_HEADS = 4
B_DIM = 128
B_QKV = 3 * B_HEADS * B_DIM
CONV_WIDTH = 4
PEER_HEADS = 8
PEER_KEYS = 128
PEER_TOPK = 16
EPS = 1e-6
NEG_INF = float("-inf")
VMEM_LIMIT = 56 * 1024 * 1024


def _params(*sem, flags=None):
    return pltpu.CompilerParams(dimension_semantics=sem, vmem_limit_bytes=VMEM_LIMIT, flags=flags)


def _dot(a, b, **kw):
    return jnp.dot(a, b, preferred_element_type=F32, **kw)


def _dot_nt(a, b, **kw):
    return lax.dot_general(a, b, (((1,), (1,)), ((), ())), preferred_element_type=F32, **kw)


def _dot_tn(a, b, **kw):
    return lax.dot_general(a, b, (((0,), (0,)), ((), ())), preferred_element_type=F32, **kw)


def _sigmoid(x):
    return 1.0 / (1.0 + jnp.exp(-x))


def _proj_in_kernel(x_ref, nw_ref, wa_ref, wb_ref, wz_ref, wg_ref, qnw_ref, knw_ref, cos_ref, sin_ref,
                    hm_ref, alog_ref, dtb_ref, q_ref, k_ref, v_ref, raw_ref, z_ref, gb_ref):
    x = x_ref[...]
    ms = jnp.mean(x * x, axis=-1, keepdims=True)
    xn = (x * lax.rsqrt(ms + EPS) * nw_ref[...]).astype(BF16)
    a = _dot(xn, wa_ref[...])
    hm = hm_ref[...]
    lane = lax.broadcasted_iota(jnp.int32, (1, A_WIDTH), 1)
    first_half = (lane & (A_HEAD_DIM // 2)) == 0
    cos = cos_ref[...]
    sin = sin_ref[...]

    def norm_rope(t, w):
        t2 = t * t
        hi = t2.astype(BF16)
        lo = (t2 - hi.astype(F32)).astype(BF16)
        ss = _dot(hi, hm) + _dot(lo, hm)
        tn = t * lax.rsqrt(ss * (1.0 / A_HEAD_DIM) + EPS) * w
        partner = jnp.where(first_half, pltpu.roll(tn, A_WIDTH - A_HEAD_DIM // 2, 1), pltpu.roll(tn, A_HEAD_DIM // 2, 1))
        return tn * cos + partner * sin

    q_ref[...] = norm_rope(a[:, :A_WIDTH], qnw_ref[...])
    k_ref[...] = norm_rope(a[:, A_WIDTH:2 * A_WIDTH], knw_ref[...])
    v_ref[...] = a[:, 2 * A_WIDTH:]
    raw_ref[...] = _dot(xn, wb_ref[...])
    z_ref[...] = _dot(xn, wz_ref[...])
    zg = _dot(xn, wg_ref[...])
    beta = _sigmoid(zg)
    t = zg + dtb_ref[...]
    softplus = jnp.maximum(t, 0.0) + jnp.log1p(jnp.exp(-jnp.abs(t)))
    g = -jnp.exp(alog_ref[...]) * softplus
    glane = lax.broadcasted_iota(jnp.int32, (1, B_HEADS * B_DIM), 1) & (B_DIM - 1)
    gb_ref[...] = jnp.where(glane == 0, beta, jnp.where(glane == 1, g, 0.0))


def _proj_in(x, pos_cos, pos_sin, pos_blocks, wts, tm):
    t = x.shape[0]
    n = t // tm
    row = lambda i: (i, 0)
    const = lambda i: (0, 0)
    posmap = lambda i: (i % pos_blocks, 0)
    full = lambda a: pl.BlockSpec(a.shape, const)
    outs = [(A_WIDTH, "q"), (A_WIDTH, "k"), (A_WIDTH, "v"), (B_QKV, "raw"), (B_HEADS * B_DIM, "z"), (B_HEADS * B_DIM, "gb")]
    return pl.pallas_call(
        _proj_in_kernel,
        grid=(n,),
        in_specs=[pl.BlockSpec((tm, D_MODEL), row), full(wts["norm_mix"]), full(wts["wa"]), full(wts["wb"]), full(wts["wz"]),
                  full(wts["wg"]), full(wts["qnw"]), full(wts["knw"]),
                  pl.BlockSpec((tm, A_WIDTH), posmap), pl.BlockSpec((tm, A_WIDTH), posmap),
                  full(wts["headmat"]), full(wts["alog"]), full(wts["dtb"])],
        out_specs=[pl.BlockSpec((tm, w), row) for w, _ in outs],
        out_shape=[jax.ShapeDtypeStruct((t, w), F32) for w, _ in outs],
        compiler_params=_params("parallel"),
        name="proj_in",
    )(x, wts["norm_mix"], wts["wa"], wts["wb"], wts["wz"], wts["wg"], wts["qnw"], wts["knw"], pos_cos, pos_sin,
      wts["headmat"], wts["alog"], wts["dtb"])


def _attn_prompt_kernel(q_ref, k_ref, v_ref, o_ref, ob_ref, lb_ref, *, seq):
    lane = lax.broadcasted_iota(jnp.int32, (BAND, 2 * A_HEAD_DIM), 1)
    head0 = lane < A_HEAD_DIM
    qi = lax.broadcasted_iota(jnp.int32, (BAND, 2 * BAND), 0)
    kj = lax.broadcasted_iota(jnp.int32, (BAND, 2 * BAND), 1)
    dist = BAND + qi - kj
    in_band = (dist >= 0) & (dist <= BAND)
    causal = in_band[:, BAND:]
    scale = 1.0 / math.sqrt(A_HEAD_DIM)

    for br, dil in enumerate(DILATIONS):
        nblk = seq // (BAND * dil)
        shift = dil.bit_length() - 1

        def rows(ref, start):
            if dil == 1:
                return ref[pl.ds(start, BAND), :]
            return ref[pl.ds(start, BAND, stride=dil), :]

        def unit(u, carry):
            r = u & (dil - 1)
            n = u >> shift
            start = r + n * (BAND * dil)
            qb = rows(q_ref, start)
            k_own = rows(k_ref, start).astype(BF16)
            v_own = rows(v_ref, start).astype(BF16)
            if nblk > 1:
                pstart = jnp.maximum(start - BAND * dil, r)
                kk = jnp.concatenate([rows(k_ref, pstart).astype(BF16), k_own], axis=0)
                vv = jnp.concatenate([rows(v_ref, pstart).astype(BF16), v_own], axis=0)
                mask = in_band & (kj >= jnp.where(n > 0, 0, BAND))
            else:
                kk, vv, mask = k_own, v_own, causal
            outs = []
            lses = []
            for hmask in (head0, ~head0):
                qh = jnp.where(hmask, qb, 0.0).astype(BF16)
                s = _dot_nt(qh, kk) * scale
                s = jnp.where(mask, s, NEG_INF)
                mx = jnp.max(s, axis=-1, keepdims=True)
                p = jnp.exp(s - mx)
                l = jnp.sum(p, axis=-1, keepdims=True)
                outs.append(_dot(p.astype(BF16), vv) / l)
                lses.append(mx + jnp.log(l))
            out = jnp.where(head0, outs[0], outs[1])
            lse = jnp.where(head0, lses[0], lses[1])
            if dil == 1:
                ob_ref[br, pl.ds(start, BAND), :] = out
                lb_ref[br, pl.ds(start, BAND), :] = lse
            else:
                ob_ref[br, pl.ds(start, BAND, stride=dil), :] = out
                lb_ref[br, pl.ds(start, BAND, stride=dil), :] = lse
            return carry

        lax.fori_loop(0, seq // BAND, unit, 0, unroll=2)

    def merge(c, carry):
        rs = pl.ds(pl.multiple_of(c * 256, 256), 256)
        l0, l1, l2 = lb_ref[0, rs, :], lb_ref[1, rs, :], lb_ref[2, rs, :]
        m = jnp.maximum(jnp.maximum(l0, l1), l2)
        w0, w1, w2 = jnp.exp(l0 - m), jnp.exp(l1 - m), jnp.exp(l2 - m)
        o_ref[rs, :] = (w0 * ob_ref[0, rs, :] + w1 * ob_ref[1, rs, :] + w2 * ob_ref[2, rs, :]) / (w0 + w1 + w2)
        return carry

    lax.fori_loop(0, seq // 256, merge, 0)


def _attn_prompt(q, k, v):
    b, s, _ = q.shape
    spec = pl.BlockSpec((None, s, 2 * A_HEAD_DIM), lambda i, j: (i, 0, j))
    return pl.pallas_call(
        functools.partial(_attn_prompt_kernel, seq=s),
        grid=(b, A_HEADS // 2),
        in_specs=[spec, spec, spec],
        out_specs=spec,
        out_shape=jax.ShapeDtypeStruct((b, s, A_WIDTH), F32),
        scratch_shapes=[pltpu.VMEM((len(DILATIONS), s, 2 * A_HEAD_DIM), F32),
                        pltpu.VMEM((len(DILATIONS), s, 2 * A_HEAD_DIM), F32)],
        compiler_params=_params("parallel", "parallel"),
        name="attn_prompt",
    )(q, k, v)


def _attn_sample_kernel(q_ref, kn_ref, vn_ref, kc_ref, vc_ref, hs_ref, he_ref, o_ref, ko_ref, vo_ref, kbuf, vbuf,
                        *, win, new):
    nlb = A_WIDTH // 128
    for buf, cache, fresh, shifted in ((kbuf, kc_ref, kn_ref, ko_ref), (vbuf, vc_ref, vn_ref, vo_ref)):
        for j in range(nlb):
            ls = slice(j * 128, (j + 1) * 128)
            buf[j, 0:win, :] = cache[:, ls]
            buf[j, win:win + new, :] = fresh[:, ls]
            shifted[:, ls] = buf[j, new:new + win, :]

    def gather(buf, start, size, stride):
        if stride == 1:
            return jnp.concatenate([buf[j, pl.ds(start, size), :] for j in range(nlb)], axis=1)
        return jnp.concatenate([buf[j, pl.ds(start, size, stride=stride), :] for j in range(nlb)], axis=1)

    hs = hs_ref[...]
    he = he_ref[...]
    scale = 1.0 / math.sqrt(A_HEAD_DIM)
    for l in range(new):
        ql = q_ref[l:l + 1, :]
        outs, lses = [], []
        for dil in DILATIONS:
            start = win + l - BAND * dil
            ks = gather(kbuf, start, BAND, dil)
            vs = gather(vbuf, start, BAND, dil)
            kself = gather(kbuf, win + l, 1, 1)
            vself = gather(vbuf, win + l, 1, 1)
            s = _dot(ks * ql, hs, precision=HIGHEST) * scale
            sself = _dot(kself * ql, hs, precision=HIGHEST) * scale
            mx = jnp.maximum(jnp.max(s, axis=0, keepdims=True), sself)
            p = jnp.exp(s - mx)
            pself = jnp.exp(sself - mx)
            lsum = jnp.sum(p, axis=0, keepdims=True) + pself
            pe = _dot(p, he, precision=HIGHEST)
            o = jnp.sum(pe * vs, axis=0, keepdims=True) + _dot(pself, he, precision=HIGHEST) * vself
            outs.append(o / _dot(lsum, he, precision=HIGHEST))
            lses.append(_dot(mx + jnp.log(lsum), he, precision=HIGHEST))
        m = jnp.maximum(jnp.maximum(lses[0], lses[1]), lses[2])
        w = [jnp.exp(ls - m) for ls in lses]
        o_ref[l:l + 1, :] = (w[0] * outs[0] + w[1] * outs[1] + w[2] * outs[2]) / (w[0] + w[1] + w[2])


def _attn_sample(q, kn, vn, kc, vc, hs, he):
    b, new, _ = q.shape
    win = kc.shape[1]
    assert win >= BAND * max(DILATIONS)
    small = pl.BlockSpec((None, new, A_WIDTH), lambda i: (i, 0, 0))
    big = pl.BlockSpec((None, win, A_WIDTH), lambda i: (i, 0, 0))
    const = lambda a: pl.BlockSpec(a.shape, lambda i: (0, 0))
    return pl.pallas_call(
        functools.partial(_attn_sample_kernel, win=win, new=new),
        grid=(b,),
        in_specs=[small, small, small, big, big, const(hs), const(he)],
        out_specs=[small, big, big],
        out_shape=[jax.ShapeDtypeStruct((b, new, A_WIDTH), F32), jax.ShapeDtypeStruct((b, win, A_WIDTH), F32),
                   jax.ShapeDtypeStruct((b, win, A_WIDTH), F32)],
        scratch_shapes=[pltpu.VMEM((A_WIDTH // 128, win + 8, 128), F32), pltpu.VMEM((A_WIDTH // 128, win + 8, 128), F32)],
        compiler_params=_params("parallel"),
        name="attn_sample",
    )(q, kn, vn, kc, vc, hs, he)


def _delta_kernel(rq_ref, rk_ref, rv_ref, hq_ref, hk_ref, hv_ref, wq_ref, wk_ref, wv_ref, gb_ref, z_ref, dnw_ref, s0_ref,
                  o_ref, sf_ref, ext, s_scr, *, seq, chunk):
    c = chunk
    for i, (r_ref, h_ref) in enumerate(((rq_ref, hq_ref), (rk_ref, hk_ref), (rv_ref, hv_ref))):
        ext[i, 0:8, :] = h_ref[...]
        ext[i, 8:8 + seq, :] = r_ref[...]
    s_scr[...] = s0_ref[...]
    ii = lax.broadcasted_iota(jnp.int32, (c, c), 0)
    jj = lax.broadcasted_iota(jnp.int32, (c, c), 1)
    causal = ii >= jj
    strict = ii > jj
    eye = (ii == jj).astype(F32)
    tri = causal.astype(F32)
    ones = jnp.ones((c, c), F32)
    dnw = dnw_ref[...]

    def conv_silu(i, w_ref, base):
        y = w_ref[0:1, :] * ext[i, pl.ds(base + 5, c), :]
        for t in range(1, CONV_WIDTH):
            y = y + w_ref[t:t + 1, :] * ext[i, pl.ds(base + 5 + t, c), :]
        return y * _sigmoid(y)

    def l2n(t):
        return t * lax.rsqrt(jnp.sum(t * t, axis=-1, keepdims=True) + EPS)

    def body(n, carry):
        base = pl.multiple_of(n * c, c)
        qc = l2n(conv_silu(0, wq_ref, base)) * (B_DIM ** -0.5)
        kc = l2n(conv_silu(1, wk_ref, base))
        vc = conv_silu(2, wv_ref, base)
        gbc = gb_ref[pl.ds(base, c), :]
        beta = gbc[:, 0:1]
        g = gbc[:, 1:2]
        gsq = jnp.broadcast_to(g, (c, c))
        gcol = _dot(tri, gsq, precision=HIGHEST)
        grow = _dot(ones, jnp.where(ii <= jj, gsq, 0.0), precision=HIGHEST)
        gcum = gcol[:, 0:1]
        glast = gcol[c - 1:c, 0:1]
        decay = jnp.exp(jnp.where(causal, gcol - grow, NEG_INF))
        kb = kc * beta
        kbh, kch, qch = kb.astype(BF16), kc.astype(BF16), qc.astype(BF16)
        lower = jnp.where(strict, _dot_nt(kbh, kch) * decay, 0.0)
        tinv = eye - lower
        lp = lower
        span = 2
        while span < c:
            lp = _dot(lp, lp, precision=HIGHEST)
            tinv = tinv + _dot(tinv, lp, precision=HIGHEST)
            span *= 2
        eg = jnp.exp(gcum)
        u = _dot(tinv, vc * beta, precision=HIGHEST)
        w = _dot(tinv, kb * eg, precision=HIGHEST)
        intra = jnp.where(causal, _dot_nt(qch, kch) * decay, 0.0)
        state = s_scr[...]
        sh = state.astype(BF16)
        v_new = u - _dot(w.astype(BF16), sh)
        out = _dot((qc * eg).astype(BF16), sh) + _dot(intra.astype(BF16), v_new.astype(BF16))
        kdec = kc * jnp.exp(glast - gcum)
        s_scr[...] = state * jnp.exp(glast) + _dot_tn(kdec.astype(BF16), v_new.astype(BF16))
        on = out * lax.rsqrt(jnp.mean(out * out, axis=-1, keepdims=True) + EPS) * dnw
        zc = z_ref[pl.ds(base, c), :]
        o_ref[pl.ds(base, c), :] = on * (zc * _sigmoid(zc))
        return carry

    lax.fori_loop(0, seq // c, body, 0)
    sf_ref[...] = s_scr[...]


def _delta(raw, hist, conv_w, gb, z, dnw, s0, chunk):
    b, seq, _ = raw.shape
    rspec = lambda off: pl.BlockSpec((None, seq, B_DIM), lambda i, h: (i, 0, off + h))
    hspec = lambda off: pl.BlockSpec((None, 8, B_DIM), lambda i, h: (i, 0, off + h))
    wspec = lambda off: pl.BlockSpec((CONV_WIDTH, B_DIM), lambda i, h: (0, off + h))
    tok = pl.BlockSpec((None, seq, B_DIM), lambda i, h: (i, 0, h))
    st = pl.BlockSpec((None, None, B_DIM, B_DIM), lambda i, h: (i, h, 0, 0))
    return pl.pallas_call(
        functools.partial(_delta_kernel, seq=seq, chunk=chunk),
        grid=(b, B_HEADS),
        in_specs=[rspec(0), rspec(B_HEADS), rspec(2 * B_HEADS), hspec(0), hspec(B_HEADS), hspec(2 * B_HEADS),
                  wspec(0), wspec(B_HEADS), wspec(2 * B_HEADS), tok, tok, pl.BlockSpec((1, B_DIM), lambda i, h: (0, 0)), st],
        out_specs=[tok, st],
        out_shape=[jax.ShapeDtypeStruct((b, seq, B_HEADS * B_DIM), F32), jax.ShapeDtypeStruct((b, B_HEADS, B_DIM, B_DIM), F32)],
        scratch_shapes=[pltpu.VMEM((3, seq + 8, B_DIM), F32), pltpu.VMEM((B_DIM, B_DIM), F32)],
        compiler_params=_params("parallel", "parallel"),
        name="delta",
    )(raw, raw, raw, hist, hist, hist, conv_w, conv_w, conv_w, gb, z, dnw, s0)


def _proj_out_kernel(x_ref, a_ref, b_ref, wa_ref, wb_ref, nw_ref, h_ref, hnt_ref):
    h = x_ref[...] + _dot(a_ref[...].astype(BF16), wa_ref[...]) + _dot(b_ref[...].astype(BF16), wb_ref[...])
    h_ref[...] = h
    hn = h * lax.rsqrt(jnp.mean(h * h, axis=-1, keepdims=True) + EPS) * nw_ref[...]
    hnt_ref[...] = hn.T.astype(BF16)


def _proj_out(x, a, b, wts, tm):
    t = x.shape[0]
    row = lambda i: (i, 0)
    full = lambda arr: pl.BlockSpec(arr.shape, lambda i: (0, 0))
    return pl.pallas_call(
        _proj_out_kernel,
        grid=(t // tm,),
        in_specs=[pl.BlockSpec((tm, D_MODEL), row), pl.BlockSpec((tm, A_WIDTH), row), pl.BlockSpec((tm, A_WIDTH), row),
                  full(wts["wo_a"]), full(wts["wo_b"]), full(wts["norm_ffn"])],
        out_specs=[pl.BlockSpec((tm, D_MODEL), row), pl.BlockSpec((D_MODEL, tm), lambda i: (0, i))],
        out_shape=[jax.ShapeDtypeStruct((t, D_MODEL), F32), jax.ShapeDtypeStruct((D_MODEL, t), BF16)],
        compiler_params=_params("parallel"),
        name="proj_out",
    )(x, a, b, wts["wo_a"], wts["wo_b"], wts["norm_ffn"])


def _top16_ranked(x):
    t = x.shape[1]
    slot = lax.broadcasted_iota(jnp.int32, (PEER_TOPK, t), 0).astype(F32)
    vals = jnp.full((PEER_TOPK, t), NEG_INF, F32)
    rank = jnp.full(x.shape, float(PEER_TOPK), F32)
    filled = jnp.zeros((1, t), F32)
    for _ in range(PEER_TOPK):
        m = jnp.max(x, axis=0, keepdims=True)
        eq = x == m
        cnt = jnp.sum(jnp.where(eq, 1.0, 0.0), axis=0, keepdims=True)
        vals = jnp.where((slot >= filled) & (slot < filled + cnt), m, vals)
        rank = jnp.where(eq, jnp.minimum(filled, float(PEER_TOPK)), rank)
        filled = filled + cnt
        x = jnp.where(eq, NEG_INF, x)
    return vals, rank


def _kth_largest(x, k):
    t = x.shape[1]
    kth = jnp.full((1, t), NEG_INF, F32)
    filled = jnp.zeros((1, t), F32)
    for _ in range(k):
        m = jnp.max(x, axis=0, keepdims=True)
        eq = x == m
        cnt = jnp.sum(jnp.where(eq, 1.0, 0.0), axis=0, keepdims=True)
        kth = jnp.where((filled < k) & (filled + cnt >= k), m, kth)
        filled = filled + cnt
        x = jnp.where(eq, NEG_INF, x)
    return kth


def _peer_select_kernel(hnt_ref, wpq_ref, sk_ref, n1_ref, q1_ref, r2_ref, q2_ref, sub_scr, top_scr, rank_scr):
    hnt = hnt_ref[...]

    def half(hc, carry):
        row0 = pl.multiple_of(hc * PEER_KEYS, PEER_KEYS)
        qt = _dot(wpq_ref[pl.ds(row0, PEER_KEYS), :], hnt)
        sub = _dot(sk_ref[hc & 1], qt.astype(BF16))
        sub_scr[hc] = sub
        top_scr[hc], rank_scr[hc] = _top16_ranked(sub)
        return carry

    lax.fori_loop(0, 2 * PEER_HEADS, half, 0, unroll=2)

    def head(h, carry):
        a = top_scr[2 * h]
        b = top_scr[2 * h + 1]
        pieces = [a[0:1] + b] + [a[i:i + 1] + b[0:8] for i in range(1, 8)] + [a[8:16] + b[0:1]]
        cand = jnp.concatenate(pieces, axis=0)
        tau = _kth_largest(cand, PEER_TOPK)
        top = a[0:1] + b[0:1]
        zsum = jnp.sum(jnp.where(cand >= tau, jnp.exp(cand - top), 0.0), axis=0, keepdims=True)
        cnt = jnp.zeros_like(a)
        for j in range(PEER_TOPK):
            cnt = cnt + jnp.where(a + b[j:j + 1] >= tau, 1.0, 0.0)
        rank1 = rank_scr[2 * h]
        n1 = jnp.zeros_like(rank1)
        for i in range(PEER_TOPK):
            n1 = jnp.where(rank1 == float(i), cnt[i:i + 1], n1)
        n1_ref[h] = n1
        r2_ref[h] = rank_scr[2 * h + 1].astype(BF16)
        q1_ref[h] = jnp.exp(sub_scr[2 * h] - a[0:1]) / zsum
        q2_ref[h] = jnp.exp(sub_scr[2 * h + 1] - b[0:1]).astype(BF16)
        return carry

    lax.fori_loop(0, PEER_HEADS, head, 0)


def _peer_select(hnt, wts, tt):
    t = hnt.shape[1]
    out = pl.BlockSpec((PEER_HEADS, PEER_KEYS, tt), lambda i: (0, 0, i))
    shape = lambda dt: jax.ShapeDtypeStruct((PEER_HEADS, PEER_KEYS, t), dt)
    return pl.pallas_call(
        _peer_select_kernel,
        grid=(t // tt,),
        in_specs=[pl.BlockSpec((D_MODEL, tt), lambda i: (0, i)),
                  pl.BlockSpec(wts["wpq_t"].shape, lambda i: (0, 0)),
                  pl.BlockSpec(wts["sub_keys"].shape, lambda i: (0, 0, 0))],
        out_specs=[out, out, out, out],
        out_shape=[shape(F32), shape(F32), shape(BF16), shape(BF16)],
        scratch_shapes=[pltpu.VMEM((2 * PEER_HEADS, PEER_KEYS, tt), F32), pltpu.VMEM((2 * PEER_HEADS, PEER_TOPK, tt), F32),
                        pltpu.VMEM((2 * PEER_HEADS, PEER_KEYS, tt), F32)],
        compiler_params=_params("parallel"),
        name="peer_select",
    )(hnt, wts["wpq_t"], wts["sub_keys"])


A_PER_TILE = 8


def _rows_bf16(row):
    return jnp.broadcast_to(row, (PEER_KEYS, 128)).astype(BF16)


def _peer_dense_kernel(hnt_ref, u_ref, vt_ref, n1_ref, q1_ref, r2_ref, q2_ref, h_ref, y_ref, acc_ref, pre_ref, w_ref,
                       r2_scr, q2_scr, *, tt):
    j = pl.program_id(1)

    @pl.when(j == 0)
    def _():
        acc_ref[...] = jnp.zeros_like(acc_ref)
        r2_scr[...] = r2_ref[...]
        q2_scr[...] = q2_ref[...]

    sub = min(tt, 256)
    zero = jnp.zeros((PEER_KEYS, 128), BF16)
    for s in range(tt // sub):
        ts = slice(s * sub, (s + 1) * sub)
        pre_ref[:, ts] = _dot(u_ref[...], hnt_ref[:, ts]).astype(BF16)
    group = 2
    for s in range(tt // sub):
        for ia0 in range(0, A_PER_TILE, group):
            for tc in range(sub // 128):
                ls = slice(s * sub + tc * 128, s * sub + (tc + 1) * 128)
                gates = [zero] * group
                for h in range(PEER_HEADS):
                    r2 = r2_scr[h, :, ls]
                    q2 = q2_scr[h, :, ls]
                    for k in range(group):
                        n1 = _rows_bf16(n1_ref[h, ia0 + k:ia0 + k + 1, ls])
                        q1 = _rows_bf16(q1_ref[h, ia0 + k:ia0 + k + 1, ls])
                        gates[k] = gates[k] + jnp.where(r2 < n1, q2, zero) * q1
                for k in range(group):
                    rows = slice((ia0 + k) * PEER_KEYS, (ia0 + k + 1) * PEER_KEYS)
                    p = pre_ref[rows, ls]
                    act = (0.5 * p) * (1.0 + lax.erf(p * (1.0 / math.sqrt(2.0))))
                    w_ref[rows, ls] = gates[k] * act
    for s in range(tt // sub):
        ts = slice(s * sub, (s + 1) * sub)
        acc_ref[:, ts] += _dot(vt_ref[...], w_ref[:, ts])

    @pl.when(j == pl.num_programs(1) - 1)
    def _():
        y_ref[...] = h_ref[...] + acc_ref[...].T


def _peer_dense(hnt, h, sel, wts, tt):
    t = hnt.shape[1]
    n1, q1, r2, q2 = sel
    et = A_PER_TILE * PEER_KEYS
    a_spec = pl.BlockSpec((PEER_HEADS, A_PER_TILE, tt), lambda i, j: (0, j, i))
    b_spec = pl.BlockSpec((PEER_HEADS, PEER_KEYS, tt), lambda i, j: (0, 0, i))
    return pl.pallas_call(
        functools.partial(_peer_dense_kernel, tt=tt),
        grid=(t // tt, PEER_KEYS // A_PER_TILE),
        in_specs=[pl.BlockSpec((D_MODEL, tt), lambda i, j: (0, i)),
                  pl.BlockSpec((et, D_MODEL), lambda i, j: (j, 0)),
                  pl.BlockSpec((D_MODEL, et), lambda i, j: (0, j)),
                  a_spec, a_spec, b_spec, b_spec,
                  pl.BlockSpec((tt, D_MODEL), lambda i, j: (i, 0))],
        out_specs=pl.BlockSpec((tt, D_MODEL), lambda i, j: (i, 0)),
        out_shape=jax.ShapeDtypeStruct((t, D_MODEL), F32),
        scratch_shapes=[pltpu.VMEM((D_MODEL, tt), F32), pltpu.VMEM((et, tt), BF16), pltpu.VMEM((et, tt), BF16),
                        pltpu.VMEM((PEER_HEADS, PEER_KEYS, tt), BF16), pltpu.VMEM((PEER_HEADS, PEER_KEYS, tt), BF16)],
        compiler_params=_params("parallel", "arbitrary"),
        name="peer_dense",
    )(hnt, wts["u"], wts["v_t"], n1, q1, r2, q2, h)


def _rope_tables(pos, reps):
    half = A_HEAD_DIM // 2
    inv_freq = ROPE_THETA ** (-jnp.arange(half, dtype=F32) / half)
    ang = pos.astype(F32)[:, None] * inv_freq[None, :]
    cos = jnp.cos(ang)
    sin = jnp.sin(ang)
    cos_t = jnp.tile(jnp.concatenate([cos, cos], axis=-1), (reps, A_HEADS))
    sin_t = jnp.tile(jnp.concatenate([-sin, sin], axis=-1), (reps, A_HEADS))
    return cos_t, sin_t


def _prep_weights(norm_mix_w, w_in, q_norm_w, k_norm_w, a_log, dt_bias, w_out, norm_ffn_w, w_pq, sub_keys, expert_u, expert_v):
    na = 3 * A_WIDTH
    hb = B_HEADS * B_DIM
    wgate = w_in[:, na + B_QKV + hb:]
    wg = jnp.zeros((D_MODEL, B_HEADS, B_DIM), F32)
    wg = wg.at[:, :, 0].set(wgate[:, :B_HEADS]).at[:, :, 1].set(wgate[:, B_HEADS:])
    lane_head = jnp.arange(A_WIDTH) // A_HEAD_DIM
    put1 = lambda vec: jnp.zeros((B_HEADS, B_DIM), F32).at[:, 1].set(vec).reshape(1, hb)
    return {
        "norm_mix": norm_mix_w.reshape(1, D_MODEL),
        "wa": w_in[:, :na].astype(BF16),
        "wb": w_in[:, na:na + B_QKV].astype(BF16),
        "wz": w_in[:, na + B_QKV:na + B_QKV + hb].astype(BF16),
        "wg": wg.reshape(D_MODEL, hb).astype(BF16),
        "qnw": jnp.tile(q_norm_w, A_HEADS).reshape(1, A_WIDTH),
        "knw": jnp.tile(k_norm_w, A_HEADS).reshape(1, A_WIDTH),
        "headmat": (lane_head[:, None] == lane_head[None, :]).astype(BF16),
        "alog": put1(a_log),
        "dtb": put1(dt_bias),
        "wo_a": w_out[:A_WIDTH].astype(BF16),
        "wo_b": w_out[A_WIDTH:].astype(BF16),
        "norm_ffn": norm_ffn_w.reshape(1, D_MODEL),
        "wpq_t": w_pq.T.astype(BF16),
        "sub_keys": sub_keys.astype(BF16),
        "u": expert_u.astype(BF16),
        "v_t": expert_v.T.astype(BF16),
    }


def _layer(x, pos_cos, pos_sin, pos_blocks, wts, conv_w, dnw, mixer_a, hist, s0, chunk, tm, tt_sel, tt_dense):
    b, t, _ = x.shape
    xf = x.reshape(b * t, D_MODEL)
    q, k, v, raw, z, gb = _proj_in(xf, pos_cos, pos_sin, pos_blocks, wts, tm)
    a_out, k_buf, v_buf = mixer_a(q.reshape(b, t, A_WIDTH), k.reshape(b, t, A_WIDTH), v.reshape(b, t, A_WIDTH))
    raw = raw.reshape(b, t, B_QKV)
    conv_state = raw[:, t - (CONV_WIDTH - 1):]
    pad = (-t) % chunk
    pad3 = lambda arr: jnp.pad(arr.reshape(b, t, -1), ((0, 0), (0, pad), (0, 0)))
    b_out, s_fin = _delta(pad3(raw), hist, conv_w, pad3(gb), pad3(z), dnw, s0, chunk)
    b_out = b_out[:, :t].reshape(b * t, B_HEADS * B_DIM)
    h, hnt = _proj_out(xf, a_out.reshape(b * t, A_WIDTH), b_out, wts, tm)
    sel = _peer_select(hnt, wts, tt_sel)
    y = _peer_dense(hnt, h, sel, wts, tt_dense)
    return y.reshape(b, t, D_MODEL), k_buf, v_buf, conv_state, s_fin


def kernel(x_prompt, x_sample, cache_k_win, cache_v_win, state_conv, state_delta, norm_mix_w, w_in, q_norm_w, k_norm_w, conv_w, a_log, dt_bias, delta_norm_w, w_out, norm_ffn_w, w_pq, sub_keys, expert_u, expert_v):
    assert w_in.shape[0] == 1, "one layer"
    bp, sp, _ = x_prompt.shape
    bs, ss, _ = x_sample.shape
    wts = _prep_weights(norm_mix_w[0], w_in[0], q_norm_w[0], k_norm_w[0], a_log[0], dt_bias[0], w_out[0], norm_ffn_w[0],
                        w_pq[0], sub_keys[0], expert_u[0], expert_v[0])
    cw = conv_w[0]
    dnw = delta_norm_w[0].reshape(1, B_DIM)
    lane_head = jnp.arange(A_WIDTH) // A_HEAD_DIM
    head_sum = (lane_head[:, None] == jnp.arange(128)[None, :]).astype(F32)
    head_expand = head_sum.T

    tm_p = 256
    cos_p, sin_p = _rope_tables(jnp.arange(sp), 1)

    def mixer_prompt(q, k, v):
        return _attn_prompt(q, k, v), k, v

    yp, kp, vp, cp, dp = _layer(
        x_prompt, cos_p, sin_p, sp // tm_p, wts, cw, dnw, mixer_prompt,
        jnp.zeros((bp, 8, B_QKV), F32), jnp.zeros((bp, B_HEADS, B_DIM, B_DIM), F32),
        chunk=64, tm=tm_p, tt_sel=256, tt_dense=1024)

    n_s = bs * ss
    cos_s, sin_s = _rope_tables(PAST_LEN + jnp.arange(ss), bs)

    def mixer_sample(q, k, v):
        kc = cache_k_win[0].reshape(bs, -1, A_WIDTH)
        vc = cache_v_win[0].reshape(bs, -1, A_WIDTH)
        return _attn_sample(q, k, v, kc, vc, head_sum, head_expand)

    hist_s = jnp.pad(state_conv[0], ((0, 0), (8 - (CONV_WIDTH - 1), 0), (0, 0)))
    ys, ks, vs, cs, ds = _layer(
        x_sample, cos_s, sin_s, 1, wts, cw, dnw, mixer_sample, hist_s, state_delta[0],
        chunk=8, tm=n_s, tt_sel=n_s, tt_dense=n_s)

    win_p = min(BAND * max(DILATIONS), sp)
    shape_kv = lambda arr, b: arr.reshape(1, b, -1, A_HEADS, A_HEAD_DIM)
    return (yp, ys,
            shape_kv(kp[:, sp - win_p:], bp), shape_kv(vp[:, sp - win_p:], bp), cp[None], dp[None],
            shape_kv(ks, bs), shape_kv(vs, bs), cs[None], ds[None])
```

```python
import functools
import math

import jax
import jax.numpy as jnp
from jax import lax
from jax.experimental import pallas as pl
from jax.experimental.pallas import tpu as pltpu

F32 = jnp.float32
BF16 = jnp.bfloat16
U32 = jnp.uint32
HIGHEST = lax.Precision.HIGHEST

D_MODEL = 1024
A_HEADS = 8
A_HEAD_DIM = 64
A_WIDTH = A_HEADS * A_HEAD_DIM
DILATIONS = (1, 4, 16)
BAND = 128
ROPE_THETA = 10000.0
PAST_LEN = 8192
B_HEADS = 4
B_DIM = 128
B_QKV = 3 * B_HEADS * B_DIM
CONV_WIDTH = 4
PEER_HEADS = 8
PEER_KEYS = 128
PEER_TOPK = 16
EPS = 1e-6
NEG_INF = float("-inf")
VMEM_LIMIT = 56 * 1024 * 1024


def _params(*sem, flags=None):
    return pltpu.CompilerParams(dimension_semantics=sem, vmem_limit_bytes=VMEM_LIMIT, flags=flags)


def _dot(a, b, **kw):
    return jnp.dot(a, b, preferred_element_type=F32, **kw)


def _dot_nt(a, b, **kw):
    return lax.dot_general(a, b, (((1,), (1,)), ((), ())), preferred_element_type=F32, **kw)


def _dot_tn(a, b, **kw):
    return lax.dot_general(a, b, (((0,), (0,)), ((), ())), preferred_element_type=F32, **kw)


def _sigmoid(x):
    return 1.0 / (1.0 + jnp.exp(-x))


def _proj_in_kernel(x_ref, nw_ref, wa_ref, wb_ref, wz_ref, wg_ref, qnw_ref, knw_ref, cos_ref, sin_ref,
                    hm_ref, alog_ref, dtb_ref, q_ref, k_ref, v_ref, raw_ref, z_ref, gb_ref):
    x = x_ref[...]
    ms = jnp.mean(x * x, axis=-1, keepdims=True)
    xn = (x * lax.rsqrt(ms + EPS) * nw_ref[...]).astype(BF16)
    a = _dot(xn, wa_ref[...])
    hm = hm_ref[...]
    lane = lax.broadcasted_iota(jnp.int32, (1, A_WIDTH), 1)
    first_half = (lane & (A_HEAD_DIM // 2)) == 0
    cos = cos_ref[...]
    sin = sin_ref[...]

    def norm_rope(t, w):
        t2 = t * t
        hi = t2.astype(BF16)
        lo = (t2 - hi.astype(F32)).astype(BF16)
        ss = _dot(hi, hm) + _dot(lo, hm)
        tn = t * lax.rsqrt(ss * (1.0 / A_HEAD_DIM) + EPS) * w
        partner = jnp.where(first_half, pltpu.roll(tn, A_WIDTH - A_HEAD_DIM // 2, 1), pltpu.roll(tn, A_HEAD_DIM // 2, 1))
        return tn * cos + partner * sin

    q_ref[...] = norm_rope(a[:, :A_WIDTH], qnw_ref[...])
    k_ref[...] = norm_rope(a[:, A_WIDTH:2 * A_WIDTH], knw_ref[...])
    v_ref[...] = a[:, 2 * A_WIDTH:]
    raw_ref[...] = _dot(xn, wb_ref[...])
    z_ref[...] = _dot(xn, wz_ref[...])
    zg = _dot(xn, wg_ref[...])
    beta = _sigmoid(zg)
    t = zg + dtb_ref[...]
    softplus = jnp.maximum(t, 0.0) + jnp.log1p(jnp.exp(-jnp.abs(t)))
    g = -jnp.exp(alog_ref[...]) * softplus
    glane = lax.broadcasted_iota(jnp.int32, (1, B_HEADS * B_DIM), 1) & (B_DIM - 1)
    gb_ref[...] = jnp.where(glane == 0, beta, jnp.where(glane == 1, g, 0.0))


def _proj_in(x, pos_cos, pos_sin, pos_blocks, wts, tm):
    t = x.shape[0]
    n = t // tm
    row = lambda i: (i, 0)
    const = lambda i: (0, 0)
    posmap = lambda i: (i % pos_blocks, 0)
    full = lambda a: pl.BlockSpec(a.shape, const)
    outs = [(A_WIDTH, "q"), (A_WIDTH, "k"), (A_WIDTH, "v"), (B_QKV, "raw"), (B_HEADS * B_DIM, "z"), (B_HEADS * B_DIM, "gb")]
    return pl.pallas_call(
        _proj_in_kernel,
        grid=(n,),
        in_specs=[pl.BlockSpec((tm, D_MODEL), row), full(wts["norm_mix"]), full(wts["wa"]), full(wts["wb"]), full(wts["wz"]),
                  full(wts["wg"]), full(wts["qnw"]), full(wts["knw"]),
                  pl.BlockSpec((tm, A_WIDTH), posmap), pl.BlockSpec((tm, A_WIDTH), posmap),
                  full(wts["headmat"]), full(wts["alog"]), full(wts["dtb"])],
        out_specs=[pl.BlockSpec((tm, w), row) for w, _ in outs],
        out_shape=[jax.ShapeDtypeStruct((t, w), F32) for w, _ in outs],
        compiler_params=_params("parallel"),
        name="proj_in",
    )(x, wts["norm_mix"], wts["wa"], wts["wb"], wts["wz"], wts["wg"], wts["qnw"], wts["knw"], pos_cos, pos_sin,
      wts["headmat"], wts["alog"], wts["dtb"])


def _attn_prompt_kernel(q_ref, k_ref, v_ref, o_ref, ob_ref, lb_ref, *, seq):
    lane = lax.broadcasted_iota(jnp.int32, (BAND, 2 * A_HEAD_DIM), 1)
    head0 = lane < A_HEAD_DIM
    qi = lax.broadcasted_iota(jnp.int32, (BAND, 2 * BAND), 0)
    kj = lax.broadcasted_iota(jnp.int32, (BAND, 2 * BAND), 1)
    dist = BAND + qi - kj
    in_band = (dist >= 0) & (dist <= BAND)
    causal = in_band[:, BAND:]
    scale = 1.0 / math.sqrt(A_HEAD_DIM)

    for br, dil in enumerate(DILATIONS):
        nblk = seq // (BAND * dil)
        shift = dil.bit_length() - 1

        def rows(ref, start):
            if dil == 1:
                return ref[pl.ds(start, BAND), :]
            return ref[pl.ds(start, BAND, stride=dil), :]

        def unit(u, carry):
            r = u & (dil - 1)
            n = u >> shift
            start = r + n * (BAND * dil)
            qb = rows(q_ref, start)
            k_own = rows(k_ref, start).astype(BF16)
            v_own = rows(v_ref, start).astype(BF16)
            if nblk > 1:
                pstart = jnp.maximum(start - BAND * dil, r)
                kk = jnp.concatenate([rows(k_ref, pstart).astype(BF16), k_own], axis=0)
                vv = jnp.concatenate([rows(v_ref, pstart).astype(BF16), v_own], axis=0)
                mask = in_band & (kj >= jnp.where(n > 0, 0, BAND))
            else:
                kk, vv, mask = k_own, v_own, causal
            outs = []
            lses = []
            for hmask in (head0, ~head0):
                qh = jnp.where(hmask, qb, 0.0).astype(BF16)
                s = _dot_nt(qh, kk) * scale
                s = jnp.where(mask, s, NEG_INF)
                mx = jnp.max(s, axis=-1, keepdims=True)
                p = jnp.exp(s - mx)
                l = jnp.sum(p, axis=-1, keepdims=True)
                outs.append(_dot(p.astype(BF16), vv) / l)
                lses.append(mx + jnp.log(l))
            out = jnp.where(head0, outs[0], outs[1])
            lse = jnp.where(head0, lses[0], lses[1])
            if dil == 1:
                ob_ref[br, pl.ds(start, BAND), :] = out
                lb_ref[br, pl.ds(start, BAND), :] = lse
            else:
                ob_ref[br, pl.ds(start, BAND, stride=dil), :] = out
                lb_ref[br, pl.ds(start, BAND, stride=dil), :] = lse
            return carry

        lax.fori_loop(0, seq // BAND, unit, 0, unroll=2)

    def merge(c, carry):
        rs = pl.ds(pl.multiple_of(c * 256, 256), 256)
        l0, l1, l2 = lb_ref[0, rs, :], lb_ref[1, rs, :], lb_ref[2, rs, :]
        m = jnp.maximum(jnp.maximum(l0, l1), l2)
        w0, w1, w2 = jnp.exp(l0 - m), jnp.exp(l1 - m), jnp.exp(l2 - m)
        o_ref[rs, :] = (w0 * ob_ref[0, rs, :] + w1 * ob_ref[1, rs, :] + w2 * ob_ref[2, rs, :]) / (w0 + w1 + w2)
        return carry

    lax.fori_loop(0, seq // 256, merge, 0)


def _attn_prompt(q, k, v):
    b, s, _ = q.shape
    spec = pl.BlockSpec((None, s, 2 * A_HEAD_DIM), lambda i, j: (i, 0, j))
    return pl.pallas_call(
        functools.partial(_attn_prompt_kernel, seq=s),
        grid=(b, A_HEADS // 2),
        in_specs=[spec, spec, spec],
        out_specs=spec,
        out_shape=jax.ShapeDtypeStruct((b, s, A_WIDTH), F32),
        scratch_shapes=[pltpu.VMEM((len(DILATIONS), s, 2 * A_HEAD_DIM), F32),
                        pltpu.VMEM((len(DILATIONS), s, 2 * A_HEAD_DIM), F32)],
        compiler_params=_params("parallel", "parallel"),
        name="attn_prompt",
    )(q, k, v)


def _attn_sample_kernel(q_ref, kn_ref, vn_ref, kc_ref, vc_ref, hs_ref, he_ref, o_ref, ko_ref, vo_ref, kbuf, vbuf,
                        *, win, new):
    nlb = A_WIDTH // 128
    for buf, cache, fresh, shifted in ((kbuf, kc_ref, kn_ref, ko_ref), (vbuf, vc_ref, vn_ref, vo_ref)):
        for j in range(nlb):
            ls = slice(j * 128, (j + 1) * 128)
            buf[j, 0:win, :] = cache[:, ls]
            buf[j, win:win + new, :] = fresh[:, ls]
            shifted[:, ls] = buf[j, new:new + win, :]

    def gather(buf, start, size, stride):
        if stride == 1:
            return jnp.concatenate([buf[j, pl.ds(start, size), :] for j in range(nlb)], axis=1)
        return jnp.concatenate([buf[j, pl.ds(start, size, stride=stride), :] for j in range(nlb)], axis=1)

    hs = hs_ref[...]
    he = he_ref[...]
    scale = 1.0 / math.sqrt(A_HEAD_DIM)
    for l in range(new):
        ql = q_ref[l:l + 1, :]
        outs, lses = [], []
        for dil in DILATIONS:
            start = win + l - BAND * dil
            ks = gather(kbuf, start, BAND, dil)
            vs = gather(vbuf, start, BAND, dil)
            kself = gather(kbuf, win + l, 1, 1)
            vself = gather(vbuf, win + l, 1, 1)
            s = _dot(ks * ql, hs, precision=HIGHEST) * scale
            sself = _dot(kself * ql, hs, precision=HIGHEST) * scale
            mx = jnp.maximum(jnp.max(s, axis=0, keepdims=True), sself)
            p = jnp.exp(s - mx)
            pself = jnp.exp(sself - mx)
            lsum = jnp.sum(p, axis=0, keepdims=True) + pself
            pe = _dot(p, he, precision=HIGHEST)
            o = jnp.sum(pe * vs, axis=0, keepdims=True) + _dot(pself, he, precision=HIGHEST) * vself
            outs.append(o / _dot(lsum, he, precision=HIGHEST))
            lses.append(_dot(mx + jnp.log(lsum), he, precision=HIGHEST))
        m = jnp.maximum(jnp.maximum(lses[0], lses[1]), lses[2])
        w = [jnp.exp(ls - m) for ls in lses]
        o_ref[l:l + 1, :] = (w[0] * outs[0] + w[1] * outs[1] + w[2] * outs[2]) / (w[0] + w[1] + w[2])


def _attn_sample(q, kn, vn, kc, vc, hs, he):
    b, new, _ = q.shape
    win = kc.shape[1]
    assert win >= BAND * max(DILATIONS)
    small = pl.BlockSpec((None, new, A_WIDTH), lambda i: (i, 0, 0))
    big = pl.BlockSpec((None, win, A_WIDTH), lambda i: (i, 0, 0))
    const = lambda a: pl.BlockSpec(a.shape, lambda i: (0, 0))
    return pl.pallas_call(
        functools.partial(_attn_sample_kernel, win=win, new=new),
        grid=(b,),
        in_specs=[small, small, small, big, big, const(hs), const(he)],
        out_specs=[small, big, big],
        out_shape=[jax.ShapeDtypeStruct((b, new, A_WIDTH), F32), jax.ShapeDtypeStruct((b, win, A_WIDTH), F32),
                   jax.ShapeDtypeStruct((b, win, A_WIDTH), F32)],
        scratch_shapes=[pltpu.VMEM((A_WIDTH // 128, win + 8, 128), F32), pltpu.VMEM((A_WIDTH // 128, win + 8, 128), F32)],
        compiler_params=_params("parallel"),
        name="attn_sample",
    )(q, kn, vn, kc, vc, hs, he)


def _dot3(a, b):
    ah = a.astype(BF16)
    al = (a - ah.astype(F32)).astype(BF16)
    bh = b.astype(BF16)
    bl = (b - bh.astype(F32)).astype(BF16)
    return _dot(ah, bh) + (_dot(ah, bl) + _dot(al, bh))


def _dot_01(a01, b):
    a = a01.astype(BF16)
    b1 = b.astype(BF16)
    r1 = b - b1.astype(F32)
    b2 = r1.astype(BF16)
    b3 = (r1 - b2.astype(F32)).astype(BF16)
    return _dot(a, b1) + (_dot(a, b2) + _dot(a, b3))


def _delta_kernel(raw_ref, hist_ref, cw_ref, gb_ref, z_ref, dnw_ref, s0_ref, o_ref, sf_ref, ext, s_scr, *, blk, chunk):
    c = chunk
    nh = B_HEADS
    r = nh * c
    ncol = 3 * nh
    shift = c.bit_length() - 1
    j = pl.program_id(1)

    @pl.when(j == 0)
    def _():
        for col in range(ncol):
            ext[col, 0:8, :] = hist_ref[:, col * B_DIM:(col + 1) * B_DIM]
        for k in range(nh):
            s_scr[k * B_DIM:(k + 1) * B_DIM, :] = s0_ref[k]

    for col in range(ncol):
        ext[col, 8:8 + blk, :] = raw_ref[:, col * B_DIM:(col + 1) * B_DIM]

    ii = lax.broadcasted_iota(jnp.int32, (r, r), 0)
    jj = lax.broadcasted_iota(jnp.int32, (r, r), 1)
    same = (ii >> shift) == (jj >> shift)
    causal = same & (ii >= jj)
    strict = same & (ii > jj)
    upper = same & (ii <= jj)
    eye = (ii == jj).astype(F32)
    tri = causal.astype(F32)
    ones = jnp.ones((r, r), F32)
    row_head = lax.broadcasted_iota(jnp.int32, (r, nh * B_DIM), 0) >> shift
    lane_head = lax.broadcasted_iota(jnp.int32, (r, nh * B_DIM), 1) >> (B_DIM.bit_length() - 1)
    own_head = row_head == lane_head
    dnw = dnw_ref[...]

    def conv_silu(col, base):
        y = cw_ref[0:1, col * B_DIM:(col + 1) * B_DIM] * ext[col, pl.ds(base + 5, c), :]
        for t in range(1, CONV_WIDTH):
            y = y + cw_ref[t:t + 1, col * B_DIM:(col + 1) * B_DIM] * ext[col, pl.ds(base + 5 + t, c), :]
        return y * _sigmoid(y)

    def l2n(t):
        return t * lax.rsqrt(jnp.sum(t * t, axis=-1, keepdims=True) + EPS)

    def stack(parts):
        return jnp.concatenate(parts, axis=0)

    def block_diag(x):
        return jnp.where(own_head, jnp.concatenate([x] * nh, axis=1), 0.0)

    def body(n, carry):
        base = pl.multiple_of(n * c, c)
        q = stack([l2n(conv_silu(k, base)) for k in range(nh)]) * (B_DIM ** -0.5)
        kk = stack([l2n(conv_silu(nh + k, base)) for k in range(nh)])
        v = stack([conv_silu(2 * nh + k, base) for k in range(nh)])
        gbs = [gb_ref[pl.ds(base, c), k * B_DIM:(k + 1) * B_DIM] for k in range(nh)]
        beta = stack([t[:, 0:1] for t in gbs])
        g = stack([t[:, 1:2] for t in gbs])
        gsq = jnp.broadcast_to(g, (r, r))
        gcol = _dot_01(tri, gsq)
        grow = _dot_01(ones, jnp.where(upper, gsq, 0.0))
        gcum = gcol[:, 0:1]
        glast = stack([jnp.broadcast_to(gcol[(k + 1) * c - 1:(k + 1) * c, 0:1], (c, 1)) for k in range(nh)])
        decay = jnp.exp(jnp.where(causal, gcol - grow, NEG_INF))
        kb = kk * beta
        kbh, kh, qh = kb.astype(BF16), kk.astype(BF16), q.astype(BF16)
        lower = jnp.where(strict, _dot_nt(kbh, kh) * decay, 0.0)
        tinv = eye - lower
        lp = lower
        span = 2
        while span < c:
            lp = _dot3(lp, lp)
            tinv = tinv + _dot3(tinv, lp)
            span *= 2
        eg = jnp.exp(gcum)
        u = _dot3(tinv, v * beta)
        w = _dot3(tinv, kb * eg)
        intra = jnp.where(causal, _dot_nt(qh, kh) * decay, 0.0)
        state = s_scr[...]
        sh = state.astype(BF16)
        v_new = u - _dot(block_diag(w).astype(BF16), sh)
        vh = v_new.astype(BF16)
        out = _dot(block_diag(q * eg).astype(BF16), sh) + _dot(intra.astype(BF16), vh)
        kdec = kk * jnp.exp(glast - gcum)
        cd = stack([jnp.broadcast_to(jnp.exp(gcol[(k + 1) * c - 1:(k + 1) * c, 0:1]), (B_DIM, 1)) for k in range(nh)])
        s_scr[...] = state * cd + _dot_tn(block_diag(kdec).astype(BF16), vh)
        on = out * lax.rsqrt(jnp.mean(out * out, axis=-1, keepdims=True) + EPS) * dnw
        for k in range(nh):
            zc = z_ref[pl.ds(base, c), k * B_DIM:(k + 1) * B_DIM]
            o_ref[pl.ds(base, c), k * B_DIM:(k + 1) * B_DIM] = on[k * c:(k + 1) * c] * (zc * _sigmoid(zc))
        return carry

    lax.fori_loop(0, blk // c, body, 0)
    for col in range(ncol):
        ext[col, 0:8, :] = ext[col, blk:blk + 8, :]

    @pl.when(j == pl.num_programs(1) - 1)
    def _():
        for k in range(nh):
            sf_ref[k] = s_scr[k * B_DIM:(k + 1) * B_DIM, :]


def _delta(raw, hist, conv_w, gb, z, dnw, s0, chunk, blk):
    b, seq, _ = raw.shape
    hb = B_HEADS * B_DIM
    tok = lambda w: pl.BlockSpec((None, blk, w), lambda i, j: (i, j, 0))
    st = pl.BlockSpec((None, B_HEADS, B_DIM, B_DIM), lambda i, j: (i, 0, 0, 0))
    return pl.pallas_call(
        functools.partial(_delta_kernel, blk=blk, chunk=chunk),
        grid=(b, seq // blk),
        in_specs=[tok(B_QKV), pl.BlockSpec((None, 8, B_QKV), lambda i, j: (i, 0, 0)),
                  pl.BlockSpec((CONV_WIDTH, B_QKV), lambda i, j: (0, 0)), tok(hb), tok(hb),
                  pl.BlockSpec((1, B_DIM), lambda i, j: (0, 0)), st],
        out_specs=[tok(hb), st],
        out_shape=[jax.ShapeDtypeStruct((b, seq, hb), F32), jax.ShapeDtypeStruct((b, B_HEADS, B_DIM, B_DIM), F32)],
        scratch_shapes=[pltpu.VMEM((3 * B_HEADS, blk + 8, B_DIM), F32), pltpu.VMEM((B_HEADS * B_DIM, B_DIM), F32)],
        compiler_params=_params("parallel", "arbitrary"),
        name="delta",
    )(raw, hist, conv_w, gb, z, dnw, s0)


def _proj_out_kernel(x_ref, a_ref, b_ref, wa_ref, wb_ref, nw_ref, h_ref, hnt_ref):
    h = x_ref[...] + _dot(a_ref[...].astype(BF16), wa_ref[...]) + _dot(b_ref[...].astype(BF16), wb_ref[...])
    h_ref[...] = h
    hn = h * lax.rsqrt(jnp.mean(h * h, axis=-1, keepdims=True) + EPS) * nw_ref[...]
    hnt_ref[...] = hn.T.astype(BF16)


def _proj_out(x, a, b, wts, tm):
    t = x.shape[0]
    row = lambda i: (i, 0)
    full = lambda arr: pl.BlockSpec(arr.shape, lambda i: (0, 0))
    return pl.pallas_call(
        _proj_out_kernel,
        grid=(t // tm,),
        in_specs=[pl.BlockSpec((tm, D_MODEL), row), pl.BlockSpec((tm, A_WIDTH), row), pl.BlockSpec((tm, A_WIDTH), row),
                  full(wts["wo_a"]), full(wts["wo_b"]), full(wts["norm_ffn"])],
        out_specs=[pl.BlockSpec((tm, D_MODEL), row), pl.BlockSpec((D_MODEL, tm), lambda i: (0, i))],
        out_shape=[jax.ShapeDtypeStruct((t, D_MODEL), F32), jax.ShapeDtypeStruct((D_MODEL, t), BF16)],
        compiler_params=_params("parallel"),
        name="proj_out",
    )(x, a, b, wts["wo_a"], wts["wo_b"], wts["norm_ffn"])


def _top16_ranked(x):
    t = x.shape[1]
    slot = lax.broadcasted_iota(jnp.int32, (PEER_TOPK, t), 0).astype(F32)
    vals = jnp.full((PEER_TOPK, t), NEG_INF, F32)
    rank = jnp.full(x.shape, float(PEER_TOPK), F32)
    filled = jnp.zeros((1, t), F32)
    for _ in range(PEER_TOPK):
        m = jnp.max(x, axis=0, keepdims=True)
        eq = x == m
        cnt = jnp.sum(jnp.where(eq, 1.0, 0.0), axis=0, keepdims=True)
        vals = jnp.where((slot >= filled) & (slot < filled + cnt), m, vals)
        rank = jnp.where(eq, jnp.minimum(filled, float(PEER_TOPK)), rank)
        filled = filled + cnt
        x = jnp.where(eq, NEG_INF, x)
    return vals, rank


def _kth_largest(x, k):
    t = x.shape[1]
    kth = jnp.full((1, t), NEG_INF, F32)
    filled = jnp.zeros((1, t), F32)
    for _ in range(k):
        m = jnp.max(x, axis=0, keepdims=True)
        eq = x == m
        cnt = jnp.sum(jnp.where(eq, 1.0, 0.0), axis=0, keepdims=True)
        kth = jnp.where((filled < k) & (filled + cnt >= k), m, kth)
        filled = filled + cnt
        x = jnp.where(eq, NEG_INF, x)
    return kth


def _peer_select_kernel(hnt_ref, wpq_ref, sk_ref, n1_ref, q1_ref, r2_ref, q2_ref, sub_scr, top_scr, rank_scr):
    hnt = hnt_ref[...]

    def half(hc, carry):
        row0 = pl.multiple_of(hc * PEER_KEYS, PEER_KEYS)
        qt = _dot(wpq_ref[pl.ds(row0, PEER_KEYS), :], hnt)
        sub = _dot(sk_ref[hc & 1], qt.astype(BF16))
        sub_scr[hc] = sub
        top_scr[hc], rank_scr[hc] = _top16_ranked(sub)
        return carry

    lax.fori_loop(0, 2 * PEER_HEADS, half, 0, unroll=2)

    def head(h, carry):
        a = top_scr[2 * h]
        b = top_scr[2 * h + 1]
        pieces = [a[0:1] + b] + [a[i:i + 1] + b[0:8] for i in range(1, 8)] + [a[8:16] + b[0:1]]
        cand = jnp.concatenate(pieces, axis=0)
        tau = _kth_largest(cand, PEER_TOPK)
        top = a[0:1] + b[0:1]
        zsum = jnp.sum(jnp.where(cand >= tau, jnp.exp(cand - top), 0.0), axis=0, keepdims=True)
        cnt = jnp.zeros_like(a)
        for j in range(PEER_TOPK):
            cnt = cnt + jnp.where(a + b[j:j + 1] >= tau, 1.0, 0.0)
        rank1 = rank_scr[2 * h]
        n1 = jnp.zeros_like(rank1)
        for i in range(PEER_TOPK):
            n1 = jnp.where(rank1 == float(i), cnt[i:i + 1], n1)
        n1_ref[h] = n1
        r2_ref[h] = rank_scr[2 * h + 1].astype(BF16)
        q1_ref[h] = jnp.exp(sub_scr[2 * h] - a[0:1]) / zsum
        q2_ref[h] = jnp.exp(sub_scr[2 * h + 1] - b[0:1]).astype(BF16)
        return carry

    lax.fori_loop(0, PEER_HEADS, head, 0)


def _peer_select(hnt, wts, tt):
    t = hnt.shape[1]
    out = pl.BlockSpec((PEER_HEADS, PEER_KEYS, tt), lambda i: (0, 0, i))
    shape = lambda dt: jax.ShapeDtypeStruct((PEER_HEADS, PEER_KEYS, t), dt)
    return pl.pallas_call(
        _peer_select_kernel,
        grid=(t // tt,),
        in_specs=[pl.BlockSpec((D_MODEL, tt), lambda i: (0, i)),
                  pl.BlockSpec(wts["wpq_t"].shape, lambda i: (0, 0)),
                  pl.BlockSpec(wts["sub_keys"].shape, lambda i: (0, 0, 0))],
        out_specs=[out, out, out, out],
        out_shape=[shape(F32), shape(F32), shape(BF16), shape(BF16)],
        scratch_shapes=[pltpu.VMEM((2 * PEER_HEADS, PEER_KEYS, tt), F32), pltpu.VMEM((2 * PEER_HEADS, PEER_TOPK, tt), F32),
                        pltpu.VMEM((2 * PEER_HEADS, PEER_KEYS, tt), F32)],
        compiler_params=_params("parallel"),
        name="peer_select",
    )(hnt, wts["wpq_t"], wts["sub_keys"])


A_PER_TILE = 8


def _rows_bf16(row):
    return jnp.broadcast_to(row, (PEER_KEYS, 128)).astype(BF16)


def _peer_dense_kernel(hnt_ref, u_ref, vt_ref, n1_ref, q1_ref, r2_ref, q2_ref, h_ref, y_ref, acc_ref, pre_ref, w_ref,
                       r2_scr, q2_scr, *, tt):
    j = pl.program_id(1)

    @pl.when(j == 0)
    def _():
        acc_ref[...] = jnp.zeros_like(acc_ref)
        r2_scr[...] = r2_ref[...]
        q2_scr[...] = q2_ref[...]

    sub = min(tt, 256)
    zero = jnp.zeros((PEER_KEYS, 128), BF16)
    for s in range(tt // sub):
        ts = slice(s * sub, (s + 1) * sub)
        pre_ref[:, ts] = _dot(u_ref[...], hnt_ref[:, ts]).astype(BF16)
    group = 2
    for s in range(tt // sub):
        for ia0 in range(0, A_PER_TILE, group):
            for tc in range(sub // 128):
                ls = slice(s * sub + tc * 128, s * sub + (tc + 1) * 128)
                gates = [zero] * group
                for h in range(PEER_HEADS):
                    r2 = r2_scr[h, :, ls]
                    q2 = q2_scr[h, :, ls]
                    for k in range(group):
                        n1 = _rows_bf16(n1_ref[h, ia0 + k:ia0 + k + 1, ls])
                        q1 = _rows_bf16(q1_ref[h, ia0 + k:ia0 + k + 1, ls])
                        gates[k] = gates[k] + jnp.where(r2 < n1, q2, zero) * q1
                for k in range(group):
                    rows = slice((ia0 + k) * PEER_KEYS, (ia0 + k + 1) * PEER_KEYS)
                    p = pre_ref[rows, ls]
                    act = (0.5 * p) * (1.0 + lax.erf(p * (1.0 / math.sqrt(2.0))))
                    w_ref[rows, ls] = gates[k] * act
    for s in range(tt // sub):
        ts = slice(s * sub, (s + 1) * sub)
        acc_ref[:, ts] += _dot(vt_ref[...], w_ref[:, ts])

    @pl.when(j == pl.num_programs(1) - 1)
    def _():
        y_ref[...] = h_ref[...] + acc_ref[...].T


def _peer_dense(hnt, h, sel, wts, tt):
    t = hnt.shape[1]
    n1, q1, r2, q2 = sel
    et = A_PER_TILE * PEER_KEYS
    a_spec = pl.BlockSpec((PEER_HEADS, A_PER_TILE, tt), lambda i, j: (0, j, i))
    b_spec = pl.BlockSpec((PEER_HEADS, PEER_KEYS, tt), lambda i, j: (0, 0, i))
    return pl.pallas_call(
        functools.partial(_peer_dense_kernel, tt=tt),
        grid=(t // tt, PEER_KEYS // A_PER_TILE),
        in_specs=[pl.BlockSpec((D_MODEL, tt), lambda i, j: (0, i)),
                  pl.BlockSpec((et, D_MODEL), lambda i, j: (j, 0)),
                  pl.BlockSpec((D_MODEL, et), lambda i, j: (0, j)),
                  a_spec, a_spec, b_spec, b_spec,
                  pl.BlockSpec((tt, D_MODEL), lambda i, j: (i, 0))],
        out_specs=pl.BlockSpec((tt, D_MODEL), lambda i, j: (i, 0)),
        out_shape=jax.ShapeDtypeStruct((t, D_MODEL), F32),
        scratch_shapes=[pltpu.VMEM((D_MODEL, tt), F32), pltpu.VMEM((et, tt), BF16), pltpu.VMEM((et, tt), BF16),
                        pltpu.VMEM((PEER_HEADS, PEER_KEYS, tt), BF16), pltpu.VMEM((PEER_HEADS, PEER_KEYS, tt), BF16)],
        compiler_params=_params("parallel", "arbitrary"),
        name="peer_dense",
    )(hnt, wts["u"], wts["v_t"], n1, q1, r2, q2, h)


def _rope_tables(pos, reps):
    half = A_HEAD_DIM // 2
    inv_freq = ROPE_THETA ** (-jnp.arange(half, dtype=F32) / half)
    ang = pos.astype(F32)[:, None] * inv_freq[None, :]
    cos = jnp.cos(ang)
    sin = jnp.sin(ang)
    cos_t = jnp.tile(jnp.concatenate([cos, cos], axis=-1), (reps, A_HEADS))
    sin_t = jnp.tile(jnp.concatenate([-sin, sin], axis=-1), (reps, A_HEADS))
    return cos_t, sin_t


def _prep_weights(norm_mix_w, w_in, q_norm_w, k_norm_w, a_log, dt_bias, w_out, norm_ffn_w, w_pq, sub_keys, expert_u, expert_v):
    na = 3 * A_WIDTH
    hb = B_HEADS * B_DIM
    wgate = w_in[:, na + B_QKV + hb:]
    wg = jnp.zeros((D_MODEL, B_HEADS, B_DIM), F32)
    wg = wg.at[:, :, 0].set(wgate[:, :B_HEADS]).at[:, :, 1].set(wgate[:, B_HEADS:])
    lane_head = jnp.arange(A_WIDTH) // A_HEAD_DIM
    put1 = lambda vec: jnp.zeros((B_HEADS, B_DIM), F32).at[:, 1].set(vec).reshape(1, hb)
    return {
        "norm_mix": norm_mix_w.reshape(1, D_MODEL),
        "wa": w_in[:, :na].astype(BF16),
        "wb": w_in[:, na:na + B_QKV].astype(BF16),
        "wz": w_in[:, na + B_QKV:na + B_QKV + hb].astype(BF16),
        "wg": wg.reshape(D_MODEL, hb).astype(BF16),
        "qnw": jnp.tile(q_norm_w, A_HEADS).reshape(1, A_WIDTH),
        "knw": jnp.tile(k_norm_w, A_HEADS).reshape(1, A_WIDTH),
        "headmat": (lane_head[:, None] == lane_head[None, :]).astype(BF16),
        "alog": put1(a_log),
        "dtb": put1(dt_bias),
        "wo_a": w_out[:A_WIDTH].astype(BF16),
        "wo_b": w_out[A_WIDTH:].astype(BF16),
        "norm_ffn": norm_ffn_w.reshape(1, D_MODEL),
        "wpq_t": w_pq.T.astype(BF16),
        "sub_keys": sub_keys.astype(BF16),
        "u": expert_u.astype(BF16),
        "v_t": expert_v.T.astype(BF16),
    }


def _layer(x, pos_cos, pos_sin, pos_blocks, wts, conv_w, dnw, mixer_a, hist, s0, chunk, tm, tt_sel, tt_dense):
    b, t, _ = x.shape
    xf = x.reshape(b * t, D_MODEL)
    q, k, v, raw, z, gb = _proj_in(xf, pos_cos, pos_sin, pos_blocks, wts, tm)
    a_out, k_buf, v_buf = mixer_a(q.reshape(b, t, A_WIDTH), k.reshape(b, t, A_WIDTH), v.reshape(b, t, A_WIDTH))
    raw = raw.reshape(b, t, B_QKV)
    conv_state = raw[:, t - (CONV_WIDTH - 1):]
    pad = (-t) % chunk
    pad3 = lambda arr: jnp.pad(arr.reshape(b, t, -1), ((0, 0), (0, pad), (0, 0)))
    b_out, s_fin = _delta(pad3(raw), hist, conv_w, pad3(gb), pad3(z), dnw, s0, chunk, min(512, t + pad))
    b_out = b_out[:, :t].reshape(b * t, B_HEADS * B_DIM)
    h, hnt = _proj_out(xf, a_out.reshape(b * t, A_WIDTH), b_out, wts, tm)
    sel = _peer_select(hnt, wts, tt_sel)
    y = _peer_dense(hnt, h, sel, wts, tt_dense)
    return y.reshape(b, t, D_MODEL), k_buf, v_buf, conv_state, s_fin


def kernel(x_prompt, x_sample, cache_k_win, cache_v_win, state_conv, state_delta, norm_mix_w, w_in, q_norm_w, k_norm_w, conv_w, a_log, dt_bias, delta_norm_w, w_out, norm_ffn_w, w_pq, sub_keys, expert_u, expert_v):
    assert w_in.shape[0] == 1, "one layer"
    bp, sp, _ = x_prompt.shape
    bs, ss, _ = x_sample.shape
    wts = _prep_weights(norm_mix_w[0], w_in[0], q_norm_w[0], k_norm_w[0], a_log[0], dt_bias[0], w_out[0], norm_ffn_w[0],
                        w_pq[0], sub_keys[0], expert_u[0], expert_v[0])
    cw = conv_w[0]
    dnw = delta_norm_w[0].reshape(1, B_DIM)
    lane_head = jnp.arange(A_WIDTH) // A_HEAD_DIM
    head_sum = (lane_head[:, None] == jnp.arange(128)[None, :]).astype(F32)
    head_expand = head_sum.T

    tm_p = 256
    cos_p, sin_p = _rope_tables(jnp.arange(sp), 1)

    def mixer_prompt(q, k, v):
        return _attn_prompt(q, k, v), k, v

    yp, kp, vp, cp, dp = _layer(
        x_prompt, cos_p, sin_p, sp // tm_p, wts, cw, dnw, mixer_prompt,
        jnp.zeros((bp, 8, B_QKV), F32), jnp.zeros((bp, B_HEADS, B_DIM, B_DIM), F32),
        chunk=64, tm=tm_p, tt_sel=256, tt_dense=1024)

    n_s = bs * ss
    cos_s, sin_s = _rope_tables(PAST_LEN + jnp.arange(ss), bs)

    def mixer_sample(q, k, v):
        kc = cache_k_win[0].reshape(bs, -1, A_WIDTH)
        vc = cache_v_win[0].reshape(bs, -1, A_WIDTH)
        return _attn_sample(q, k, v, kc, vc, head_sum, head_expand)

    hist_s = jnp.pad(state_conv[0], ((0, 0), (8 - (CONV_WIDTH - 1), 0), (0, 0)))
    ys, ks, vs, cs, ds = _layer(
        x_sample, cos_s, sin_s, 1, wts, cw, dnw, mixer_sample, hist_s, state_delta[0],
        chunk=8, tm=n_s, tt_sel=n_s, tt_dense=n_s)

    win_p = min(BAND * max(DILATIONS), sp)
    shape_kv = lambda arr, b: arr.reshape(1, b, -1, A_HEADS, A_HEAD_DIM)
    return (yp, ys,
            shape_kv(kp[:, sp - win_p:], bp), shape_kv(vp[:, sp - win_p:], bp), cp[None], dp[None],
            shape_kv(ks, bs), shape_kv(vs, bs), cs[None], ds[None])
```

```python
import functools
import math

import jax
import jax.numpy as jnp
from jax import lax
from jax.experimental import pallas as pl
from jax.experimental.pallas import tpu as pltpu

F32 = jnp.float32
BF16 = jnp.bfloat16
U32 = jnp.uint32
HIGHEST = lax.Precision.HIGHEST

D_MODEL = 1024
A_HEADS = 8
A_HEAD_DIM = 64
A_WIDTH = A_HEADS * A_HEAD_DIM
DILATIONS = (1, 4, 16)
BAND = 128
ROPE_THETA = 10000.0
PAST_LEN = 8192
B_HEADS = 4
B_DIM = 128
B_QKV = 3 * B_HEADS * B_DIM
CONV_WIDTH = 4
PEER_HEADS = 8
PEER_KEYS = 128
PEER_TOPK = 16
EPS = 1e-6
NEG_INF = float("-inf")
VMEM_LIMIT = 56 * 1024 * 1024


def _params(*sem, flags=None):
    return pltpu.CompilerParams(dimension_semantics=sem, vmem_limit_bytes=VMEM_LIMIT, flags=flags)


def _dot(a, b, **kw):
    return jnp.dot(a, b, preferred_element_type=F32, **kw)


def _dot_nt(a, b, **kw):
    return lax.dot_general(a, b, (((1,), (1,)), ((), ())), preferred_element_type=F32, **kw)


def _dot_tn(a, b, **kw):
    return lax.dot_general(a, b, (((0,), (0,)), ((), ())), preferred_element_type=F32, **kw)


def _sigmoid(x):
    return 1.0 / (1.0 + jnp.exp(-x))


def _proj_in_kernel(x_ref, nw_ref, wa_ref, wb_ref, wz_ref, wg_ref, qnw_ref, knw_ref, cos_ref, sin_ref,
                    hm_ref, alog_ref, dtb_ref, q_ref, k_ref, v_ref, raw_ref, z_ref, gb_ref):
    x = x_ref[...]
    ms = jnp.mean(x * x, axis=-1, keepdims=True)
    xn = (x * lax.rsqrt(ms + EPS) * nw_ref[...]).astype(BF16)
    a = _dot(xn, wa_ref[...])
    hm = hm_ref[...]
    lane = lax.broadcasted_iota(jnp.int32, (1, A_WIDTH), 1)
    first_half = (lane & (A_HEAD_DIM // 2)) == 0
    cos = cos_ref[...]
    sin = sin_ref[...]

    def norm_rope(t, w):
        t2 = t * t
        hi = t2.astype(BF16)
        lo = (t2 - hi.astype(F32)).astype(BF16)
        ss = _dot(hi, hm) + _dot(lo, hm)
        tn = t * lax.rsqrt(ss * (1.0 / A_HEAD_DIM) + EPS) * w
        partner = jnp.where(first_half, pltpu.roll(tn, A_WIDTH - A_HEAD_DIM // 2, 1), pltpu.roll(tn, A_HEAD_DIM // 2, 1))
        return tn * cos + partner * sin

    q_ref[...] = norm_rope(a[:, :A_WIDTH], qnw_ref[...])
    k_ref[...] = norm_rope(a[:, A_WIDTH:2 * A_WIDTH], knw_ref[...])
    v_ref[...] = a[:, 2 * A_WIDTH:]
    raw_ref[...] = _dot(xn, wb_ref[...])
    z_ref[...] = _dot(xn, wz_ref[...])
    zg = _dot(xn, wg_ref[...])
    beta = _sigmoid(zg)
    t = zg + dtb_ref[...]
    softplus = jnp.maximum(t, 0.0) + jnp.log1p(jnp.exp(-jnp.abs(t)))
    g = -jnp.exp(alog_ref[...]) * softplus
    glane = lax.broadcasted_iota(jnp.int32, (1, B_HEADS * B_DIM), 1) & (B_DIM - 1)
    gb_ref[...] = jnp.where(glane == 0, beta, jnp.where(glane == 1, g, 0.0))


def _proj_in(x, pos_cos, pos_sin, pos_blocks, wts, tm):
    t = x.shape[0]
    n = t // tm
    row = lambda i: (i, 0)
    const = lambda i: (0, 0)
    posmap = lambda i: (i % pos_blocks, 0)
    full = lambda a: pl.BlockSpec(a.shape, const)
    outs = [(A_WIDTH, "q"), (A_WIDTH, "k"), (A_WIDTH, "v"), (B_QKV, "raw"), (B_HEADS * B_DIM, "z"), (B_HEADS * B_DIM, "gb")]
    return pl.pallas_call(
        _proj_in_kernel,
        grid=(n,),
        in_specs=[pl.BlockSpec((tm, D_MODEL), row), full(wts["norm_mix"]), full(wts["wa"]), full(wts["wb"]), full(wts["wz"]),
                  full(wts["wg"]), full(wts["qnw"]), full(wts["knw"]),
                  pl.BlockSpec((tm, A_WIDTH), posmap), pl.BlockSpec((tm, A_WIDTH), posmap),
                  full(wts["headmat"]), full(wts["alog"]), full(wts["dtb"])],
        out_specs=[pl.BlockSpec((tm, w), row) for w, _ in outs],
        out_shape=[jax.ShapeDtypeStruct((t, w), F32) for w, _ in outs],
        compiler_params=_params("parallel"),
        name="proj_in",
    )(x, wts["norm_mix"], wts["wa"], wts["wb"], wts["wz"], wts["wg"], wts["qnw"], wts["knw"], pos_cos, pos_sin,
      wts["headmat"], wts["alog"], wts["dtb"])


def _attn_prompt_kernel(q_ref, k_ref, v_ref, o_ref, ob_ref, lb_ref, *, seq):
    lane = lax.broadcasted_iota(jnp.int32, (BAND, 2 * A_HEAD_DIM), 1)
    head0 = lane < A_HEAD_DIM
    qi = lax.broadcasted_iota(jnp.int32, (BAND, 2 * BAND), 0)
    kj = lax.broadcasted_iota(jnp.int32, (BAND, 2 * BAND), 1)
    dist = BAND + qi - kj
    in_band = (dist >= 0) & (dist <= BAND)
    causal = in_band[:, BAND:]
    scale = 1.0 / math.sqrt(A_HEAD_DIM)

    for br, dil in enumerate(DILATIONS):
        nblk = seq // (BAND * dil)
        shift = dil.bit_length() - 1

        def rows(ref, start):
            if dil == 1:
                return ref[pl.ds(start, BAND), :]
            return ref[pl.ds(start, BAND, stride=dil), :]

        def unit(u, carry):
            r = u & (dil - 1)
            n = u >> shift
            start = r + n * (BAND * dil)
            qb = rows(q_ref, start)
            k_own = rows(k_ref, start).astype(BF16)
            v_own = rows(v_ref, start).astype(BF16)
            if nblk > 1:
                pstart = jnp.maximum(start - BAND * dil, r)
                kk = jnp.concatenate([rows(k_ref, pstart).astype(BF16), k_own], axis=0)
                vv = jnp.concatenate([rows(v_ref, pstart).astype(BF16), v_own], axis=0)
                mask = in_band & (kj >= jnp.where(n > 0, 0, BAND))
            else:
                kk, vv, mask = k_own, v_own, causal
            outs = []
            lses = []
            for hmask in (head0, ~head0):
                qh = jnp.where(hmask, qb, 0.0).astype(BF16)
                s = _dot_nt(qh, kk) * scale
                s = jnp.where(mask, s, NEG_INF)
                mx = jnp.max(s, axis=-1, keepdims=True)
                p = jnp.exp(s - mx)
                l = jnp.sum(p, axis=-1, keepdims=True)
                outs.append(_dot(p.astype(BF16), vv) / l)
                lses.append(mx + jnp.log(l))
            out = jnp.where(head0, outs[0], outs[1])
            lse = jnp.where(head0, lses[0], lses[1])
            if dil == 1:
                ob_ref[br, pl.ds(start, BAND), :] = out
                lb_ref[br, pl.ds(start, BAND), :] = lse
            else:
                ob_ref[br, pl.ds(start, BAND, stride=dil), :] = out
                lb_ref[br, pl.ds(start, BAND, stride=dil), :] = lse
            return carry

        lax.fori_loop(0, seq // BAND, unit, 0, unroll=2)

    def merge(c, carry):
        rs = pl.ds(pl.multiple_of(c * 256, 256), 256)
        l0, l1, l2 = lb_ref[0, rs, :], lb_ref[1, rs, :], lb_ref[2, rs, :]
        m = jnp.maximum(jnp.maximum(l0, l1), l2)
        w0, w1, w2 = jnp.exp(l0 - m), jnp.exp(l1 - m), jnp.exp(l2 - m)
        o_ref[rs, :] = (w0 * ob_ref[0, rs, :] + w1 * ob_ref[1, rs, :] + w2 * ob_ref[2, rs, :]) / (w0 + w1 + w2)
        return carry

    lax.fori_loop(0, seq // 256, merge, 0)


def _attn_prompt(q, k, v):
    b, s, _ = q.shape
    spec = pl.BlockSpec((None, s, 2 * A_HEAD_DIM), lambda i, j: (i, 0, j))
    return pl.pallas_call(
        functools.partial(_attn_prompt_kernel, seq=s),
        grid=(b, A_HEADS // 2),
        in_specs=[spec, spec, spec],
        out_specs=spec,
        out_shape=jax.ShapeDtypeStruct((b, s, A_WIDTH), F32),
        scratch_shapes=[pltpu.VMEM((len(DILATIONS), s, 2 * A_HEAD_DIM), F32),
                        pltpu.VMEM((len(DILATIONS), s, 2 * A_HEAD_DIM), F32)],
        compiler_params=_params("parallel", "parallel"),
        name="attn_prompt",
    )(q, k, v)


def _attn_sample_kernel(q_ref, kn_ref, vn_ref, kc_ref, vc_ref, hs_ref, he_ref, o_ref, ko_ref, vo_ref, kbuf, vbuf,
                        *, win, new):
    nlb = A_WIDTH // 128
    for buf, cache, fresh, shifted in ((kbuf, kc_ref, kn_ref, ko_ref), (vbuf, vc_ref, vn_ref, vo_ref)):
        for j in range(nlb):
            ls = slice(j * 128, (j + 1) * 128)
            buf[j, 0:win, :] = cache[:, ls]
            buf[j, win:win + new, :] = fresh[:, ls]
            shifted[:, ls] = buf[j, new:new + win, :]

    def gather(buf, start, size, stride):
        if stride == 1:
            return jnp.concatenate([buf[j, pl.ds(start, size), :] for j in range(nlb)], axis=1)
        return jnp.concatenate([buf[j, pl.ds(start, size, stride=stride), :] for j in range(nlb)], axis=1)

    hs = hs_ref[...]
    he = he_ref[...]
    scale = 1.0 / math.sqrt(A_HEAD_DIM)
    for l in range(new):
        ql = q_ref[l:l + 1, :]
        outs, lses = [], []
        for dil in DILATIONS:
            start = win + l - BAND * dil
            ks = gather(kbuf, start, BAND, dil)
            vs = gather(vbuf, start, BAND, dil)
            kself = gather(kbuf, win + l, 1, 1)
            vself = gather(vbuf, win + l, 1, 1)
            s = _dot(ks * ql, hs, precision=HIGHEST) * scale
            sself = _dot(kself * ql, hs, precision=HIGHEST) * scale
            mx = jnp.maximum(jnp.max(s, axis=0, keepdims=True), sself)
            p = jnp.exp(s - mx)
            pself = jnp.exp(sself - mx)
            lsum = jnp.sum(p, axis=0, keepdims=True) + pself
            pe = _dot(p, he, precision=HIGHEST)
            o = jnp.sum(pe * vs, axis=0, keepdims=True) + _dot(pself, he, precision=HIGHEST) * vself
            outs.append(o / _dot(lsum, he, precision=HIGHEST))
            lses.append(_dot(mx + jnp.log(lsum), he, precision=HIGHEST))
        m = jnp.maximum(jnp.maximum(lses[0], lses[1]), lses[2])
        w = [jnp.exp(ls - m) for ls in lses]
        o_ref[l:l + 1, :] = (w[0] * outs[0] + w[1] * outs[1] + w[2] * outs[2]) / (w[0] + w[1] + w[2])


def _attn_sample(q, kn, vn, kc, vc, hs, he):
    b, new, _ = q.shape
    win = kc.shape[1]
    assert win >= BAND * max(DILATIONS)
    small = pl.BlockSpec((None, new, A_WIDTH), lambda i: (i, 0, 0))
    big = pl.BlockSpec((None, win, A_WIDTH), lambda i: (i, 0, 0))
    const = lambda a: pl.BlockSpec(a.shape, lambda i: (0, 0))
    return pl.pallas_call(
        functools.partial(_attn_sample_kernel, win=win, new=new),
        grid=(b,),
        in_specs=[small, small, small, big, big, const(hs), const(he)],
        out_specs=[small, big, big],
        out_shape=[jax.ShapeDtypeStruct((b, new, A_WIDTH), F32), jax.ShapeDtypeStruct((b, win, A_WIDTH), F32),
                   jax.ShapeDtypeStruct((b, win, A_WIDTH), F32)],
        scratch_shapes=[pltpu.VMEM((A_WIDTH // 128, win + 8, 128), F32), pltpu.VMEM((A_WIDTH // 128, win + 8, 128), F32)],
        compiler_params=_params("parallel"),
        name="attn_sample",
    )(q, kn, vn, kc, vc, hs, he)


def _dot3(a, b):
    ah = a.astype(BF16)
    al = (a - ah.astype(F32)).astype(BF16)
    bh = b.astype(BF16)
    bl = (b - bh.astype(F32)).astype(BF16)
    return _dot(ah, bh) + (_dot(ah, bl) + _dot(al, bh))


def _dot_01(a01, b):
    a = a01.astype(BF16)
    b1 = b.astype(BF16)
    r1 = b - b1.astype(F32)
    b2 = r1.astype(BF16)
    b3 = (r1 - b2.astype(F32)).astype(BF16)
    return _dot(a, b1) + (_dot(a, b2) + _dot(a, b3))


def _delta_kernel(raw_ref, hist_ref, cw_ref, gb_ref, z_ref, dnw_ref, s0_ref, o_ref, sf_ref, ext, s_scr, *, blk, chunk):
    c = chunk
    nh = B_HEADS
    r = nh * c
    ncol = 3 * nh
    shift = c.bit_length() - 1
    j = pl.program_id(1)

    @pl.when(j == 0)
    def _():
        for col in range(ncol):
            ext[col, 0:8, :] = hist_ref[:, col * B_DIM:(col + 1) * B_DIM]
        for k in range(nh):
            s_scr[k * B_DIM:(k + 1) * B_DIM, :] = s0_ref[k]

    for col in range(ncol):
        ext[col, 8:8 + blk, :] = raw_ref[:, col * B_DIM:(col + 1) * B_DIM]

    ii = lax.broadcasted_iota(jnp.int32, (r, r), 0)
    jj = lax.broadcasted_iota(jnp.int32, (r, r), 1)
    same = (ii >> shift) == (jj >> shift)
    causal = same & (ii >= jj)
    strict = same & (ii > jj)
    upper = same & (ii <= jj)
    eye = (ii == jj).astype(F32)
    tri = causal.astype(F32)
    ones = jnp.ones((r, r), F32)
    row_head = lax.broadcasted_iota(jnp.int32, (r, nh * B_DIM), 0) >> shift
    lane_head = lax.broadcasted_iota(jnp.int32, (r, nh * B_DIM), 1) >> (B_DIM.bit_length() - 1)
    own_head = row_head == lane_head
    dnw = dnw_ref[...]

    def conv_silu(col, base):
        y = cw_ref[0:1, col * B_DIM:(col + 1) * B_DIM] * ext[col, pl.ds(base + 5, c), :]
        for t in range(1, CONV_WIDTH):
            y = y + cw_ref[t:t + 1, col * B_DIM:(col + 1) * B_DIM] * ext[col, pl.ds(base + 5 + t, c), :]
        return y * _sigmoid(y)

    def l2n(t):
        return t * lax.rsqrt(jnp.sum(t * t, axis=-1, keepdims=True) + EPS)

    def stack(parts):
        return jnp.concatenate(parts, axis=0)

    def block_diag(x):
        return jnp.where(own_head, jnp.concatenate([x] * nh, axis=1), 0.0)

    def body(n, carry):
        base = pl.multiple_of(n * c, c)
        q = stack([l2n(conv_silu(k, base)) for k in range(nh)]) * (B_DIM ** -0.5)
        kk = stack([l2n(conv_silu(nh + k, base)) for k in range(nh)])
        v = stack([conv_silu(2 * nh + k, base) for k in range(nh)])
        gbs = [gb_ref[pl.ds(base, c), k * B_DIM:(k + 1) * B_DIM] for k in range(nh)]
        beta = stack([t[:, 0:1] for t in gbs])
        g = stack([t[:, 1:2] for t in gbs])
        gsq = jnp.broadcast_to(g, (r, r))
        gcol = _dot_01(tri, gsq)
        grow = _dot_01(ones, jnp.where(upper, gsq, 0.0))
        gcum = gcol[:, 0:1]
        glast = stack([jnp.broadcast_to(gcol[(k + 1) * c - 1:(k + 1) * c, 0:1], (c, 1)) for k in range(nh)])
        decay = jnp.exp(jnp.where(causal, gcol - grow, NEG_INF))
        kb = kk * beta
        kbh, kh, qh = kb.astype(BF16), kk.astype(BF16), q.astype(BF16)
        lower = jnp.where(strict, _dot_nt(kbh, kh) * decay, 0.0)
        tinv = eye - lower
        lp = lower
        span = 2
        while span < c:
            lp = _dot3(lp, lp)
            tinv = tinv + _dot3(tinv, lp)
            span *= 2
        eg = jnp.exp(gcum)
        u = _dot3(tinv, v * beta)
        w = _dot3(tinv, kb * eg)
        intra = jnp.where(causal, _dot_nt(qh, kh) * decay, 0.0)
        state = s_scr[...]
        sh = state.astype(BF16)
        v_new = u - _dot(block_diag(w).astype(BF16), sh)
        vh = v_new.astype(BF16)
        out = _dot(block_diag(q * eg).astype(BF16), sh) + _dot(intra.astype(BF16), vh)
        kdec = kk * jnp.exp(glast - gcum)
        cd = stack([jnp.broadcast_to(jnp.exp(gcol[(k + 1) * c - 1:(k + 1) * c, 0:1]), (B_DIM, 1)) for k in range(nh)])
        s_scr[...] = state * cd + _dot_tn(block_diag(kdec).astype(BF16), vh)
        on = out * lax.rsqrt(jnp.mean(out * out, axis=-1, keepdims=True) + EPS) * dnw
        for k in range(nh):
            zc = z_ref[pl.ds(base, c), k * B_DIM:(k + 1) * B_DIM]
            o_ref[pl.ds(base, c), k * B_DIM:(k + 1) * B_DIM] = on[k * c:(k + 1) * c] * (zc * _sigmoid(zc))
        return carry

    lax.fori_loop(0, blk // c, body, 0)
    for col in range(ncol):
        ext[col, 0:8, :] = ext[col, blk:blk + 8, :]

    @pl.when(j == pl.num_programs(1) - 1)
    def _():
        for k in range(nh):
            sf_ref[k] = s_scr[k * B_DIM:(k + 1) * B_DIM, :]


def _delta(raw, hist, conv_w, gb, z, dnw, s0, chunk, blk):
    b, seq, _ = raw.shape
    hb = B_HEADS * B_DIM
    tok = lambda w: pl.BlockSpec((None, blk, w), lambda i, j: (i, j, 0))
    st = pl.BlockSpec((None, B_HEADS, B_DIM, B_DIM), lambda i, j: (i, 0, 0, 0))
    return pl.pallas_call(
        functools.partial(_delta_kernel, blk=blk, chunk=chunk),
        grid=(b, seq // blk),
        in_specs=[tok(B_QKV), pl.BlockSpec((None, 8, B_QKV), lambda i, j: (i, 0, 0)),
                  pl.BlockSpec((CONV_WIDTH, B_QKV), lambda i, j: (0, 0)), tok(hb), tok(hb),
                  pl.BlockSpec((1, B_DIM), lambda i, j: (0, 0)), st],
        out_specs=[tok(hb), st],
        out_shape=[jax.ShapeDtypeStruct((b, seq, hb), F32), jax.ShapeDtypeStruct((b, B_HEADS, B_DIM, B_DIM), F32)],
        scratch_shapes=[pltpu.VMEM((3 * B_HEADS, blk + 8, B_DIM), F32), pltpu.VMEM((B_HEADS * B_DIM, B_DIM), F32)],
        compiler_params=_params("parallel", "arbitrary"),
        name="delta",
    )(raw, hist, conv_w, gb, z, dnw, s0)


def _proj_out_kernel(x_ref, a_ref, b_ref, wa_ref, wb_ref, nw_ref, h_ref, hnt_ref):
    h = x_ref[...] + _dot(a_ref[...].astype(BF16), wa_ref[...]) + _dot(b_ref[...].astype(BF16), wb_ref[...])
    h_ref[...] = h
    hn = h * lax.rsqrt(jnp.mean(h * h, axis=-1, keepdims=True) + EPS) * nw_ref[...]
    hnt_ref[...] = hn.T.astype(BF16)


def _proj_out(x, a, b, wts, tm):
    t = x.shape[0]
    row = lambda i: (i, 0)
    full = lambda arr: pl.BlockSpec(arr.shape, lambda i: (0, 0))
    return pl.pallas_call(
        _proj_out_kernel,
        grid=(t // tm,),
        in_specs=[pl.BlockSpec((tm, D_MODEL), row), pl.BlockSpec((tm, A_WIDTH), row), pl.BlockSpec((tm, A_WIDTH), row),
                  full(wts["wo_a"]), full(wts["wo_b"]), full(wts["norm_ffn"])],
        out_specs=[pl.BlockSpec((tm, D_MODEL), row), pl.BlockSpec((D_MODEL, tm), lambda i: (0, i))],
        out_shape=[jax.ShapeDtypeStruct((t, D_MODEL), F32), jax.ShapeDtypeStruct((D_MODEL, t), BF16)],
        compiler_params=_params("parallel"),
        name="proj_out",
    )(x, a, b, wts["wo_a"], wts["wo_b"], wts["norm_ffn"])


def _top16_ranked(x):
    t = x.shape[1]
    slot = lax.broadcasted_iota(jnp.int32, (PEER_TOPK, t), 0).astype(F32)
    vals = jnp.full((PEER_TOPK, t), NEG_INF, F32)
    rank = jnp.full(x.shape, float(PEER_TOPK), F32)
    filled = jnp.zeros((1, t), F32)
    for _ in range(PEER_TOPK):
        m = jnp.max(x, axis=0, keepdims=True)
        eq = x == m
        cnt = jnp.sum(jnp.where(eq, 1.0, 0.0), axis=0, keepdims=True)
        vals = jnp.where((slot >= filled) & (slot < filled + cnt), m, vals)
        rank = jnp.where(eq, jnp.minimum(filled, float(PEER_TOPK)), rank)
        filled = filled + cnt
        x = jnp.where(eq, NEG_INF, x)
    return vals, rank


def _kth_largest(x, k):
    t = x.shape[1]
    kth = jnp.full((1, t), NEG_INF, F32)
    filled = jnp.zeros((1, t), F32)
    for _ in range(k):
        m = jnp.max(x, axis=0, keepdims=True)
        eq = x == m
        cnt = jnp.sum(jnp.where(eq, 1.0, 0.0), axis=0, keepdims=True)
        kth = jnp.where((filled < k) & (filled + cnt >= k), m, kth)
        filled = filled + cnt
        x = jnp.where(eq, NEG_INF, x)
    return kth


def _peer_select_kernel(hnt_ref, wpq_ref, sk_ref, n1_ref, q1_ref, r2_ref, q2_ref, sub_scr, top_scr, rank_scr):
    hnt = hnt_ref[...]
    nslab = hnt.shape[1] // 128

    def half(hc, carry):
        row0 = pl.multiple_of(hc * PEER_KEYS, PEER_KEYS)
        qt = _dot(wpq_ref[pl.ds(row0, PEER_KEYS), :], hnt)
        sub = _dot(sk_ref[hc & 1], qt.astype(BF16))
        sub_scr[hc] = sub
        top_scr[hc], rank_scr[hc] = _top16_ranked(sub)
        return carry

    lax.fori_loop(0, 2 * PEER_HEADS, half, 0, unroll=2)

    def slabs(ref, h, x):
        for c in range(nslab):
            ref[h, c] = x[:, c * 128:(c + 1) * 128].astype(ref.dtype)

    def head(h, carry):
        a = top_scr[2 * h]
        b = top_scr[2 * h + 1]
        pieces = [a[0:1] + b] + [a[i:i + 1] + b[0:8] for i in range(1, 8)] + [a[8:16] + b[0:1]]
        cand = jnp.concatenate(pieces, axis=0)
        tau = _kth_largest(cand, PEER_TOPK)
        top = a[0:1] + b[0:1]
        zsum = jnp.sum(jnp.where(cand >= tau, jnp.exp(cand - top), 0.0), axis=0, keepdims=True)
        cnt = jnp.zeros_like(a)
        for j in range(PEER_TOPK):
            cnt = cnt + jnp.where(a + b[j:j + 1] >= tau, 1.0, 0.0)
        rank1 = rank_scr[2 * h]
        n1 = jnp.zeros_like(rank1)
        for i in range(PEER_TOPK):
            n1 = jnp.where(rank1 == float(i), cnt[i:i + 1], n1)
        slabs(n1_ref, h, n1)
        slabs(r2_ref, h, rank_scr[2 * h + 1])
        slabs(q1_ref, h, jnp.exp(sub_scr[2 * h] - a[0:1]) / zsum)
        slabs(q2_ref, h, jnp.exp(sub_scr[2 * h + 1] - b[0:1]))
        return carry

    lax.fori_loop(0, PEER_HEADS, head, 0)


def _peer_select(hnt, wts, tt):
    t = hnt.shape[1]
    out = pl.BlockSpec((PEER_HEADS, tt // 128, PEER_KEYS, 128), lambda i: (0, i, 0, 0))
    shape = lambda dt: jax.ShapeDtypeStruct((PEER_HEADS, t // 128, PEER_KEYS, 128), dt)
    return pl.pallas_call(
        _peer_select_kernel,
        grid=(t // tt,),
        in_specs=[pl.BlockSpec((D_MODEL, tt), lambda i: (0, i)),
                  pl.BlockSpec(wts["wpq_t"].shape, lambda i: (0, 0)),
                  pl.BlockSpec(wts["sub_keys"].shape, lambda i: (0, 0, 0))],
        out_specs=[out, out, out, out],
        out_shape=[shape(F32), shape(F32), shape(BF16), shape(BF16)],
        scratch_shapes=[pltpu.VMEM((2 * PEER_HEADS, PEER_KEYS, tt), F32), pltpu.VMEM((2 * PEER_HEADS, PEER_TOPK, tt), F32),
                        pltpu.VMEM((2 * PEER_HEADS, PEER_KEYS, tt), F32)],
        compiler_params=_params("parallel"),
        name="peer_select",
    )(hnt, wts["wpq_t"], wts["sub_keys"])


A_PER_TILE = 8


def _rows_bf16(row):
    return jnp.broadcast_to(row, (PEER_KEYS, 128)).astype(BF16)


def _peer_dense_kernel(hnt_ref, u_ref, vt_ref, n1_ref, q1_ref, r2_ref, q2_ref, h_ref, y_ref, acc_ref, pre_ref, w_ref,
                       r2_scr, q2_scr, *, tt):
    j = pl.program_id(1)
    nslab = tt // 128

    @pl.when(j == 0)
    def _():
        acc_ref[...] = jnp.zeros_like(acc_ref)
        r2_scr[...] = r2_ref[...]
        q2_scr[...] = q2_ref[...]

    per_sub = min(nslab, 2)
    zero = jnp.zeros((PEER_KEYS, 128), BF16)
    for s in range(0, nslab, per_sub):
        pre = _dot(u_ref[...], hnt_ref[:, s * 128:(s + per_sub) * 128]).astype(BF16)
        for c in range(per_sub):
            pre_ref[s + c] = pre[:, c * 128:(c + 1) * 128]
    group = 2
    for tc in range(nslab):
        for ia0 in range(0, A_PER_TILE, group):
            gates = [zero] * group
            for h in range(PEER_HEADS):
                r2 = r2_scr[h, tc]
                q2 = q2_scr[h, tc]
                for k in range(group):
                    n1 = _rows_bf16(n1_ref[h, tc, ia0 + k:ia0 + k + 1, :])
                    q1 = _rows_bf16(q1_ref[h, tc, ia0 + k:ia0 + k + 1, :])
                    gates[k] = gates[k] + jnp.where(r2 < n1, q2, zero) * q1
            for k in range(group):
                rows = slice((ia0 + k) * PEER_KEYS, (ia0 + k + 1) * PEER_KEYS)
                p = pre_ref[tc, rows, :]
                act = (0.5 * p) * (1.0 + lax.erf(p * (1.0 / math.sqrt(2.0))))
                w_ref[tc, rows, :] = gates[k] * act
    for s in range(0, nslab, per_sub):
        w = jnp.concatenate([w_ref[s + c] for c in range(per_sub)], axis=1)
        out = _dot(vt_ref[...], w)
        for c in range(per_sub):
            acc_ref[s + c] += out[:, c * 128:(c + 1) * 128]

    @pl.when(j == pl.num_programs(1) - 1)
    def _():
        for c in range(nslab):
            y_ref[c * 128:(c + 1) * 128, :] = h_ref[c * 128:(c + 1) * 128, :] + acc_ref[c].T


def _peer_dense(hnt, h, sel, wts, tt):
    t = hnt.shape[1]
    n1, q1, r2, q2 = sel
    et = A_PER_TILE * PEER_KEYS
    nslab = tt // 128
    a_spec = pl.BlockSpec((PEER_HEADS, nslab, A_PER_TILE, 128), lambda i, j: (0, i, j, 0))
    b_spec = pl.BlockSpec((PEER_HEADS, nslab, PEER_KEYS, 128), lambda i, j: (0, i, 0, 0))
    return pl.pallas_call(
        functools.partial(_peer_dense_kernel, tt=tt),
        grid=(t // tt, PEER_KEYS // A_PER_TILE),
        in_specs=[pl.BlockSpec((D_MODEL, tt), lambda i, j: (0, i)),
                  pl.BlockSpec((et, D_MODEL), lambda i, j: (j, 0)),
                  pl.BlockSpec((D_MODEL, et), lambda i, j: (0, j)),
                  a_spec, a_spec, b_spec, b_spec,
                  pl.BlockSpec((tt, D_MODEL), lambda i, j: (i, 0))],
        out_specs=pl.BlockSpec((tt, D_MODEL), lambda i, j: (i, 0)),
        out_shape=jax.ShapeDtypeStruct((t, D_MODEL), F32),
        scratch_shapes=[pltpu.VMEM((nslab, D_MODEL, 128), F32), pltpu.VMEM((nslab, et, 128), BF16),
                        pltpu.VMEM((nslab, et, 128), BF16),
                        pltpu.VMEM((PEER_HEADS, nslab, PEER_KEYS, 128), BF16),
                        pltpu.VMEM((PEER_HEADS, nslab, PEER_KEYS, 128), BF16)],
        compiler_params=_params("parallel", "arbitrary"),
        name="peer_dense",
    )(hnt, wts["u"], wts["v_t"], n1, q1, r2, q2, h)


def _rope_tables(pos, reps):
    half = A_HEAD_DIM // 2
    inv_freq = ROPE_THETA ** (-jnp.arange(half, dtype=F32) / half)
    ang = pos.astype(F32)[:, None] * inv_freq[None, :]
    cos = jnp.cos(ang)
    sin = jnp.sin(ang)
    cos_t = jnp.tile(jnp.concatenate([cos, cos], axis=-1), (reps, A_HEADS))
    sin_t = jnp.tile(jnp.concatenate([-sin, sin], axis=-1), (reps, A_HEADS))
    return cos_t, sin_t


def _prep_weights(norm_mix_w, w_in, q_norm_w, k_norm_w, a_log, dt_bias, w_out, norm_ffn_w, w_pq, sub_keys, expert_u, expert_v):
    na = 3 * A_WIDTH
    hb = B_HEADS * B_DIM
    wgate = w_in[:, na + B_QKV + hb:]
    wg = jnp.zeros((D_MODEL, B_HEADS, B_DIM), F32)
    wg = wg.at[:, :, 0].set(wgate[:, :B_HEADS]).at[:, :, 1].set(wgate[:, B_HEADS:])
    lane_head = jnp.arange(A_WIDTH) // A_HEAD_DIM
    put1 = lambda vec: jnp.zeros((B_HEADS, B_DIM), F32).at[:, 1].set(vec).reshape(1, hb)
    return {
        "norm_mix": norm_mix_w.reshape(1, D_MODEL),
        "wa": w_in[:, :na].astype(BF16),
        "wb": w_in[:, na:na + B_QKV].astype(BF16),
        "wz": w_in[:, na + B_QKV:na + B_QKV + hb].astype(BF16),
        "wg": wg.reshape(D_MODEL, hb).astype(BF16),
        "qnw": jnp.tile(q_norm_w, A_HEADS).reshape(1, A_WIDTH),
        "knw": jnp.tile(k_norm_w, A_HEADS).reshape(1, A_WIDTH),
        "headmat": (lane_head[:, None] == lane_head[None, :]).astype(BF16),
        "alog": put1(a_log),
        "dtb": put1(dt_bias),
        "wo_a": w_out[:A_WIDTH].astype(BF16),
        "wo_b": w_out[A_WIDTH:].astype(BF16),
        "norm_ffn": norm_ffn_w.reshape(1, D_MODEL),
        "wpq_t": w_pq.T.astype(BF16),
        "sub_keys": sub_keys.astype(BF16),
        "u": expert_u.astype(BF16),
        "v_t": expert_v.T.astype(BF16),
    }


def _layer(x, pos_cos, pos_sin, pos_blocks, wts, conv_w, dnw, mixer_a, hist, s0, chunk, tm, tt_sel, tt_dense):
    b, t, _ = x.shape
    xf = x.reshape(b * t, D_MODEL)
    q, k, v, raw, z, gb = _proj_in(xf, pos_cos, pos_sin, pos_blocks, wts, tm)
    a_out, k_buf, v_buf = mixer_a(q.reshape(b, t, A_WIDTH), k.reshape(b, t, A_WIDTH), v.reshape(b, t, A_WIDTH))
    raw = raw.reshape(b, t, B_QKV)
    conv_state = raw[:, t - (CONV_WIDTH - 1):]
    pad = (-t) % chunk
    pad3 = lambda arr: jnp.pad(arr.reshape(b, t, -1), ((0, 0), (0, pad), (0, 0)))
    b_out, s_fin = _delta(pad3(raw), hist, conv_w, pad3(gb), pad3(z), dnw, s0, chunk, min(512, t + pad))
    b_out = b_out[:, :t].reshape(b * t, B_HEADS * B_DIM)
    h, hnt = _proj_out(xf, a_out.reshape(b * t, A_WIDTH), b_out, wts, tm)
    sel = _peer_select(hnt, wts, tt_sel)
    y = _peer_dense(hnt, h, sel, wts, tt_dense)
    return y.reshape(b, t, D_MODEL), k_buf, v_buf, conv_state, s_fin


def kernel(x_prompt, x_sample, cache_k_win, cache_v_win, state_conv, state_delta, norm_mix_w, w_in, q_norm_w, k_norm_w, conv_w, a_log, dt_bias, delta_norm_w, w_out, norm_ffn_w, w_pq, sub_keys, expert_u, expert_v):
    assert w_in.shape[0] == 1, "one layer"
    bp, sp, _ = x_prompt.shape
    bs, ss, _ = x_sample.shape
    wts = _prep_weights(norm_mix_w[0], w_in[0], q_norm_w[0], k_norm_w[0], a_log[0], dt_bias[0], w_out[0], norm_ffn_w[0],
                        w_pq[0], sub_keys[0], expert_u[0], expert_v[0])
    cw = conv_w[0]
    dnw = delta_norm_w[0].reshape(1, B_DIM)
    lane_head = jnp.arange(A_WIDTH) // A_HEAD_DIM
    head_sum = (lane_head[:, None] == jnp.arange(128)[None, :]).astype(F32)
    head_expand = head_sum.T

    tm_p = 256
    cos_p, sin_p = _rope_tables(jnp.arange(sp), 1)

    def mixer_prompt(q, k, v):
        return _attn_prompt(q, k, v), k, v

    yp, kp, vp, cp, dp = _layer(
        x_prompt, cos_p, sin_p, sp // tm_p, wts, cw, dnw, mixer_prompt,
        jnp.zeros((bp, 8, B_QKV), F32), jnp.zeros((bp, B_HEADS, B_DIM, B_DIM), F32),
        chunk=64, tm=tm_p, tt_sel=256, tt_dense=1024)

    n_s = bs * ss
    cos_s, sin_s = _rope_tables(PAST_LEN + jnp.arange(ss), bs)

    def mixer_sample(q, k, v):
        kc = cache_k_win[0].reshape(bs, -1, A_WIDTH)
        vc = cache_v_win[0].reshape(bs, -1, A_WIDTH)
        return _attn_sample(q, k, v, kc, vc, head_sum, head_expand)

    hist_s = jnp.pad(state_conv[0], ((0, 0), (8 - (CONV_WIDTH - 1), 0), (0, 0)))
    ys, ks, vs, cs, ds = _layer(
        x_sample, cos_s, sin_s, 1, wts, cw, dnw, mixer_sample, hist_s, state_delta[0],
        chunk=8, tm=n_s, tt_sel=n_s, tt_dense=n_s)

    win_p = min(BAND * max(DILATIONS), sp)
    shape_kv = lambda arr, b: arr.reshape(1, b, -1, A_HEADS, A_HEAD_DIM)
    return (yp, ys,
            shape_kv(kp[:, sp - win_p:], bp), shape_kv(vp[:, sp - win_p:], bp), cp[None], dp[None],
            shape_kv(ks, bs), shape_kv(vs, bs), cs[None], ds[None])
```

```python
import functools
import math

import jax
import jax.numpy as jnp
from jax import lax
from jax.experimental import pallas as pl
from jax.experimental.pallas import tpu as pltpu

F32 = jnp.float32
BF16 = jnp.bfloat16
U32 = jnp.uint32
HIGHEST = lax.Precision.HIGHEST

D_MODEL = 1024
A_HEADS = 8
A_HEAD_DIM = 64
A_WIDTH = A_HEADS * A_HEAD_DIM
DILATIONS = (1, 4, 16)
BAND = 128
ROPE_THETA = 10000.0
PAST_LEN = 8192
B_HEADS = 4
B_DIM = 128
B_QKV = 3 * B_HEADS * B_DIM
CONV_WIDTH = 4
PEER_HEADS = 8
PEER_KEYS = 128
PEER_TOPK = 16
EPS = 1e-6
NEG_INF = float("-inf")
VMEM_LIMIT = 56 * 1024 * 1024


def _params(*sem, flags=None):
    return pltpu.CompilerParams(dimension_semantics=sem, vmem_limit_bytes=VMEM_LIMIT, flags=flags)


def _dot(a, b, **kw):
    return jnp.dot(a, b, preferred_element_type=F32, **kw)


def _dot_nt(a, b, **kw):
    return lax.dot_general(a, b, (((1,), (1,)), ((), ())), preferred_element_type=F32, **kw)


def _dot_tn(a, b, **kw):
    return lax.dot_general(a, b, (((0,), (0,)), ((), ())), preferred_element_type=F32, **kw)


def _sigmoid(x):
    return 1.0 / (1.0 + jnp.exp(-x))


def _proj_in_kernel(x_ref, nw_ref, wa_ref, wb_ref, wz_ref, wg_ref, qnw_ref, knw_ref, cos_ref, sin_ref,
                    hm_ref, alog_ref, dtb_ref, q_ref, k_ref, v_ref, raw_ref, z_ref, gb_ref):
    x = x_ref[...]
    ms = jnp.mean(x * x, axis=-1, keepdims=True)
    xn = (x * lax.rsqrt(ms + EPS) * nw_ref[...]).astype(BF16)
    a = _dot(xn, wa_ref[...])
    hm = hm_ref[...]
    lane = lax.broadcasted_iota(jnp.int32, (1, A_WIDTH), 1)
    first_half = (lane & (A_HEAD_DIM // 2)) == 0
    cos = cos_ref[...]
    sin = sin_ref[...]

    def norm_rope(t, w):
        t2 = t * t
        hi = t2.astype(BF16)
        lo = (t2 - hi.astype(F32)).astype(BF16)
        ss = _dot(hi, hm) + _dot(lo, hm)
        tn = t * lax.rsqrt(ss * (1.0 / A_HEAD_DIM) + EPS) * w
        partner = jnp.where(first_half, pltpu.roll(tn, A_WIDTH - A_HEAD_DIM // 2, 1), pltpu.roll(tn, A_HEAD_DIM // 2, 1))
        return tn * cos + partner * sin

    q_ref[...] = norm_rope(a[:, :A_WIDTH], qnw_ref[...])
    k_ref[...] = norm_rope(a[:, A_WIDTH:2 * A_WIDTH], knw_ref[...])
    v_ref[...] = a[:, 2 * A_WIDTH:]
    raw_ref[...] = _dot(xn, wb_ref[...])
    z_ref[...] = _dot(xn, wz_ref[...])
    zg = _dot(xn, wg_ref[...])
    beta = _sigmoid(zg)
    t = zg + dtb_ref[...]
    softplus = jnp.maximum(t, 0.0) + jnp.log1p(jnp.exp(-jnp.abs(t)))
    g = -jnp.exp(alog_ref[...]) * softplus
    glane = lax.broadcasted_iota(jnp.int32, (1, B_HEADS * B_DIM), 1) & (B_DIM - 1)
    gb_ref[...] = jnp.where(glane == 0, beta, jnp.where(glane == 1, g, 0.0))


def _proj_in(x, pos_cos, pos_sin, pos_blocks, wts, tm):
    t = x.shape[0]
    n = t // tm
    row = lambda i: (i, 0)
    const = lambda i: (0, 0)
    posmap = lambda i: (i % pos_blocks, 0)
    full = lambda a: pl.BlockSpec(a.shape, const)
    outs = [(A_WIDTH, "q"), (A_WIDTH, "k"), (A_WIDTH, "v"), (B_QKV, "raw"), (B_HEADS * B_DIM, "z"), (B_HEADS * B_DIM, "gb")]
    return pl.pallas_call(
        _proj_in_kernel,
        grid=(n,),
        in_specs=[pl.BlockSpec((tm, D_MODEL), row), full(wts["norm_mix"]), full(wts["wa"]), full(wts["wb"]), full(wts["wz"]),
                  full(wts["wg"]), full(wts["qnw"]), full(wts["knw"]),
                  pl.BlockSpec((tm, A_WIDTH), posmap), pl.BlockSpec((tm, A_WIDTH), posmap),
                  full(wts["headmat"]), full(wts["alog"]), full(wts["dtb"])],
        out_specs=[pl.BlockSpec((tm, w), row) for w, _ in outs],
        out_shape=[jax.ShapeDtypeStruct((t, w), F32) for w, _ in outs],
        compiler_params=_params("parallel"),
        name="proj_in",
    )(x, wts["norm_mix"], wts["wa"], wts["wb"], wts["wz"], wts["wg"], wts["qnw"], wts["knw"], pos_cos, pos_sin,
      wts["headmat"], wts["alog"], wts["dtb"])


def _attn_prompt_kernel(q_ref, k_ref, v_ref, o_ref, ob_ref, lb_ref, *, seq):
    lane = lax.broadcasted_iota(jnp.int32, (BAND, 2 * A_HEAD_DIM), 1)
    head0 = lane < A_HEAD_DIM
    qi = lax.broadcasted_iota(jnp.int32, (BAND, 2 * BAND), 0)
    kj = lax.broadcasted_iota(jnp.int32, (BAND, 2 * BAND), 1)
    dist = BAND + qi - kj
    in_band = (dist >= 0) & (dist <= BAND)
    causal = in_band[:, BAND:]
    scale = 1.0 / math.sqrt(A_HEAD_DIM)

    for br, dil in enumerate(DILATIONS):
        nblk = seq // (BAND * dil)
        shift = dil.bit_length() - 1

        def rows(ref, start):
            if dil == 1:
                return ref[pl.ds(start, BAND), :]
            return ref[pl.ds(start, BAND, stride=dil), :]

        def unit(u, carry):
            r = u & (dil - 1)
            n = u >> shift
            start = r + n * (BAND * dil)
            qb = rows(q_ref, start)
            k_own = rows(k_ref, start).astype(BF16)
            v_own = rows(v_ref, start).astype(BF16)
            if nblk > 1:
                pstart = jnp.maximum(start - BAND * dil, r)
                kk = jnp.concatenate([rows(k_ref, pstart).astype(BF16), k_own], axis=0)
                vv = jnp.concatenate([rows(v_ref, pstart).astype(BF16), v_own], axis=0)
                mask = in_band & (kj >= jnp.where(n > 0, 0, BAND))
            else:
                kk, vv, mask = k_own, v_own, causal
            outs = []
            lses = []
            for hmask in (head0, ~head0):
                qh = jnp.where(hmask, qb, 0.0).astype(BF16)
                s = _dot_nt(qh, kk) * scale
                s = jnp.where(mask, s, NEG_INF)
                mx = jnp.max(s, axis=-1, keepdims=True)
                p = jnp.exp(s - mx)
                l = jnp.sum(p, axis=-1, keepdims=True)
                outs.append(_dot(p.astype(BF16), vv) / l)
                lses.append(mx + jnp.log(l))
            out = jnp.where(head0, outs[0], outs[1])
            lse = jnp.where(head0, lses[0], lses[1])
            if dil == 1:
                ob_ref[br, pl.ds(start, BAND), :] = out
                lb_ref[br, pl.ds(start, BAND), :] = lse
            else:
                ob_ref[br, pl.ds(start, BAND, stride=dil), :] = out
                lb_ref[br, pl.ds(start, BAND, stride=dil), :] = lse
            return carry

        lax.fori_loop(0, seq // BAND, unit, 0, unroll=4)

    def merge(c, carry):
        rs = pl.ds(pl.multiple_of(c * 256, 256), 256)
        l0, l1, l2 = lb_ref[0, rs, :], lb_ref[1, rs, :], lb_ref[2, rs, :]
        m = jnp.maximum(jnp.maximum(l0, l1), l2)
        w0, w1, w2 = jnp.exp(l0 - m), jnp.exp(l1 - m), jnp.exp(l2 - m)
        o_ref[rs, :] = (w0 * ob_ref[0, rs, :] + w1 * ob_ref[1, rs, :] + w2 * ob_ref[2, rs, :]) / (w0 + w1 + w2)
        return carry

    lax.fori_loop(0, seq // 256, merge, 0)


def _attn_prompt(q, k, v):
    b, s, _ = q.shape
    spec = pl.BlockSpec((None, s, 2 * A_HEAD_DIM), lambda i, j: (i, 0, j))
    return pl.pallas_call(
        functools.partial(_attn_prompt_kernel, seq=s),
        grid=(b, A_HEADS // 2),
        in_specs=[spec, spec, spec],
        out_specs=spec,
        out_shape=jax.ShapeDtypeStruct((b, s, A_WIDTH), F32),
        scratch_shapes=[pltpu.VMEM((len(DILATIONS), s, 2 * A_HEAD_DIM), F32),
                        pltpu.VMEM((len(DILATIONS), s, 2 * A_HEAD_DIM), F32)],
        compiler_params=_params("parallel", "parallel"),
        name="attn_prompt",
    )(q, k, v)


def _attn_sample_kernel(q_ref, kn_ref, vn_ref, kc_ref, vc_ref, hs_ref, he_ref, o_ref, ko_ref, vo_ref, kbuf, vbuf,
                        *, win, new):
    nlb = A_WIDTH // 128
    for buf, cache, fresh, shifted in ((kbuf, kc_ref, kn_ref, ko_ref), (vbuf, vc_ref, vn_ref, vo_ref)):
        for j in range(nlb):
            ls = slice(j * 128, (j + 1) * 128)
            buf[j, 0:win, :] = cache[:, ls]
            buf[j, win:win + new, :] = fresh[:, ls]
            shifted[:, ls] = buf[j, new:new + win, :]

    def gather(buf, start, size, stride):
        if stride == 1:
            return jnp.concatenate([buf[j, pl.ds(start, size), :] for j in range(nlb)], axis=1)
        return jnp.concatenate([buf[j, pl.ds(start, size, stride=stride), :] for j in range(nlb)], axis=1)

    hs = hs_ref[...]
    he = he_ref[...]
    scale = 1.0 / math.sqrt(A_HEAD_DIM)
    for l in range(new):
        ql = q_ref[l:l + 1, :]
        outs, lses = [], []
        for dil in DILATIONS:
            start = win + l - BAND * dil
            ks = gather(kbuf, start, BAND, dil)
            vs = gather(vbuf, start, BAND, dil)
            kself = gather(kbuf, win + l, 1, 1)
            vself = gather(vbuf, win + l, 1, 1)
            s = _dot(ks * ql, hs, precision=HIGHEST) * scale
            sself = _dot(kself * ql, hs, precision=HIGHEST) * scale
            mx = jnp.maximum(jnp.max(s, axis=0, keepdims=True), sself)
            p = jnp.exp(s - mx)
            pself = jnp.exp(sself - mx)
            lsum = jnp.sum(p, axis=0, keepdims=True) + pself
            pe = _dot(p, he, precision=HIGHEST)
            o = jnp.sum(pe * vs, axis=0, keepdims=True) + _dot(pself, he, precision=HIGHEST) * vself
            outs.append(o / _dot(lsum, he, precision=HIGHEST))
            lses.append(_dot(mx + jnp.log(lsum), he, precision=HIGHEST))
        m = jnp.maximum(jnp.maximum(lses[0], lses[1]), lses[2])
        w = [jnp.exp(ls - m) for ls in lses]
        o_ref[l:l + 1, :] = (w[0] * outs[0] + w[1] * outs[1] + w[2] * outs[2]) / (w[0] + w[1] + w[2])


def _attn_sample(q, kn, vn, kc, vc, hs, he):
    b, new, _ = q.shape
    win = kc.shape[1]
    assert win >= BAND * max(DILATIONS)
    small = pl.BlockSpec((None, new, A_WIDTH), lambda i: (i, 0, 0))
    big = pl.BlockSpec((None, win, A_WIDTH), lambda i: (i, 0, 0))
    const = lambda a: pl.BlockSpec(a.shape, lambda i: (0, 0))
    return pl.pallas_call(
        functools.partial(_attn_sample_kernel, win=win, new=new),
        grid=(b,),
        in_specs=[small, small, small, big, big, const(hs), const(he)],
        out_specs=[small, big, big],
        out_shape=[jax.ShapeDtypeStruct((b, new, A_WIDTH), F32), jax.ShapeDtypeStruct((b, win, A_WIDTH), F32),
                   jax.ShapeDtypeStruct((b, win, A_WIDTH), F32)],
        scratch_shapes=[pltpu.VMEM((A_WIDTH // 128, win + 8, 128), F32), pltpu.VMEM((A_WIDTH // 128, win + 8, 128), F32)],
        compiler_params=_params("parallel"),
        name="attn_sample",
    )(q, kn, vn, kc, vc, hs, he)


def _dot3(a, b):
    ah = a.astype(BF16)
    al = (a - ah.astype(F32)).astype(BF16)
    bh = b.astype(BF16)
    bl = (b - bh.astype(F32)).astype(BF16)
    return _dot(ah, bh) + (_dot(ah, bl) + _dot(al, bh))


def _dot_01(a01, b):
    a = a01.astype(BF16)
    b1 = b.astype(BF16)
    r1 = b - b1.astype(F32)
    b2 = r1.astype(BF16)
    b3 = (r1 - b2.astype(F32)).astype(BF16)
    return _dot(a, b1) + (_dot(a, b2) + _dot(a, b3))


def _delta_kernel(raw_ref, hist_ref, cw_ref, gb_ref, z_ref, dnw_ref, s0_ref, o_ref, sf_ref, ext, s_scr, *, blk, chunk):
    c = chunk
    nh = B_HEADS
    r = nh * c
    ncol = 3 * nh
    shift = c.bit_length() - 1
    j = pl.program_id(1)

    @pl.when(j == 0)
    def _():
        for col in range(ncol):
            ext[col, 0:8, :] = hist_ref[:, col * B_DIM:(col + 1) * B_DIM]
        for k in range(nh):
            s_scr[k * B_DIM:(k + 1) * B_DIM, :] = s0_ref[k]

    for col in range(ncol):
        ext[col, 8:8 + blk, :] = raw_ref[:, col * B_DIM:(col + 1) * B_DIM]

    ii = lax.broadcasted_iota(jnp.int32, (r, r), 0)
    jj = lax.broadcasted_iota(jnp.int32, (r, r), 1)
    same = (ii >> shift) == (jj >> shift)
    causal = same & (ii >= jj)
    strict = same & (ii > jj)
    upper = same & (ii <= jj)
    eye = (ii == jj).astype(F32)
    tri = causal.astype(F32)
    ones = jnp.ones((r, r), F32)
    row_head = lax.broadcasted_iota(jnp.int32, (r, nh * B_DIM), 0) >> shift
    lane_head = lax.broadcasted_iota(jnp.int32, (r, nh * B_DIM), 1) >> (B_DIM.bit_length() - 1)
    own_head = row_head == lane_head
    dnw = dnw_ref[...]

    def conv_silu(col, base):
        y = cw_ref[0:1, col * B_DIM:(col + 1) * B_DIM] * ext[col, pl.ds(base + 5, c), :]
        for t in range(1, CONV_WIDTH):
            y = y + cw_ref[t:t + 1, col * B_DIM:(col + 1) * B_DIM] * ext[col, pl.ds(base + 5 + t, c), :]
        return y * _sigmoid(y)

    def l2n(t):
        return t * lax.rsqrt(jnp.sum(t * t, axis=-1, keepdims=True) + EPS)

    def stack(parts):
        return jnp.concatenate(parts, axis=0)

    def block_diag(x):
        return jnp.where(own_head, jnp.concatenate([x] * nh, axis=1), 0.0)

    def body(n, carry):
        base = pl.multiple_of(n * c, c)
        q = stack([l2n(conv_silu(k, base)) for k in range(nh)]) * (B_DIM ** -0.5)
        kk = stack([l2n(conv_silu(nh + k, base)) for k in range(nh)])
        v = stack([conv_silu(2 * nh + k, base) for k in range(nh)])
        gbs = [gb_ref[pl.ds(base, c), k * B_DIM:(k + 1) * B_DIM] for k in range(nh)]
        beta = stack([t[:, 0:1] for t in gbs])
        g = stack([t[:, 1:2] for t in gbs])
        gsq = jnp.broadcast_to(g, (r, r))
        gcol = _dot_01(tri, gsq)
        grow = _dot_01(ones, jnp.where(upper, gsq, 0.0))
        gcum = gcol[:, 0:1]
        glast = stack([jnp.broadcast_to(gcol[(k + 1) * c - 1:(k + 1) * c, 0:1], (c, 1)) for k in range(nh)])
        decay = jnp.exp(jnp.where(causal, gcol - grow, NEG_INF))
        kb = kk * beta
        kbh, kh, qh = kb.astype(BF16), kk.astype(BF16), q.astype(BF16)
        lower = jnp.where(strict, _dot_nt(kbh, kh) * decay, 0.0)
        tinv = eye - lower
        lp = lower
        span = 2
        while span < c:
            lp = _dot3(lp, lp)
            tinv = tinv + _dot3(tinv, lp)
            span *= 2
        eg = jnp.exp(gcum)
        u = _dot3(tinv, v * beta)
        w = _dot3(tinv, kb * eg)
        intra = jnp.where(causal, _dot_nt(qh, kh) * decay, 0.0)
        state = s_scr[...]
        sh = state.astype(BF16)
        v_new = u - _dot(block_diag(w).astype(BF16), sh)
        vh = v_new.astype(BF16)
        out = _dot(block_diag(q * eg).astype(BF16), sh) + _dot(intra.astype(BF16), vh)
        kdec = kk * jnp.exp(glast - gcum)
        cd = stack([jnp.broadcast_to(jnp.exp(gcol[(k + 1) * c - 1:(k + 1) * c, 0:1]), (B_DIM, 1)) for k in range(nh)])
        s_scr[...] = state * cd + _dot_tn(block_diag(kdec).astype(BF16), vh)
        on = out * lax.rsqrt(jnp.mean(out * out, axis=-1, keepdims=True) + EPS) * dnw
        for k in range(nh):
            zc = z_ref[pl.ds(base, c), k * B_DIM:(k + 1) * B_DIM]
            o_ref[pl.ds(base, c), k * B_DIM:(k + 1) * B_DIM] = on[k * c:(k + 1) * c] * (zc * _sigmoid(zc))
        return carry

    lax.fori_loop(0, blk // c, body, 0)
    for col in range(ncol):
        ext[col, 0:8, :] = ext[col, blk:blk + 8, :]

    @pl.when(j == pl.num_programs(1) - 1)
    def _():
        for k in range(nh):
            sf_ref[k] = s_scr[k * B_DIM:(k + 1) * B_DIM, :]


def _delta(raw, hist, conv_w, gb, z, dnw, s0, chunk, blk):
    b, seq, _ = raw.shape
    hb = B_HEADS * B_DIM
    tok = lambda w: pl.BlockSpec((None, blk, w), lambda i, j: (i, j, 0))
    st = pl.BlockSpec((None, B_HEADS, B_DIM, B_DIM), lambda i, j: (i, 0, 0, 0))
    return pl.pallas_call(
        functools.partial(_delta_kernel, blk=blk, chunk=chunk),
        grid=(b, seq // blk),
        in_specs=[tok(B_QKV), pl.BlockSpec((None, 8, B_QKV), lambda i, j: (i, 0, 0)),
                  pl.BlockSpec((CONV_WIDTH, B_QKV), lambda i, j: (0, 0)), tok(hb), tok(hb),
                  pl.BlockSpec((1, B_DIM), lambda i, j: (0, 0)), st],
        out_specs=[tok(hb), st],
        out_shape=[jax.ShapeDtypeStruct((b, seq, hb), F32), jax.ShapeDtypeStruct((b, B_HEADS, B_DIM, B_DIM), F32)],
        scratch_shapes=[pltpu.VMEM((3 * B_HEADS, blk + 8, B_DIM), F32), pltpu.VMEM((B_HEADS * B_DIM, B_DIM), F32)],
        compiler_params=_params("parallel", "arbitrary"),
        name="delta",
    )(raw, hist, conv_w, gb, z, dnw, s0)


def _proj_out_kernel(x_ref, a_ref, b_ref, wa_ref, wb_ref, nw_ref, h_ref, hnt_ref):
    h = x_ref[...] + _dot(a_ref[...].astype(BF16), wa_ref[...]) + _dot(b_ref[...].astype(BF16), wb_ref[...])
    h_ref[...] = h
    hn = h * lax.rsqrt(jnp.mean(h * h, axis=-1, keepdims=True) + EPS) * nw_ref[...]
    hnt_ref[...] = hn.T.astype(BF16)


def _proj_out(x, a, b, wts, tm):
    t = x.shape[0]
    row = lambda i: (i, 0)
    full = lambda arr: pl.BlockSpec(arr.shape, lambda i: (0, 0))
    return pl.pallas_call(
        _proj_out_kernel,
        grid=(t // tm,),
        in_specs=[pl.BlockSpec((tm, D_MODEL), row), pl.BlockSpec((tm, A_WIDTH), row), pl.BlockSpec((tm, A_WIDTH), row),
                  full(wts["wo_a"]), full(wts["wo_b"]), full(wts["norm_ffn"])],
        out_specs=[pl.BlockSpec((tm, D_MODEL), row), pl.BlockSpec((D_MODEL, tm), lambda i: (0, i))],
        out_shape=[jax.ShapeDtypeStruct((t, D_MODEL), F32), jax.ShapeDtypeStruct((D_MODEL, t), BF16)],
        compiler_params=_params("parallel"),
        name="proj_out",
    )(x, a, b, wts["wo_a"], wts["wo_b"], wts["norm_ffn"])


def _top16_ranked(x):
    t = x.shape[1]
    slot = lax.broadcasted_iota(jnp.int32, (PEER_TOPK, t), 0).astype(F32)
    vals = jnp.full((PEER_TOPK, t), NEG_INF, F32)
    rank = jnp.full(x.shape, float(PEER_TOPK), F32)
    filled = jnp.zeros((1, t), F32)
    for _ in range(PEER_TOPK):
        m = jnp.max(x, axis=0, keepdims=True)
        eq = x == m
        cnt = jnp.sum(jnp.where(eq, 1.0, 0.0), axis=0, keepdims=True)
        vals = jnp.where((slot >= filled) & (slot < filled + cnt), m, vals)
        rank = jnp.where(eq, jnp.minimum(filled, float(PEER_TOPK)), rank)
        filled = filled + cnt
        x = jnp.where(eq, NEG_INF, x)
    return vals, rank


def _kth_largest(x, k):
    t = x.shape[1]
    kth = jnp.full((1, t), NEG_INF, F32)
    filled = jnp.zeros((1, t), F32)
    for _ in range(k):
        m = jnp.max(x, axis=0, keepdims=True)
        eq = x == m
        cnt = jnp.sum(jnp.where(eq, 1.0, 0.0), axis=0, keepdims=True)
        kth = jnp.where((filled < k) & (filled + cnt >= k), m, kth)
        filled = filled + cnt
        x = jnp.where(eq, NEG_INF, x)
    return kth


def _peer_select_kernel(hnt_ref, wpq_ref, sk_ref, n1_ref, q1_ref, r2_ref, q2_ref, sub_scr, top_scr, rank_scr):
    hnt = hnt_ref[...]
    nslab = hnt.shape[1] // 128

    def half(hc, carry):
        row0 = pl.multiple_of(hc * PEER_KEYS, PEER_KEYS)
        qt = _dot(wpq_ref[pl.ds(row0, PEER_KEYS), :], hnt)
        sub = _dot(sk_ref[hc & 1], qt.astype(BF16))
        sub_scr[hc] = sub
        top_scr[hc], rank_scr[hc] = _top16_ranked(sub)
        return carry

    lax.fori_loop(0, 2 * PEER_HEADS, half, 0, unroll=2)

    def slabs(ref, h, x):
        for c in range(nslab):
            ref[h, c] = x[:, c * 128:(c + 1) * 128].astype(ref.dtype)

    def head(h, carry):
        a = top_scr[2 * h]
        b = top_scr[2 * h + 1]
        pieces = [a[0:1] + b] + [a[i:i + 1] + b[0:8] for i in range(1, 8)] + [a[8:16] + b[0:1]]
        cand = jnp.concatenate(pieces, axis=0)
        tau = _kth_largest(cand, PEER_TOPK)
        top = a[0:1] + b[0:1]
        zsum = jnp.sum(jnp.where(cand >= tau, jnp.exp(cand - top), 0.0), axis=0, keepdims=True)
        cnt = jnp.zeros_like(a)
        for j in range(PEER_TOPK):
            cnt = cnt + jnp.where(a + b[j:j + 1] >= tau, 1.0, 0.0)
        rank1 = rank_scr[2 * h]
        n1 = jnp.zeros_like(rank1)
        for i in range(PEER_TOPK):
            n1 = jnp.where(rank1 == float(i), cnt[i:i + 1], n1)
        slabs(n1_ref, h, n1)
        slabs(r2_ref, h, rank_scr[2 * h + 1])
        slabs(q1_ref, h, jnp.exp(sub_scr[2 * h] - a[0:1]) / zsum)
        slabs(q2_ref, h, jnp.exp(sub_scr[2 * h + 1] - b[0:1]))
        return carry

    lax.fori_loop(0, PEER_HEADS, head, 0)


def _peer_select(hnt, wts, tt):
    t = hnt.shape[1]
    out = pl.BlockSpec((PEER_HEADS, tt // 128, PEER_KEYS, 128), lambda i: (0, i, 0, 0))
    shape = lambda dt: jax.ShapeDtypeStruct((PEER_HEADS, t // 128, PEER_KEYS, 128), dt)
    return pl.pallas_call(
        _peer_select_kernel,
        grid=(t // tt,),
        in_specs=[pl.BlockSpec((D_MODEL, tt), lambda i: (0, i)),
                  pl.BlockSpec(wts["wpq_t"].shape, lambda i: (0, 0)),
                  pl.BlockSpec(wts["sub_keys"].shape, lambda i: (0, 0, 0))],
        out_specs=[out, out, out, out],
        out_shape=[shape(F32), shape(F32), shape(BF16), shape(BF16)],
        scratch_shapes=[pltpu.VMEM((2 * PEER_HEADS, PEER_KEYS, tt), F32), pltpu.VMEM((2 * PEER_HEADS, PEER_TOPK, tt), F32),
                        pltpu.VMEM((2 * PEER_HEADS, PEER_KEYS, tt), F32)],
        compiler_params=_params("parallel"),
        name="peer_select",
    )(hnt, wts["wpq_t"], wts["sub_keys"])


A_PER_TILE = 8


def _rows_bf16(row):
    return jnp.broadcast_to(row, (PEER_KEYS, 128)).astype(BF16)


def _peer_dense_kernel(hnt_ref, u_ref, vt_ref, n1_ref, q1_ref, r2_ref, q2_ref, h_ref, y_ref, acc_ref, pre_ref, w_ref,
                       r2_scr, q2_scr, *, tt):
    j = pl.program_id(1)
    nslab = tt // 128

    @pl.when(j == 0)
    def _():
        acc_ref[...] = jnp.zeros_like(acc_ref)
        r2_scr[...] = r2_ref[...]
        q2_scr[...] = q2_ref[...]

    per_sub = min(nslab, 2)
    zero = jnp.zeros((PEER_KEYS, 128), BF16)
    for s in range(0, nslab, per_sub):
        pre = _dot(u_ref[...], hnt_ref[:, s * 128:(s + per_sub) * 128]).astype(BF16)
        for c in range(per_sub):
            pre_ref[s + c] = pre[:, c * 128:(c + 1) * 128]
    group = 2

    def build(tc, carry):
        for ia0 in range(0, A_PER_TILE, group):
            gates = [zero] * group
            for h in range(PEER_HEADS):
                r2 = r2_scr[h, tc]
                q2 = q2_scr[h, tc]
                for k in range(group):
                    n1 = _rows_bf16(n1_ref[h, tc, ia0 + k:ia0 + k + 1, :])
                    q1 = _rows_bf16(q1_ref[h, tc, ia0 + k:ia0 + k + 1, :])
                    gates[k] = gates[k] + jnp.where(r2 < n1, q2, zero) * q1
            for k in range(group):
                rows = slice((ia0 + k) * PEER_KEYS, (ia0 + k + 1) * PEER_KEYS)
                p = pre_ref[tc, rows, :]
                act = (0.5 * p) * (1.0 + lax.erf(p * (1.0 / math.sqrt(2.0))))
                w_ref[tc, rows, :] = gates[k] * act
        return carry

    lax.fori_loop(0, nslab, build, 0)
    for s in range(0, nslab, per_sub):
        w = jnp.concatenate([w_ref[s + c] for c in range(per_sub)], axis=1)
        out = _dot(vt_ref[...], w)
        for c in range(per_sub):
            acc_ref[s + c] += out[:, c * 128:(c + 1) * 128]

    @pl.when(j == pl.num_programs(1) - 1)
    def _():
        for c in range(nslab):
            y_ref[c * 128:(c + 1) * 128, :] = h_ref[c * 128:(c + 1) * 128, :] + acc_ref[c].T


def _peer_dense(hnt, h, sel, wts, tt):
    t = hnt.shape[1]
    n1, q1, r2, q2 = sel
    et = A_PER_TILE * PEER_KEYS
    nslab = tt // 128
    a_spec = pl.BlockSpec((PEER_HEADS, nslab, A_PER_TILE, 128), lambda i, j: (0, i, j, 0))
    b_spec = pl.BlockSpec((PEER_HEADS, nslab, PEER_KEYS, 128), lambda i, j: (0, i, 0, 0))
    return pl.pallas_call(
        functools.partial(_peer_dense_kernel, tt=tt),
        grid=(t // tt, PEER_KEYS // A_PER_TILE),
        in_specs=[pl.BlockSpec((D_MODEL, tt), lambda i, j: (0, i)),
                  pl.BlockSpec((et, D_MODEL), lambda i, j: (j, 0)),
                  pl.BlockSpec((D_MODEL, et), lambda i, j: (0, j)),
                  a_spec, a_spec, b_spec, b_spec,
                  pl.BlockSpec((tt, D_MODEL), lambda i, j: (i, 0))],
        out_specs=pl.BlockSpec((tt, D_MODEL), lambda i, j: (i, 0)),
        out_shape=jax.ShapeDtypeStruct((t, D_MODEL), F32),
        scratch_shapes=[pltpu.VMEM((nslab, D_MODEL, 128), F32), pltpu.VMEM((nslab, et, 128), BF16),
                        pltpu.VMEM((nslab, et, 128), BF16),
                        pltpu.VMEM((PEER_HEADS, nslab, PEER_KEYS, 128), BF16),
                        pltpu.VMEM((PEER_HEADS, nslab, PEER_KEYS, 128), BF16)],
        compiler_params=_params("parallel", "arbitrary"),
        name="peer_dense",
    )(hnt, wts["u"], wts["v_t"], n1, q1, r2, q2, h)


def _rope_tables(pos, reps):
    half = A_HEAD_DIM // 2
    inv_freq = ROPE_THETA ** (-jnp.arange(half, dtype=F32) / half)
    ang = pos.astype(F32)[:, None] * inv_freq[None, :]
    cos = jnp.cos(ang)
    sin = jnp.sin(ang)
    cos_t = jnp.tile(jnp.concatenate([cos, cos], axis=-1), (reps, A_HEADS))
    sin_t = jnp.tile(jnp.concatenate([-sin, sin], axis=-1), (reps, A_HEADS))
    return cos_t, sin_t


def _prep_weights(norm_mix_w, w_in, q_norm_w, k_norm_w, a_log, dt_bias, w_out, norm_ffn_w, w_pq, sub_keys, expert_u, expert_v):
    na = 3 * A_WIDTH
    hb = B_HEADS * B_DIM
    wgate = w_in[:, na + B_QKV + hb:]
    wg = jnp.zeros((D_MODEL, B_HEADS, B_DIM), F32)
    wg = wg.at[:, :, 0].set(wgate[:, :B_HEADS]).at[:, :, 1].set(wgate[:, B_HEADS:])
    lane_head = jnp.arange(A_WIDTH) // A_HEAD_DIM
    put1 = lambda vec: jnp.zeros((B_HEADS, B_DIM), F32).at[:, 1].set(vec).reshape(1, hb)
    return {
        "norm_mix": norm_mix_w.reshape(1, D_MODEL),
        "wa": w_in[:, :na].astype(BF16),
        "wb": w_in[:, na:na + B_QKV].astype(BF16),
        "wz": w_in[:, na + B_QKV:na + B_QKV + hb].astype(BF16),
        "wg": wg.reshape(D_MODEL, hb).astype(BF16),
        "qnw": jnp.tile(q_norm_w, A_HEADS).reshape(1, A_WIDTH),
        "knw": jnp.tile(k_norm_w, A_HEADS).reshape(1, A_WIDTH),
        "headmat": (lane_head[:, None] == lane_head[None, :]).astype(BF16),
        "alog": put1(a_log),
        "dtb": put1(dt_bias),
        "wo_a": w_out[:A_WIDTH].astype(BF16),
        "wo_b": w_out[A_WIDTH:].astype(BF16),
        "norm_ffn": norm_ffn_w.reshape(1, D_MODEL),
        "wpq_t": w_pq.T.astype(BF16),
        "sub_keys": sub_keys.astype(BF16),
        "u": expert_u.astype(BF16),
        "v_t": expert_v.T.astype(BF16),
    }


def _layer(x, pos_cos, pos_sin, pos_blocks, wts, conv_w, dnw, mixer_a, hist, s0, chunk, tm, tt_sel, tt_dense):
    b, t, _ = x.shape
    xf = x.reshape(b * t, D_MODEL)
    q, k, v, raw, z, gb = _proj_in(xf, pos_cos, pos_sin, pos_blocks, wts, tm)
    a_out, k_buf, v_buf = mixer_a(q.reshape(b, t, A_WIDTH), k.reshape(b, t, A_WIDTH), v.reshape(b, t, A_WIDTH))
    raw = raw.reshape(b, t, B_QKV)
    conv_state = raw[:, t - (CONV_WIDTH - 1):]
    pad = (-t) % chunk
    pad3 = lambda arr: jnp.pad(arr.reshape(b, t, -1), ((0, 0), (0, pad), (0, 0)))
    b_out, s_fin = _delta(pad3(raw), hist, conv_w, pad3(gb), pad3(z), dnw, s0, chunk, min(512, t + pad))
    b_out = b_out[:, :t].reshape(b * t, B_HEADS * B_DIM)
    h, hnt = _proj_out(xf, a_out.reshape(b * t, A_WIDTH), b_out, wts, tm)
    sel = _peer_select(hnt, wts, tt_sel)
    y = _peer_dense(hnt, h, sel, wts, tt_dense)
    return y.reshape(b, t, D_MODEL), k_buf, v_buf, conv_state, s_fin


def kernel(x_prompt, x_sample, cache_k_win, cache_v_win, state_conv, state_delta, norm_mix_w, w_in, q_norm_w, k_norm_w, conv_w, a_log, dt_bias, delta_norm_w, w_out, norm_ffn_w, w_pq, sub_keys, expert_u, expert_v):
    assert w_in.shape[0] == 1, "one layer"
    bp, sp, _ = x_prompt.shape
    bs, ss, _ = x_sample.shape
    wts = _prep_weights(norm_mix_w[0], w_in[0], q_norm_w[0], k_norm_w[0], a_log[0], dt_bias[0], w_out[0], norm_ffn_w[0],
                        w_pq[0], sub_keys[0], expert_u[0], expert_v[0])
    cw = conv_w[0]
    dnw = delta_norm_w[0].reshape(1, B_DIM)
    lane_head = jnp.arange(A_WIDTH) // A_HEAD_DIM
    head_sum = (lane_head[:, None] == jnp.arange(128)[None, :]).astype(F32)
    head_expand = head_sum.T

    tm_p = 512
    cos_p, sin_p = _rope_tables(jnp.arange(sp), 1)

    def mixer_prompt(q, k, v):
        return _attn_prompt(q, k, v), k, v

    yp, kp, vp, cp, dp = _layer(
        x_prompt, cos_p, sin_p, sp // tm_p, wts, cw, dnw, mixer_prompt,
        jnp.zeros((bp, 8, B_QKV), F32), jnp.zeros((bp, B_HEADS, B_DIM, B_DIM), F32),
        chunk=64, tm=tm_p, tt_sel=256, tt_dense=1024)

    n_s = bs * ss
    cos_s, sin_s = _rope_tables(PAST_LEN + jnp.arange(ss), bs)

    def mixer_sample(q, k, v):
        kc = cache_k_win[0].reshape(bs, -1, A_WIDTH)
        vc = cache_v_win[0].reshape(bs, -1, A_WIDTH)
        return _attn_sample(q, k, v, kc, vc, head_sum, head_expand)

    hist_s = jnp.pad(state_conv[0], ((0, 0), (8 - (CONV_WIDTH - 1), 0), (0, 0)))
    ys, ks, vs, cs, ds = _layer(
        x_sample, cos_s, sin_s, 1, wts, cw, dnw, mixer_sample, hist_s, state_delta[0],
        chunk=8, tm=n_s, tt_sel=n_s, tt_dense=n_s)

    win_p = min(BAND * max(DILATIONS), sp)
    shape_kv = lambda arr, b: arr.reshape(1, b, -1, A_HEADS, A_HEAD_DIM)
    return (yp, ys,
            shape_kv(kp[:, sp - win_p:], bp), shape_kv(vp[:, sp - win_p:], bp), cp[None], dp[None],
            shape_kv(ks, bs), shape_kv(vs, bs), cs[None], ds[None])
```

```python
import functools
import math

import jax
import jax.numpy as jnp
from jax import lax
from jax.experimental import pallas as pl
from jax.experimental.pallas import tpu as pltpu

F32 = jnp.float32
BF16 = jnp.bfloat16
U32 = jnp.uint32
HIGHEST = lax.Precision.HIGHEST

D_MODEL = 1024
A_HEADS = 8
A_HEAD_DIM = 64
A_WIDTH = A_HEADS * A_HEAD_DIM
DILATIONS = (1, 4, 16)
BAND = 128
ROPE_THETA = 10000.0
PAST_LEN = 8192
B_HEADS = 4
B_DIM = 128
B_QKV = 3 * B_HEADS * B_DIM
CONV_WIDTH = 4
PEER_HEADS = 8
PEER_KEYS = 128
PEER_TOPK = 16
EPS = 1e-6
NEG_INF = float("-inf")
VMEM_LIMIT = 56 * 1024 * 1024


def _params(*sem, flags=None):
    return pltpu.CompilerParams(dimension_semantics=sem, vmem_limit_bytes=VMEM_LIMIT, flags=flags)


def _dot(a, b, **kw):
    return jnp.dot(a, b, preferred_element_type=F32, **kw)


def _dot_nt(a, b, **kw):
    return lax.dot_general(a, b, (((1,), (1,)), ((), ())), preferred_element_type=F32, **kw)


def _dot_tn(a, b, **kw):
    return lax.dot_general(a, b, (((0,), (0,)), ((), ())), preferred_element_type=F32, **kw)


def _sigmoid(x):
    return 1.0 / (1.0 + jnp.exp(-x))


def _proj_in_kernel(x_ref, nw_ref, wa_ref, wb_ref, wz_ref, wg_ref, qnw_ref, knw_ref, cos_ref, sin_ref,
                    hm_ref, alog_ref, dtb_ref, q_ref, k_ref, v_ref, raw_ref, z_ref, gb_ref):
    x = x_ref[...]
    ms = jnp.mean(x * x, axis=-1, keepdims=True)
    xn = (x * lax.rsqrt(ms + EPS) * nw_ref[...]).astype(BF16)
    a = _dot(xn, wa_ref[...])
    hm = hm_ref[...]
    lane = lax.broadcasted_iota(jnp.int32, (1, A_WIDTH), 1)
    first_half = (lane & (A_HEAD_DIM // 2)) == 0
    cos = cos_ref[...]
    sin = sin_ref[...]

    def norm_rope(t, w):
        t2 = t * t
        hi = t2.astype(BF16)
        lo = (t2 - hi.astype(F32)).astype(BF16)
        ss = _dot(hi, hm) + _dot(lo, hm)
        tn = t * lax.rsqrt(ss * (1.0 / A_HEAD_DIM) + EPS) * w
        partner = jnp.where(first_half, pltpu.roll(tn, A_WIDTH - A_HEAD_DIM // 2, 1), pltpu.roll(tn, A_HEAD_DIM // 2, 1))
        return tn * cos + partner * sin

    q_ref[...] = norm_rope(a[:, :A_WIDTH], qnw_ref[...])
    k_ref[...] = norm_rope(a[:, A_WIDTH:2 * A_WIDTH], knw_ref[...])
    v_ref[...] = a[:, 2 * A_WIDTH:]
    raw_ref[...] = _dot(xn, wb_ref[...])
    z_ref[...] = _dot(xn, wz_ref[...])
    zg = _dot(xn, wg_ref[...])
    beta = _sigmoid(zg)
    t = zg + dtb_ref[...]
    softplus = jnp.maximum(t, 0.0) + jnp.log1p(jnp.exp(-jnp.abs(t)))
    g = -jnp.exp(alog_ref[...]) * softplus
    glane = lax.broadcasted_iota(jnp.int32, (1, B_HEADS * B_DIM), 1) & (B_DIM - 1)
    gb_ref[...] = jnp.where(glane == 0, beta, jnp.where(glane == 1, g, 0.0))


def _proj_in(x, pos_cos, pos_sin, pos_blocks, wts, tm):
    t = x.shape[0]
    n = t // tm
    row = lambda i: (i, 0)
    const = lambda i: (0, 0)
    posmap = lambda i: (i % pos_blocks, 0)
    full = lambda a: pl.BlockSpec(a.shape, const)
    outs = [(A_WIDTH, "q"), (A_WIDTH, "k"), (A_WIDTH, "v"), (B_QKV, "raw"), (B_HEADS * B_DIM, "z"), (B_HEADS * B_DIM, "gb")]
    return pl.pallas_call(
        _proj_in_kernel,
        grid=(n,),
        in_specs=[pl.BlockSpec((tm, D_MODEL), row), full(wts["norm_mix"]), full(wts["wa"]), full(wts["wb"]), full(wts["wz"]),
                  full(wts["wg"]), full(wts["qnw"]), full(wts["knw"]),
                  pl.BlockSpec((tm, A_WIDTH), posmap), pl.BlockSpec((tm, A_WIDTH), posmap),
                  full(wts["headmat"]), full(wts["alog"]), full(wts["dtb"])],
        out_specs=[pl.BlockSpec((tm, w), row) for w, _ in outs],
        out_shape=[jax.ShapeDtypeStruct((t, w), F32) for w, _ in outs],
        compiler_params=_params("parallel"),
        name="proj_in",
    )(x, wts["norm_mix"], wts["wa"], wts["wb"], wts["wz"], wts["wg"], wts["qnw"], wts["knw"], pos_cos, pos_sin,
      wts["headmat"], wts["alog"], wts["dtb"])


def _attn_prompt_kernel(q_ref, k_ref, v_ref, o_ref, ob_ref, lb_ref, *, seq):
    lane = lax.broadcasted_iota(jnp.int32, (BAND, 2 * A_HEAD_DIM), 1)
    head0 = lane < A_HEAD_DIM
    qi = lax.broadcasted_iota(jnp.int32, (BAND, 2 * BAND), 0)
    kj = lax.broadcasted_iota(jnp.int32, (BAND, 2 * BAND), 1)
    dist = BAND + qi - kj
    in_band = (dist >= 0) & (dist <= BAND)
    causal = in_band[:, BAND:]
    scale = 1.0 / math.sqrt(A_HEAD_DIM)

    for br, dil in enumerate(DILATIONS):
        nblk = seq // (BAND * dil)
        shift = dil.bit_length() - 1

        def rows(ref, start):
            if dil == 1:
                return ref[pl.ds(start, BAND), :]
            return ref[pl.ds(start, BAND, stride=dil), :]

        def unit(u, carry):
            r = u & (dil - 1)
            n = u >> shift
            start = r + n * (BAND * dil)
            qb = rows(q_ref, start)
            k_own = rows(k_ref, start).astype(BF16)
            v_own = rows(v_ref, start).astype(BF16)
            if nblk > 1:
                pstart = jnp.maximum(start - BAND * dil, r)
                kk = jnp.concatenate([rows(k_ref, pstart).astype(BF16), k_own], axis=0)
                vv = jnp.concatenate([rows(v_ref, pstart).astype(BF16), v_own], axis=0)
                mask = in_band & (kj >= jnp.where(n > 0, 0, BAND))
            else:
                kk, vv, mask = k_own, v_own, causal
            outs = []
            lses = []
            for hmask in (head0, ~head0):
                qh = jnp.where(hmask, qb, 0.0).astype(BF16)
                s = _dot_nt(qh, kk) * scale
                s = jnp.where(mask, s, NEG_INF)
                mx = jnp.max(s, axis=-1, keepdims=True)
                p = jnp.exp(s - mx)
                l = jnp.sum(p, axis=-1, keepdims=True)
                outs.append(_dot(p.astype(BF16), vv) / l)
                lses.append(mx + jnp.log(l))
            out = jnp.where(head0, outs[0], outs[1])
            lse = jnp.where(head0, lses[0], lses[1])
            if dil == 1:
                ob_ref[br, pl.ds(start, BAND), :] = out
                lb_ref[br, pl.ds(start, BAND), :] = lse
            else:
                ob_ref[br, pl.ds(start, BAND, stride=dil), :] = out
                lb_ref[br, pl.ds(start, BAND, stride=dil), :] = lse
            return carry

        lax.fori_loop(0, seq // BAND, unit, 0, unroll=4)

    def merge(c, carry):
        rs = pl.ds(pl.multiple_of(c * 256, 256), 256)
        l0, l1, l2 = lb_ref[0, rs, :], lb_ref[1, rs, :], lb_ref[2, rs, :]
        m = jnp.maximum(jnp.maximum(l0, l1), l2)
        w0, w1, w2 = jnp.exp(l0 - m), jnp.exp(l1 - m), jnp.exp(l2 - m)
        o_ref[rs, :] = (w0 * ob_ref[0, rs, :] + w1 * ob_ref[1, rs, :] + w2 * ob_ref[2, rs, :]) / (w0 + w1 + w2)
        return carry

    lax.fori_loop(0, seq // 256, merge, 0)


def _attn_prompt(q, k, v):
    b, s, _ = q.shape
    spec = pl.BlockSpec((None, s, 2 * A_HEAD_DIM), lambda i, j: (i, 0, j))
    return pl.pallas_call(
        functools.partial(_attn_prompt_kernel, seq=s),
        grid=(b, A_HEADS // 2),
        in_specs=[spec, spec, spec],
        out_specs=spec,
        out_shape=jax.ShapeDtypeStruct((b, s, A_WIDTH), F32),
        scratch_shapes=[pltpu.VMEM((len(DILATIONS), s, 2 * A_HEAD_DIM), F32),
                        pltpu.VMEM((len(DILATIONS), s, 2 * A_HEAD_DIM), F32)],
        compiler_params=_params("parallel", "parallel"),
        name="attn_prompt",
    )(q, k, v)


def _attn_sample_kernel(q_ref, kn_ref, vn_ref, kc_ref, vc_ref, hs_ref, he_ref, o_ref, ko_ref, vo_ref, kbuf, vbuf,
                        *, win, new):
    nlb = A_WIDTH // 128
    for buf, cache, fresh, shifted in ((kbuf, kc_ref, kn_ref, ko_ref), (vbuf, vc_ref, vn_ref, vo_ref)):
        for j in range(nlb):
            ls = slice(j * 128, (j + 1) * 128)
            buf[j, 0:win, :] = cache[:, ls]
            buf[j, win:win + new, :] = fresh[:, ls]
            shifted[:, ls] = buf[j, new:new + win, :]

    def gather(buf, start, size, stride):
        if stride == 1:
            return jnp.concatenate([buf[j, pl.ds(start, size), :] for j in range(nlb)], axis=1)
        return jnp.concatenate([buf[j, pl.ds(start, size, stride=stride), :] for j in range(nlb)], axis=1)

    hs = hs_ref[...]
    he = he_ref[...]
    scale = 1.0 / math.sqrt(A_HEAD_DIM)
    for l in range(new):
        ql = q_ref[l:l + 1, :]
        outs, lses = [], []
        for dil in DILATIONS:
            start = win + l - BAND * dil
            ks = gather(kbuf, start, BAND, dil)
            vs = gather(vbuf, start, BAND, dil)
            kself = gather(kbuf, win + l, 1, 1)
            vself = gather(vbuf, win + l, 1, 1)
            s = _dot(ks * ql, hs, precision=HIGHEST) * scale
            sself = _dot(kself * ql, hs, precision=HIGHEST) * scale
            mx = jnp.maximum(jnp.max(s, axis=0, keepdims=True), sself)
            p = jnp.exp(s - mx)
            pself = jnp.exp(sself - mx)
            lsum = jnp.sum(p, axis=0, keepdims=True) + pself
            pe = _dot(p, he, precision=HIGHEST)
            o = jnp.sum(pe * vs, axis=0, keepdims=True) + _dot(pself, he, precision=HIGHEST) * vself
            outs.append(o / _dot(lsum, he, precision=HIGHEST))
            lses.append(_dot(mx + jnp.log(lsum), he, precision=HIGHEST))
        m = jnp.maximum(jnp.maximum(lses[0], lses[1]), lses[2])
        w = [jnp.exp(ls - m) for ls in lses]
        o_ref[l:l + 1, :] = (w[0] * outs[0] + w[1] * outs[1] + w[2] * outs[2]) / (w[0] + w[1] + w[2])


def _attn_sample(q, kn, vn, kc, vc, hs, he):
    b, new, _ = q.shape
    win = kc.shape[1]
    assert win >= BAND * max(DILATIONS)
    small = pl.BlockSpec((None, new, A_WIDTH), lambda i: (i, 0, 0))
    big = pl.BlockSpec((None, win, A_WIDTH), lambda i: (i, 0, 0))
    const = lambda a: pl.BlockSpec(a.shape, lambda i: (0, 0))
    return pl.pallas_call(
        functools.partial(_attn_sample_kernel, win=win, new=new),
        grid=(b,),
        in_specs=[small, small, small, big, big, const(hs), const(he)],
        out_specs=[small, big, big],
        out_shape=[jax.ShapeDtypeStruct((b, new, A_WIDTH), F32), jax.ShapeDtypeStruct((b, win, A_WIDTH), F32),
                   jax.ShapeDtypeStruct((b, win, A_WIDTH), F32)],
        scratch_shapes=[pltpu.VMEM((A_WIDTH // 128, win + 8, 128), F32), pltpu.VMEM((A_WIDTH // 128, win + 8, 128), F32)],
        compiler_params=_params("parallel"),
        name="attn_sample",
    )(q, kn, vn, kc, vc, hs, he)


def _dot3(a, b):
    ah = a.astype(BF16)
    al = (a - ah.astype(F32)).astype(BF16)
    bh = b.astype(BF16)
    bl = (b - bh.astype(F32)).astype(BF16)
    return _dot(ah, bh) + (_dot(ah, bl) + _dot(al, bh))


def _dot_01(a01, b):
    a = a01.astype(BF16)
    b1 = b.astype(BF16)
    r1 = b - b1.astype(F32)
    b2 = r1.astype(BF16)
    b3 = (r1 - b2.astype(F32)).astype(BF16)
    return _dot(a, b1) + (_dot(a, b2) + _dot(a, b3))


def _delta_kernel(raw_ref, hist_ref, cw_ref, gb_ref, z_ref, dnw_ref, s0_ref, o_ref, sf_ref, ext, s_scr, *, blk, chunk):
    c = chunk
    nh = B_HEADS
    r = nh * c
    ncol = 3 * nh
    shift = c.bit_length() - 1
    j = pl.program_id(1)

    @pl.when(j == 0)
    def _():
        for col in range(ncol):
            ext[col, 0:8, :] = hist_ref[:, col * B_DIM:(col + 1) * B_DIM]
        for k in range(nh):
            s_scr[k * B_DIM:(k + 1) * B_DIM, :] = s0_ref[k]

    for col in range(ncol):
        ext[col, 8:8 + blk, :] = raw_ref[:, col * B_DIM:(col + 1) * B_DIM]

    ii = lax.broadcasted_iota(jnp.int32, (r, r), 0)
    jj = lax.broadcasted_iota(jnp.int32, (r, r), 1)
    same = (ii >> shift) == (jj >> shift)
    causal = same & (ii >= jj)
    strict = same & (ii > jj)
    upper = same & (ii <= jj)
    eye = (ii == jj).astype(F32)
    tri = causal.astype(F32)
    ones = jnp.ones((r, r), F32)
    row_head = lax.broadcasted_iota(jnp.int32, (r, nh * B_DIM), 0) >> shift
    lane_head = lax.broadcasted_iota(jnp.int32, (r, nh * B_DIM), 1) >> (B_DIM.bit_length() - 1)
    own_head = row_head == lane_head
    dnw = dnw_ref[...]

    def conv_silu(col, base):
        y = cw_ref[0:1, col * B_DIM:(col + 1) * B_DIM] * ext[col, pl.ds(base + 5, c), :]
        for t in range(1, CONV_WIDTH):
            y = y + cw_ref[t:t + 1, col * B_DIM:(col + 1) * B_DIM] * ext[col, pl.ds(base + 5 + t, c), :]
        return y * _sigmoid(y)

    def l2n(t):
        return t * lax.rsqrt(jnp.sum(t * t, axis=-1, keepdims=True) + EPS)

    def stack(parts):
        return jnp.concatenate(parts, axis=0)

    def block_diag(x):
        return jnp.where(own_head, jnp.concatenate([x] * nh, axis=1), 0.0)

    def body(n, carry):
        base = pl.multiple_of(n * c, c)
        q = stack([l2n(conv_silu(k, base)) for k in range(nh)]) * (B_DIM ** -0.5)
        kk = stack([l2n(conv_silu(nh + k, base)) for k in range(nh)])
        v = stack([conv_silu(2 * nh + k, base) for k in range(nh)])
        gbs = [gb_ref[pl.ds(base, c), k * B_DIM:(k + 1) * B_DIM] for k in range(nh)]
        beta = stack([t[:, 0:1] for t in gbs])
        g = stack([t[:, 1:2] for t in gbs])
        gsq = jnp.broadcast_to(g, (r, r))
        gcol = _dot_01(tri, gsq)
        grow = _dot_01(ones, jnp.where(upper, gsq, 0.0))
        gcum = gcol[:, 0:1]
        glast = stack([jnp.broadcast_to(gcol[(k + 1) * c - 1:(k + 1) * c, 0:1], (c, 1)) for k in range(nh)])
        decay = jnp.exp(jnp.where(causal, gcol - grow, NEG_INF))
        kb = kk * beta
        kbh, kh, qh = kb.astype(BF16), kk.astype(BF16), q.astype(BF16)
        lower = jnp.where(strict, _dot_nt(kbh, kh) * decay, 0.0)
        tinv = eye - lower
        lp = lower
        span = 2
        while span < c:
            lp = _dot3(lp, lp)
            tinv = tinv + _dot3(tinv, lp)
            span *= 2
        eg = jnp.exp(gcum)
        u = _dot3(tinv, v * beta)
        w = _dot3(tinv, kb * eg)
        intra = jnp.where(causal, _dot_nt(qh, kh) * decay, 0.0)
        state = s_scr[...]
        sh = state.astype(BF16)
        v_new = u - _dot(block_diag(w).astype(BF16), sh)
        vh = v_new.astype(BF16)
        out = _dot(block_diag(q * eg).astype(BF16), sh) + _dot(intra.astype(BF16), vh)
        kdec = kk * jnp.exp(glast - gcum)
        cd = stack([jnp.broadcast_to(jnp.exp(gcol[(k + 1) * c - 1:(k + 1) * c, 0:1]), (B_DIM, 1)) for k in range(nh)])
        s_scr[...] = state * cd + _dot_tn(block_diag(kdec).astype(BF16), vh)
        on = out * lax.rsqrt(jnp.mean(out * out, axis=-1, keepdims=True) + EPS) * dnw
        for k in range(nh):
            zc = z_ref[pl.ds(base, c), k * B_DIM:(k + 1) * B_DIM]
            o_ref[pl.ds(base, c), k * B_DIM:(k + 1) * B_DIM] = on[k * c:(k + 1) * c] * (zc * _sigmoid(zc))
        return carry

    lax.fori_loop(0, blk // c, body, 0)
    for col in range(ncol):
        ext[col, 0:8, :] = ext[col, blk:blk + 8, :]

    @pl.when(j == pl.num_programs(1) - 1)
    def _():
        for k in range(nh):
            sf_ref[k] = s_scr[k * B_DIM:(k + 1) * B_DIM, :]


def _delta(raw, hist, conv_w, gb, z, dnw, s0, chunk, blk):
    b, seq, _ = raw.shape
    hb = B_HEADS * B_DIM
    tok = lambda w: pl.BlockSpec((None, blk, w), lambda i, j: (i, j, 0))
    st = pl.BlockSpec((None, B_HEADS, B_DIM, B_DIM), lambda i, j: (i, 0, 0, 0))
    return pl.pallas_call(
        functools.partial(_delta_kernel, blk=blk, chunk=chunk),
        grid=(b, seq // blk),
        in_specs=[tok(B_QKV), pl.BlockSpec((None, 8, B_QKV), lambda i, j: (i, 0, 0)),
                  pl.BlockSpec((CONV_WIDTH, B_QKV), lambda i, j: (0, 0)), tok(hb), tok(hb),
                  pl.BlockSpec((1, B_DIM), lambda i, j: (0, 0)), st],
        out_specs=[tok(hb), st],
        out_shape=[jax.ShapeDtypeStruct((b, seq, hb), F32), jax.ShapeDtypeStruct((b, B_HEADS, B_DIM, B_DIM), F32)],
        scratch_shapes=[pltpu.VMEM((3 * B_HEADS, blk + 8, B_DIM), F32), pltpu.VMEM((B_HEADS * B_DIM, B_DIM), F32)],
        compiler_params=_params("parallel", "arbitrary"),
        name="delta",
    )(raw, hist, conv_w, gb, z, dnw, s0)


def _proj_out_kernel(x_ref, a_ref, b_ref, wa_ref, wb_ref, nw_ref, h_ref, hnt_ref):
    h = x_ref[...] + _dot(a_ref[...].astype(BF16), wa_ref[...]) + _dot(b_ref[...].astype(BF16), wb_ref[...])
    h_ref[...] = h
    hn = h * lax.rsqrt(jnp.mean(h * h, axis=-1, keepdims=True) + EPS) * nw_ref[...]
    hnt = hn.T.astype(BF16)
    for c in range(hnt_ref.shape[0]):
        hnt_ref[c] = hnt[:, c * 128:(c + 1) * 128]


def _proj_out(x, a, b, wts, tm):
    t = x.shape[0]
    row = lambda i: (i, 0)
    full = lambda arr: pl.BlockSpec(arr.shape, lambda i: (0, 0))
    return pl.pallas_call(
        _proj_out_kernel,
        grid=(t // tm,),
        in_specs=[pl.BlockSpec((tm, D_MODEL), row), pl.BlockSpec((tm, A_WIDTH), row), pl.BlockSpec((tm, A_WIDTH), row),
                  full(wts["wo_a"]), full(wts["wo_b"]), full(wts["norm_ffn"])],
        out_specs=[pl.BlockSpec((tm, D_MODEL), row), pl.BlockSpec((tm // 128, D_MODEL, 128), lambda i: (i, 0, 0))],
        out_shape=[jax.ShapeDtypeStruct((t, D_MODEL), F32), jax.ShapeDtypeStruct((t // 128, D_MODEL, 128), BF16)],
        compiler_params=_params("parallel"),
        name="proj_out",
    )(x, a, b, wts["wo_a"], wts["wo_b"], wts["norm_ffn"])


def _top16_ranked(x):
    t = x.shape[1]
    slot = lax.broadcasted_iota(jnp.int32, (PEER_TOPK, t), 0).astype(F32)
    vals = jnp.full((PEER_TOPK, t), NEG_INF, F32)
    rank = jnp.full(x.shape, float(PEER_TOPK), F32)
    filled = jnp.zeros((1, t), F32)
    for _ in range(PEER_TOPK):
        m = jnp.max(x, axis=0, keepdims=True)
        eq = x == m
        cnt = jnp.sum(jnp.where(eq, 1.0, 0.0), axis=0, keepdims=True)
        vals = jnp.where((slot >= filled) & (slot < filled + cnt), m, vals)
        rank = jnp.where(eq, jnp.minimum(filled, float(PEER_TOPK)), rank)
        filled = filled + cnt
        x = jnp.where(eq, NEG_INF, x)
    return vals, rank


def _kth_largest(x, k):
    t = x.shape[1]
    kth = jnp.full((1, t), NEG_INF, F32)
    filled = jnp.zeros((1, t), F32)
    for _ in range(k):
        m = jnp.max(x, axis=0, keepdims=True)
        eq = x == m
        cnt = jnp.sum(jnp.where(eq, 1.0, 0.0), axis=0, keepdims=True)
        kth = jnp.where((filled < k) & (filled + cnt >= k), m, kth)
        filled = filled + cnt
        x = jnp.where(eq, NEG_INF, x)
    return kth


def _peer_select_kernel(hnt_ref, wpq_ref, sk_ref, n1_ref, q1_ref, r2_ref, q2_ref, sub_scr, top_scr, rank_scr):
    nslab = hnt_ref.shape[0]
    hnt = jnp.concatenate([hnt_ref[c] for c in range(nslab)], axis=1)

    def half(hc, carry):
        row0 = pl.multiple_of(hc * PEER_KEYS, PEER_KEYS)
        qt = _dot(wpq_ref[pl.ds(row0, PEER_KEYS), :], hnt)
        sub = _dot(sk_ref[hc & 1], qt.astype(BF16))
        sub_scr[hc] = sub
        top_scr[hc], rank_scr[hc] = _top16_ranked(sub)
        return carry

    lax.fori_loop(0, 2 * PEER_HEADS, half, 0, unroll=2)

    def slabs(ref, h, x):
        for c in range(nslab):
            ref[h, c] = x[:, c * 128:(c + 1) * 128].astype(ref.dtype)

    def head(h, carry):
        a = top_scr[2 * h]
        b = top_scr[2 * h + 1]
        pieces = [a[0:1] + b] + [a[i:i + 1] + b[0:8] for i in range(1, 8)] + [a[8:16] + b[0:1]]
        cand = jnp.concatenate(pieces, axis=0)
        tau = _kth_largest(cand, PEER_TOPK)
        top = a[0:1] + b[0:1]
        zsum = jnp.sum(jnp.where(cand >= tau, jnp.exp(cand - top), 0.0), axis=0, keepdims=True)
        cnt = jnp.zeros_like(a)
        for j in range(PEER_TOPK):
            cnt = cnt + jnp.where(a + b[j:j + 1] >= tau, 1.0, 0.0)
        rank1 = rank_scr[2 * h]
        n1 = jnp.zeros_like(rank1)
        for i in range(PEER_TOPK):
            n1 = jnp.where(rank1 == float(i), cnt[i:i + 1], n1)
        slabs(n1_ref, h, n1)
        slabs(r2_ref, h, rank_scr[2 * h + 1])
        slabs(q1_ref, h, jnp.exp(sub_scr[2 * h] - a[0:1]) / zsum)
        slabs(q2_ref, h, jnp.exp(sub_scr[2 * h + 1] - b[0:1]))
        return carry

    lax.fori_loop(0, PEER_HEADS, head, 0)


def _peer_select(hnt, wts, tt):
    t = hnt.shape[0] * 128
    out = pl.BlockSpec((PEER_HEADS, tt // 128, PEER_KEYS, 128), lambda i: (0, i, 0, 0))
    shape = lambda dt: jax.ShapeDtypeStruct((PEER_HEADS, t // 128, PEER_KEYS, 128), dt)
    return pl.pallas_call(
        _peer_select_kernel,
        grid=(t // tt,),
        in_specs=[pl.BlockSpec((tt // 128, D_MODEL, 128), lambda i: (i, 0, 0)),
                  pl.BlockSpec(wts["wpq_t"].shape, lambda i: (0, 0)),
                  pl.BlockSpec(wts["sub_keys"].shape, lambda i: (0, 0, 0))],
        out_specs=[out, out, out, out],
        out_shape=[shape(F32), shape(F32), shape(BF16), shape(BF16)],
        scratch_shapes=[pltpu.VMEM((2 * PEER_HEADS, PEER_KEYS, tt), F32), pltpu.VMEM((2 * PEER_HEADS, PEER_TOPK, tt), F32),
                        pltpu.VMEM((2 * PEER_HEADS, PEER_KEYS, tt), F32)],
        compiler_params=_params("parallel"),
        name="peer_select",
    )(hnt, wts["wpq_t"], wts["sub_keys"])


A_PER_TILE = 8


def _rows_bf16(row):
    return jnp.broadcast_to(row, (PEER_KEYS, 128)).astype(BF16)


def _peer_dense_kernel(hnt_ref, u_ref, vt_ref, n1_ref, q1_ref, r2_ref, q2_ref, h_ref, y_ref, acc_ref, pre_ref, w_ref,
                       r2_scr, q2_scr, *, tt):
    j = pl.program_id(1)
    nslab = tt // 128

    @pl.when(j == 0)
    def _():
        acc_ref[...] = jnp.zeros_like(acc_ref)
        r2_scr[...] = r2_ref[...]
        q2_scr[...] = q2_ref[...]

    per_sub = min(nslab, 2)
    nsub = nslab // per_sub
    zero = jnp.zeros((PEER_KEYS, 128), BF16)
    group = 2

    def mm_u(s):
        hn = jnp.concatenate([hnt_ref[s * per_sub + c] for c in range(per_sub)], axis=1)
        pre = _dot(u_ref[...], hn).astype(BF16)
        for c in range(per_sub):
            pre_ref[s * per_sub + c] = pre[:, c * 128:(c + 1) * 128]

    def build(s):
        for c in range(per_sub):
            tc = s * per_sub + c
            for ia0 in range(0, A_PER_TILE, group):
                gates = [zero] * group
                for h in range(PEER_HEADS):
                    r2 = r2_scr[h, tc]
                    q2 = q2_scr[h, tc]
                    for k in range(group):
                        n1 = _rows_bf16(n1_ref[h, tc, ia0 + k:ia0 + k + 1, :])
                        q1 = _rows_bf16(q1_ref[h, tc, ia0 + k:ia0 + k + 1, :])
                        gates[k] = gates[k] + jnp.where(r2 < n1, q2, zero) * q1
                for k in range(group):
                    rows = slice((ia0 + k) * PEER_KEYS, (ia0 + k + 1) * PEER_KEYS)
                    p = pre_ref[tc, rows, :]
                    act = (0.5 * p) * (1.0 + lax.erf(p * (1.0 / math.sqrt(2.0))))
                    w_ref[tc, rows, :] = gates[k] * act

    def mm_v(s):
        w = jnp.concatenate([w_ref[s * per_sub + c] for c in range(per_sub)], axis=1)
        out = _dot(vt_ref[...], w)
        for c in range(per_sub):
            acc_ref[s * per_sub + c] += out[:, c * 128:(c + 1) * 128]

    mm_u(0)
    if nsub > 1:
        mm_u(1)
        build(0)

        def steady(s, carry):
            mm_u(s + 1)
            build(s)
            mm_v(s - 1)
            return carry

        lax.fori_loop(1, nsub - 1, steady, 0)
        build(nsub - 1)
        mm_v(nsub - 2)
    else:
        build(0)
    mm_v(nsub - 1)

    @pl.when(j == pl.num_programs(1) - 1)
    def _():
        for c in range(nslab):
            y_ref[c * 128:(c + 1) * 128, :] = h_ref[c * 128:(c + 1) * 128, :] + acc_ref[c].T


def _peer_dense(hnt, h, sel, wts, tt):
    t = hnt.shape[0] * 128
    n1, q1, r2, q2 = sel
    et = A_PER_TILE * PEER_KEYS
    nslab = tt // 128
    a_spec = pl.BlockSpec((PEER_HEADS, nslab, A_PER_TILE, 128), lambda i, j: (0, i, j, 0))
    b_spec = pl.BlockSpec((PEER_HEADS, nslab, PEER_KEYS, 128), lambda i, j: (0, i, 0, 0))
    return pl.pallas_call(
        functools.partial(_peer_dense_kernel, tt=tt),
        grid=(t // tt, PEER_KEYS // A_PER_TILE),
        in_specs=[pl.BlockSpec((nslab, D_MODEL, 128), lambda i, j: (i, 0, 0)),
                  pl.BlockSpec((et, D_MODEL), lambda i, j: (j, 0)),
                  pl.BlockSpec((D_MODEL, et), lambda i, j: (0, j)),
                  a_spec, a_spec, b_spec, b_spec,
                  pl.BlockSpec((tt, D_MODEL), lambda i, j: (i, 0))],
        out_specs=pl.BlockSpec((tt, D_MODEL), lambda i, j: (i, 0)),
        out_shape=jax.ShapeDtypeStruct((t, D_MODEL), F32),
        scratch_shapes=[pltpu.VMEM((nslab, D_MODEL, 128), F32), pltpu.VMEM((nslab, et, 128), BF16),
                        pltpu.VMEM((nslab, et, 128), BF16),
                        pltpu.VMEM((PEER_HEADS, nslab, PEER_KEYS, 128), BF16),
                        pltpu.VMEM((PEER_HEADS, nslab, PEER_KEYS, 128), BF16)],
        compiler_params=_params("parallel", "arbitrary"),
        name="peer_dense",
    )(hnt, wts["u"], wts["v_t"], n1, q1, r2, q2, h)


def _rope_tables(pos, reps):
    half = A_HEAD_DIM // 2
    inv_freq = ROPE_THETA ** (-jnp.arange(half, dtype=F32) / half)
    ang = pos.astype(F32)[:, None] * inv_freq[None, :]
    cos = jnp.cos(ang)
    sin = jnp.sin(ang)
    cos_t = jnp.tile(jnp.concatenate([cos, cos], axis=-1), (reps, A_HEADS))
    sin_t = jnp.tile(jnp.concatenate([-sin, sin], axis=-1), (reps, A_HEADS))
    return cos_t, sin_t


def _prep_weights(norm_mix_w, w_in, q_norm_w, k_norm_w, a_log, dt_bias, w_out, norm_ffn_w, w_pq, sub_keys, expert_u, expert_v):
    na = 3 * A_WIDTH
    hb = B_HEADS * B_DIM
    wgate = w_in[:, na + B_QKV + hb:]
    wg = jnp.zeros((D_MODEL, B_HEADS, B_DIM), F32)
    wg = wg.at[:, :, 0].set(wgate[:, :B_HEADS]).at[:, :, 1].set(wgate[:, B_HEADS:])
    lane_head = jnp.arange(A_WIDTH) // A_HEAD_DIM
    put1 = lambda vec: jnp.zeros((B_HEADS, B_DIM), F32).at[:, 1].set(vec).reshape(1, hb)
    return {
        "norm_mix": norm_mix_w.reshape(1, D_MODEL),
        "wa": w_in[:, :na].astype(BF16),
        "wb": w_in[:, na:na + B_QKV].astype(BF16),
        "wz": w_in[:, na + B_QKV:na + B_QKV + hb].astype(BF16),
        "wg": wg.reshape(D_MODEL, hb).astype(BF16),
        "qnw": jnp.tile(q_norm_w, A_HEADS).reshape(1, A_WIDTH),
        "knw": jnp.tile(k_norm_w, A_HEADS).reshape(1, A_WIDTH),
        "headmat": (lane_head[:, None] == lane_head[None, :]).astype(BF16),
        "alog": put1(a_log),
        "dtb": put1(dt_bias),
        "wo_a": w_out[:A_WIDTH].astype(BF16),
        "wo_b": w_out[A_WIDTH:].astype(BF16),
        "norm_ffn": norm_ffn_w.reshape(1, D_MODEL),
        "wpq_t": w_pq.T.astype(BF16),
        "sub_keys": sub_keys.astype(BF16),
        "u": expert_u.astype(BF16),
        "v_t": expert_v.T.astype(BF16),
    }


def _layer(x, pos_cos, pos_sin, pos_blocks, wts, conv_w, dnw, mixer_a, hist, s0, chunk, tm, tt_sel, tt_dense):
    b, t, _ = x.shape
    xf = x.reshape(b * t, D_MODEL)
    q, k, v, raw, z, gb = _proj_in(xf, pos_cos, pos_sin, pos_blocks, wts, tm)
    a_out, k_buf, v_buf = mixer_a(q.reshape(b, t, A_WIDTH), k.reshape(b, t, A_WIDTH), v.reshape(b, t, A_WIDTH))
    raw = raw.reshape(b, t, B_QKV)
    conv_state = raw[:, t - (CONV_WIDTH - 1):]
    pad = (-t) % chunk
    pad3 = lambda arr: jnp.pad(arr.reshape(b, t, -1), ((0, 0), (0, pad), (0, 0)))
    b_out, s_fin = _delta(pad3(raw), hist, conv_w, pad3(gb), pad3(z), dnw, s0, chunk, min(512, t + pad))
    b_out = b_out[:, :t].reshape(b * t, B_HEADS * B_DIM)
    h, hnt = _proj_out(xf, a_out.reshape(b * t, A_WIDTH), b_out, wts, tm)
    sel = _peer_select(hnt, wts, tt_sel)
    y = _peer_dense(hnt, h, sel, wts, tt_dense)
    return y.reshape(b, t, D_MODEL), k_buf, v_buf, conv_state, s_fin


def kernel(x_prompt, x_sample, cache_k_win, cache_v_win, state_conv, state_delta, norm_mix_w, w_in, q_norm_w, k_norm_w, conv_w, a_log, dt_bias, delta_norm_w, w_out, norm_ffn_w, w_pq, sub_keys, expert_u, expert_v):
    assert w_in.shape[0] == 1, "one layer"
    bp, sp, _ = x_prompt.shape
    bs, ss, _ = x_sample.shape
    wts = _prep_weights(norm_mix_w[0], w_in[0], q_norm_w[0], k_norm_w[0], a_log[0], dt_bias[0], w_out[0], norm_ffn_w[0],
                        w_pq[0], sub_keys[0], expert_u[0], expert_v[0])
    cw = conv_w[0]
    dnw = delta_norm_w[0].reshape(1, B_DIM)
    lane_head = jnp.arange(A_WIDTH) // A_HEAD_DIM
    head_sum = (lane_head[:, None] == jnp.arange(128)[None, :]).astype(F32)
    head_expand = head_sum.T

    tm_p = 512
    cos_p, sin_p = _rope_tables(jnp.arange(sp), 1)

    def mixer_prompt(q, k, v):
        return _attn_prompt(q, k, v), k, v

    yp, kp, vp, cp, dp = _layer(
        x_prompt, cos_p, sin_p, sp // tm_p, wts, cw, dnw, mixer_prompt,
        jnp.zeros((bp, 8, B_QKV), F32), jnp.zeros((bp, B_HEADS, B_DIM, B_DIM), F32),
        chunk=64, tm=tm_p, tt_sel=256, tt_dense=1024)

    n_s = bs * ss
    cos_s, sin_s = _rope_tables(PAST_LEN + jnp.arange(ss), bs)

    def mixer_sample(q, k, v):
        kc = cache_k_win[0].reshape(bs, -1, A_WIDTH)
        vc = cache_v_win[0].reshape(bs, -1, A_WIDTH)
        return _attn_sample(q, k, v, kc, vc, head_sum, head_expand)

    hist_s = jnp.pad(state_conv[0], ((0, 0), (8 - (CONV_WIDTH - 1), 0), (0, 0)))
    ys, ks, vs, cs, ds = _layer(
        x_sample, cos_s, sin_s, 1, wts, cw, dnw, mixer_sample, hist_s, state_delta[0],
        chunk=8, tm=n_s, tt_sel=n_s, tt_dense=n_s)

    win_p = min(BAND * max(DILATIONS), sp)
    shape_kv = lambda arr, b: arr.reshape(1, b, -1, A_HEADS, A_HEAD_DIM)
    return (yp, ys,
            shape_kv(kp[:, sp - win_p:], bp), shape_kv(vp[:, sp - win_p:], bp), cp[None], dp[None],
            shape_kv(ks, bs), shape_kv(vs, bs), cs[None], ds[None])
```

```python
import functools
import math

import jax
import jax.numpy as jnp
from jax import lax
from jax.experimental import pallas as pl
from jax.experimental.pallas import tpu as pltpu

F32 = jnp.float32
BF16 = jnp.bfloat16
U32 = jnp.uint32
HIGHEST = lax.Precision.HIGHEST

D_MODEL = 1024
A_HEADS = 8
A_HEAD_DIM = 64
A_WIDTH = A_HEADS * A_HEAD_DIM
DILATIONS = (1, 4, 16)
BAND = 128
ROPE_THETA = 10000.0
PAST_LEN = 8192
B_HEADS = 4
B_DIM = 128
B_QKV = 3 * B_HEADS * B_DIM
CONV_WIDTH = 4
PEER_HEADS = 8
PEER_KEYS = 128
PEER_TOPK = 16
EPS = 1e-6
NEG_INF = float("-inf")
VMEM_LIMIT = 56 * 1024 * 1024


def _params(*sem, flags=None):
    return pltpu.CompilerParams(dimension_semantics=sem, vmem_limit_bytes=VMEM_LIMIT, flags=flags)


def _dot(a, b, **kw):
    return jnp.dot(a, b, preferred_element_type=F32, **kw)


def _dot_nt(a, b, **kw):
    return lax.dot_general(a, b, (((1,), (1,)), ((), ())), preferred_element_type=F32, **kw)


def _dot_tn(a, b, **kw):
    return lax.dot_general(a, b, (((0,), (0,)), ((), ())), preferred_element_type=F32, **kw)


def _dot_x01(a, b01):
    b = b01.astype(BF16)
    a1 = a.astype(BF16)
    r1 = a - a1.astype(F32)
    a2 = r1.astype(BF16)
    a3 = (r1 - a2.astype(F32)).astype(BF16)
    return _dot(a1, b) + (_dot(a2, b) + _dot(a3, b))


def _sigmoid(x):
    return 1.0 / (1.0 + jnp.exp(-x))


def _proj_in_kernel(x_ref, nw_ref, wa_ref, wb_ref, wz_ref, wg_ref, qnw_ref, knw_ref, cos_ref, sin_ref,
                    hm_ref, alog_ref, dtb_ref, q_ref, k_ref, v_ref, raw_ref, z_ref, gb_ref):
    x = x_ref[...]
    ms = jnp.mean(x * x, axis=-1, keepdims=True)
    xn = (x * lax.rsqrt(ms + EPS) * nw_ref[...]).astype(BF16)
    a = _dot(xn, wa_ref[...])
    hm = hm_ref[...]
    lane = lax.broadcasted_iota(jnp.int32, (1, A_WIDTH), 1)
    first_half = (lane & (A_HEAD_DIM // 2)) == 0
    cos = cos_ref[...]
    sin = sin_ref[...]

    def norm_rope(t, w):
        t2 = t * t
        hi = t2.astype(BF16)
        lo = (t2 - hi.astype(F32)).astype(BF16)
        ss = _dot(hi, hm) + _dot(lo, hm)
        tn = t * lax.rsqrt(ss * (1.0 / A_HEAD_DIM) + EPS) * w
        partner = jnp.where(first_half, pltpu.roll(tn, A_WIDTH - A_HEAD_DIM // 2, 1), pltpu.roll(tn, A_HEAD_DIM // 2, 1))
        return tn * cos + partner * sin

    q_ref[...] = norm_rope(a[:, :A_WIDTH], qnw_ref[...])
    k_ref[...] = norm_rope(a[:, A_WIDTH:2 * A_WIDTH], knw_ref[...])
    v_ref[...] = a[:, 2 * A_WIDTH:]
    raw_ref[...] = _dot(xn, wb_ref[...])
    z_ref[...] = _dot(xn, wz_ref[...])
    zg = _dot(xn, wg_ref[...])
    beta = _sigmoid(zg)
    t = zg + dtb_ref[...]
    softplus = jnp.maximum(t, 0.0) + jnp.log1p(jnp.exp(-jnp.abs(t)))
    g = -jnp.exp(alog_ref[...]) * softplus
    glane = lax.broadcasted_iota(jnp.int32, (1, B_HEADS * B_DIM), 1) & (B_DIM - 1)
    gb_ref[...] = jnp.where(glane == 0, beta, jnp.where(glane == 1, g, 0.0))


def _proj_in(x, pos_cos, pos_sin, pos_blocks, wts, tm):
    t = x.shape[0]
    n = t // tm
    row = lambda i: (i, 0)
    const = lambda i: (0, 0)
    posmap = lambda i: (i % pos_blocks, 0)
    full = lambda a: pl.BlockSpec(a.shape, const)
    outs = [(A_WIDTH, "q"), (A_WIDTH, "k"), (A_WIDTH, "v"), (B_QKV, "raw"), (B_HEADS * B_DIM, "z"), (B_HEADS * B_DIM, "gb")]
    return pl.pallas_call(
        _proj_in_kernel,
        grid=(n,),
        in_specs=[pl.BlockSpec((tm, D_MODEL), row), full(wts["norm_mix"]), full(wts["wa"]), full(wts["wb"]), full(wts["wz"]),
                  full(wts["wg"]), full(wts["qnw"]), full(wts["knw"]),
                  pl.BlockSpec((tm, A_WIDTH), posmap), pl.BlockSpec((tm, A_WIDTH), posmap),
                  full(wts["headmat"]), full(wts["alog"]), full(wts["dtb"])],
        out_specs=[pl.BlockSpec((tm, w), row) for w, _ in outs],
        out_shape=[jax.ShapeDtypeStruct((t, w), F32) for w, _ in outs],
        compiler_params=_params("parallel"),
        name="proj_in",
    )(x, wts["norm_mix"], wts["wa"], wts["wb"], wts["wz"], wts["wg"], wts["qnw"], wts["knw"], pos_cos, pos_sin,
      wts["headmat"], wts["alog"], wts["dtb"])


def _attn_prompt_kernel(q_ref, k_ref, v_ref, o_ref, ob_ref, lb_ref, *, seq):
    lane = lax.broadcasted_iota(jnp.int32, (BAND, 2 * A_HEAD_DIM), 1)
    head0 = lane < A_HEAD_DIM
    qi = lax.broadcasted_iota(jnp.int32, (BAND, 2 * BAND), 0)
    kj = lax.broadcasted_iota(jnp.int32, (BAND, 2 * BAND), 1)
    dist = BAND + qi - kj
    in_band = (dist >= 0) & (dist <= BAND)
    causal = in_band[:, BAND:]
    scale = 1.0 / math.sqrt(A_HEAD_DIM)

    for br, dil in enumerate(DILATIONS):
        nblk = seq // (BAND * dil)
        shift = dil.bit_length() - 1

        def rows(ref, start):
            if dil == 1:
                return ref[pl.ds(start, BAND), :]
            return ref[pl.ds(start, BAND, stride=dil), :]

        def unit(u, carry):
            r = u & (dil - 1)
            n = u >> shift
            start = r + n * (BAND * dil)
            qb = rows(q_ref, start)
            k_own = rows(k_ref, start).astype(BF16)
            v_own = rows(v_ref, start).astype(BF16)
            if nblk > 1:
                pstart = jnp.maximum(start - BAND * dil, r)
                kk = jnp.concatenate([rows(k_ref, pstart).astype(BF16), k_own], axis=0)
                vv = jnp.concatenate([rows(v_ref, pstart).astype(BF16), v_own], axis=0)
                mask = in_band & (kj >= jnp.where(n > 0, 0, BAND))
            else:
                kk, vv, mask = k_own, v_own, causal
            outs = []
            lses = []
            for hmask in (head0, ~head0):
                qh = jnp.where(hmask, qb, 0.0).astype(BF16)
                s = _dot_nt(qh, kk) * scale
                s = jnp.where(mask, s, NEG_INF)
                mx = jnp.max(s, axis=-1, keepdims=True)
                p = jnp.exp(s - mx)
                l = jnp.sum(p, axis=-1, keepdims=True)
                outs.append(_dot(p.astype(BF16), vv) / l)
                lses.append(mx + jnp.log(l))
            out = jnp.where(head0, outs[0], outs[1])
            lse = jnp.where(head0, lses[0], lses[1])
            if dil == 1:
                ob_ref[br, pl.ds(start, BAND), :] = out
                lb_ref[br, pl.ds(start, BAND), :] = lse
            else:
                ob_ref[br, pl.ds(start, BAND, stride=dil), :] = out
                lb_ref[br, pl.ds(start, BAND, stride=dil), :] = lse
            return carry

        lax.fori_loop(0, seq // BAND, unit, 0, unroll=4)

    def merge(c, carry):
        rs = pl.ds(pl.multiple_of(c * 256, 256), 256)
        l0, l1, l2 = lb_ref[0, rs, :], lb_ref[1, rs, :], lb_ref[2, rs, :]
        m = jnp.maximum(jnp.maximum(l0, l1), l2)
        w0, w1, w2 = jnp.exp(l0 - m), jnp.exp(l1 - m), jnp.exp(l2 - m)
        o_ref[rs, :] = (w0 * ob_ref[0, rs, :] + w1 * ob_ref[1, rs, :] + w2 * ob_ref[2, rs, :]) / (w0 + w1 + w2)
        return carry

    lax.fori_loop(0, seq // 256, merge, 0)


def _attn_prompt(q, k, v):
    b, s, _ = q.shape
    spec = pl.BlockSpec((None, s, 2 * A_HEAD_DIM), lambda i, j: (i, 0, j))
    return pl.pallas_call(
        functools.partial(_attn_prompt_kernel, seq=s),
        grid=(b, A_HEADS // 2),
        in_specs=[spec, spec, spec],
        out_specs=spec,
        out_shape=jax.ShapeDtypeStruct((b, s, A_WIDTH), F32),
        scratch_shapes=[pltpu.VMEM((len(DILATIONS), s, 2 * A_HEAD_DIM), F32),
                        pltpu.VMEM((len(DILATIONS), s, 2 * A_HEAD_DIM), F32)],
        compiler_params=_params("parallel", "parallel"),
        name="attn_prompt",
    )(q, k, v)


def _attn_sample_kernel(q_ref, kn_ref, vn_ref, kc_ref, vc_ref, hs_ref, he_ref, o_ref, ko_ref, vo_ref, kbuf, vbuf,
                        *, win, new):
    nlb = A_WIDTH // 128
    for buf, cache, fresh, shifted in ((kbuf, kc_ref, kn_ref, ko_ref), (vbuf, vc_ref, vn_ref, vo_ref)):
        for j in range(nlb):
            ls = slice(j * 128, (j + 1) * 128)
            buf[j, 0:win, :] = cache[:, ls]
            buf[j, win:win + new, :] = fresh[:, ls]
            shifted[:, ls] = buf[j, new:new + win, :]

    def gather(buf, start, size, stride):
        if stride == 1:
            return jnp.concatenate([buf[j, pl.ds(start, size), :] for j in range(nlb)], axis=1)
        return jnp.concatenate([buf[j, pl.ds(start, size, stride=stride), :] for j in range(nlb)], axis=1)

    hs = hs_ref[...]
    he = he_ref[...]
    scale = 1.0 / math.sqrt(A_HEAD_DIM)
    for l in range(new):
        ql = q_ref[l:l + 1, :]
        outs, lses = [], []
        for dil in DILATIONS:
            start = win + l - BAND * dil
            ks = gather(kbuf, start, BAND, dil)
            vs = gather(vbuf, start, BAND, dil)
            kself = gather(kbuf, win + l, 1, 1)
            vself = gather(vbuf, win + l, 1, 1)
            s = _dot_x01(ks * ql, hs) * scale
            sself = _dot_x01(kself * ql, hs) * scale
            mx = jnp.maximum(jnp.max(s, axis=0, keepdims=True), sself)
            p = jnp.exp(s - mx)
            pself = jnp.exp(sself - mx)
            lsum = jnp.sum(p, axis=0, keepdims=True) + pself
            pe = _dot_x01(p, he)
            o = jnp.sum(pe * vs, axis=0, keepdims=True) + _dot_x01(pself, he) * vself
            outs.append(o / _dot_x01(lsum, he))
            lses.append(_dot_x01(mx + jnp.log(lsum), he))
        m = jnp.maximum(jnp.maximum(lses[0], lses[1]), lses[2])
        w = [jnp.exp(ls - m) for ls in lses]
        o_ref[l:l + 1, :] = (w[0] * outs[0] + w[1] * outs[1] + w[2] * outs[2]) / (w[0] + w[1] + w[2])


def _attn_sample(q, kn, vn, kc, vc, hs, he):
    b, new, _ = q.shape
    win = kc.shape[1]
    assert win >= BAND * max(DILATIONS)
    small = pl.BlockSpec((None, new, A_WIDTH), lambda i: (i, 0, 0))
    big = pl.BlockSpec((None, win, A_WIDTH), lambda i: (i, 0, 0))
    const = lambda a: pl.BlockSpec(a.shape, lambda i: (0, 0))
    return pl.pallas_call(
        functools.partial(_attn_sample_kernel, win=win, new=new),
        grid=(b,),
        in_specs=[small, small, small, big, big, const(hs), const(he)],
        out_specs=[small, big, big],
        out_shape=[jax.ShapeDtypeStruct((b, new, A_WIDTH), F32), jax.ShapeDtypeStruct((b, win, A_WIDTH), F32),
                   jax.ShapeDtypeStruct((b, win, A_WIDTH), F32)],
        scratch_shapes=[pltpu.VMEM((A_WIDTH // 128, win + 8, 128), F32), pltpu.VMEM((A_WIDTH // 128, win + 8, 128), F32)],
        compiler_params=_params("parallel"),
        name="attn_sample",
    )(q, kn, vn, kc, vc, hs, he)


def _dot3(a, b):
    ah = a.astype(BF16)
    al = (a - ah.astype(F32)).astype(BF16)
    bh = b.astype(BF16)
    bl = (b - bh.astype(F32)).astype(BF16)
    return _dot(ah, bh) + (_dot(ah, bl) + _dot(al, bh))


def _dot_01(a01, b):
    a = a01.astype(BF16)
    b1 = b.astype(BF16)
    r1 = b - b1.astype(F32)
    b2 = r1.astype(BF16)
    b3 = (r1 - b2.astype(F32)).astype(BF16)
    return _dot(a, b1) + (_dot(a, b2) + _dot(a, b3))


def _delta_kernel(raw_ref, hist_ref, cw_ref, gb_ref, z_ref, dnw_ref, s0_ref, o_ref, sf_ref, ext, s_scr, *, blk, chunk):
    c = chunk
    nh = B_HEADS
    r = nh * c
    ncol = 3 * nh
    shift = c.bit_length() - 1
    j = pl.program_id(1)

    @pl.when(j == 0)
    def _():
        for col in range(ncol):
            ext[col, 0:8, :] = hist_ref[:, col * B_DIM:(col + 1) * B_DIM]
        for k in range(nh):
            s_scr[k * B_DIM:(k + 1) * B_DIM, :] = s0_ref[k]

    for col in range(ncol):
        ext[col, 8:8 + blk, :] = raw_ref[:, col * B_DIM:(col + 1) * B_DIM]

    ii = lax.broadcasted_iota(jnp.int32, (r, r), 0)
    jj = lax.broadcasted_iota(jnp.int32, (r, r), 1)
    same = (ii >> shift) == (jj >> shift)
    causal = same & (ii >= jj)
    strict = same & (ii > jj)
    upper = same & (ii <= jj)
    eye = (ii == jj).astype(F32)
    tri = causal.astype(F32)
    ones = jnp.ones((r, r), F32)
    row_head = lax.broadcasted_iota(jnp.int32, (r, nh * B_DIM), 0) >> shift
    lane_head = lax.broadcasted_iota(jnp.int32, (r, nh * B_DIM), 1) >> (B_DIM.bit_length() - 1)
    own_head = row_head == lane_head
    dnw = dnw_ref[...]

    def conv_silu(col, base):
        y = cw_ref[0:1, col * B_DIM:(col + 1) * B_DIM] * ext[col, pl.ds(base + 5, c), :]
        for t in range(1, CONV_WIDTH):
            y = y + cw_ref[t:t + 1, col * B_DIM:(col + 1) * B_DIM] * ext[col, pl.ds(base + 5 + t, c), :]
        return y * _sigmoid(y)

    def l2n(t):
        return t * lax.rsqrt(jnp.sum(t * t, axis=-1, keepdims=True) + EPS)

    def stack(parts):
        return jnp.concatenate(parts, axis=0)

    def block_diag(x):
        return jnp.where(own_head, jnp.concatenate([x] * nh, axis=1), 0.0)

    def body(n, carry):
        base = pl.multiple_of(n * c, c)
        q = stack([l2n(conv_silu(k, base)) for k in range(nh)]) * (B_DIM ** -0.5)
        kk = stack([l2n(conv_silu(nh + k, base)) for k in range(nh)])
        v = stack([conv_silu(2 * nh + k, base) for k in range(nh)])
        gbs = [gb_ref[pl.ds(base, c), k * B_DIM:(k + 1) * B_DIM] for k in range(nh)]
        beta = stack([t[:, 0:1] for t in gbs])
        g = stack([t[:, 1:2] for t in gbs])
        gsq = jnp.broadcast_to(g, (r, r))
        gcol = _dot_01(tri, gsq)
        grow = _dot_01(ones, jnp.where(upper, gsq, 0.0))
        gcum = gcol[:, 0:1]
        glast = stack([jnp.broadcast_to(gcol[(k + 1) * c - 1:(k + 1) * c, 0:1], (c, 1)) for k in range(nh)])
        decay = jnp.exp(jnp.where(causal, gcol - grow, NEG_INF))
        kb = kk * beta
        kbh, kh, qh = kb.astype(BF16), kk.astype(BF16), q.astype(BF16)
        lower = jnp.where(strict, _dot_nt(kbh, kh) * decay, 0.0)
        tinv = eye - lower
        lp = lower
        span = 2
        while span < c:
            lp = _dot3(lp, lp)
            tinv = tinv + _dot3(tinv, lp)
            span *= 2
        eg = jnp.exp(gcum)
        u = _dot3(tinv, v * beta)
        w = _dot3(tinv, kb * eg)
        intra = jnp.where(causal, _dot_nt(qh, kh) * decay, 0.0)
        state = s_scr[...]
        sh = state.astype(BF16)
        v_new = u - _dot(block_diag(w).astype(BF16), sh)
        vh = v_new.astype(BF16)
        out = _dot(block_diag(q * eg).astype(BF16), sh) + _dot(intra.astype(BF16), vh)
        kdec = kk * jnp.exp(glast - gcum)
        cd = stack([jnp.broadcast_to(jnp.exp(gcol[(k + 1) * c - 1:(k + 1) * c, 0:1]), (B_DIM, 1)) for k in range(nh)])
        s_scr[...] = state * cd + _dot_tn(block_diag(kdec).astype(BF16), vh)
        on = out * lax.rsqrt(jnp.mean(out * out, axis=-1, keepdims=True) + EPS) * dnw
        for k in range(nh):
            zc = z_ref[pl.ds(base, c), k * B_DIM:(k + 1) * B_DIM]
            o_ref[pl.ds(base, c), k * B_DIM:(k + 1) * B_DIM] = on[k * c:(k + 1) * c] * (zc * _sigmoid(zc))
        return carry

    lax.fori_loop(0, blk // c, body, 0)
    for col in range(ncol):
        ext[col, 0:8, :] = ext[col, blk:blk + 8, :]

    @pl.when(j == pl.num_programs(1) - 1)
    def _():
        for k in range(nh):
            sf_ref[k] = s_scr[k * B_DIM:(k + 1) * B_DIM, :]


def _delta(raw, hist, conv_w, gb, z, dnw, s0, chunk, blk):
    b, seq, _ = raw.shape
    hb = B_HEADS * B_DIM
    tok = lambda w: pl.BlockSpec((None, blk, w), lambda i, j: (i, j, 0))
    st = pl.BlockSpec((None, B_HEADS, B_DIM, B_DIM), lambda i, j: (i, 0, 0, 0))
    return pl.pallas_call(
        functools.partial(_delta_kernel, blk=blk, chunk=chunk),
        grid=(b, seq // blk),
        in_specs=[tok(B_QKV), pl.BlockSpec((None, 8, B_QKV), lambda i, j: (i, 0, 0)),
                  pl.BlockSpec((CONV_WIDTH, B_QKV), lambda i, j: (0, 0)), tok(hb), tok(hb),
                  pl.BlockSpec((1, B_DIM), lambda i, j: (0, 0)), st],
        out_specs=[tok(hb), st],
        out_shape=[jax.ShapeDtypeStruct((b, seq, hb), F32), jax.ShapeDtypeStruct((b, B_HEADS, B_DIM, B_DIM), F32)],
        scratch_shapes=[pltpu.VMEM((3 * B_HEADS, blk + 8, B_DIM), F32), pltpu.VMEM((B_HEADS * B_DIM, B_DIM), F32)],
        compiler_params=_params("parallel", "arbitrary"),
        name="delta",
    )(raw, hist, conv_w, gb, z, dnw, s0)


def _proj_out_kernel(x_ref, a_ref, b_ref, wa_ref, wb_ref, nw_ref, h_ref, hnt_ref):
    h = x_ref[...] + _dot(a_ref[...].astype(BF16), wa_ref[...]) + _dot(b_ref[...].astype(BF16), wb_ref[...])
    h_ref[...] = h
    hn = h * lax.rsqrt(jnp.mean(h * h, axis=-1, keepdims=True) + EPS) * nw_ref[...]
    hnt = hn.T.astype(BF16)
    for c in range(hnt_ref.shape[0]):
        hnt_ref[c] = hnt[:, c * 128:(c + 1) * 128]


def _proj_out(x, a, b, wts, tm):
    t = x.shape[0]
    row = lambda i: (i, 0)
    full = lambda arr: pl.BlockSpec(arr.shape, lambda i: (0, 0))
    return pl.pallas_call(
        _proj_out_kernel,
        grid=(t // tm,),
        in_specs=[pl.BlockSpec((tm, D_MODEL), row), pl.BlockSpec((tm, A_WIDTH), row), pl.BlockSpec((tm, A_WIDTH), row),
                  full(wts["wo_a"]), full(wts["wo_b"]), full(wts["norm_ffn"])],
        out_specs=[pl.BlockSpec((tm, D_MODEL), row), pl.BlockSpec((tm // 128, D_MODEL, 128), lambda i: (i, 0, 0))],
        out_shape=[jax.ShapeDtypeStruct((t, D_MODEL), F32), jax.ShapeDtypeStruct((t // 128, D_MODEL, 128), BF16)],
        compiler_params=_params("parallel"),
        name="proj_out",
    )(x, a, b, wts["wo_a"], wts["wo_b"], wts["norm_ffn"])


def _top16_ranked(x):
    t = x.shape[1]
    slot = lax.broadcasted_iota(jnp.int32, (PEER_TOPK, t), 0).astype(F32)
    vals = jnp.full((PEER_TOPK, t), NEG_INF, F32)
    rank = jnp.full(x.shape, float(PEER_TOPK), F32)
    filled = jnp.zeros((1, t), F32)
    for _ in range(PEER_TOPK):
        m = jnp.max(x, axis=0, keepdims=True)
        eq = x == m
        cnt = jnp.sum(jnp.where(eq, 1.0, 0.0), axis=0, keepdims=True)
        vals = jnp.where((slot >= filled) & (slot < filled + cnt), m, vals)
        rank = jnp.where(eq, jnp.minimum(filled, float(PEER_TOPK)), rank)
        filled = filled + cnt
        x = jnp.where(eq, NEG_INF, x)
    return vals, rank


def _kth_largest(x, k):
    t = x.shape[1]
    kth = jnp.full((1, t), NEG_INF, F32)
    filled = jnp.zeros((1, t), F32)
    for _ in range(k):
        m = jnp.max(x, axis=0, keepdims=True)
        eq = x == m
        cnt = jnp.sum(jnp.where(eq, 1.0, 0.0), axis=0, keepdims=True)
        kth = jnp.where((filled < k) & (filled + cnt >= k), m, kth)
        filled = filled + cnt
        x = jnp.where(eq, NEG_INF, x)
    return kth


def _peer_select_kernel(hnt_ref, wpq_ref, sk_ref, n1_ref, q1_ref, r2_ref, q2_ref, sub_scr, top_scr, rank_scr):
    nslab = hnt_ref.shape[0]
    hnt = jnp.concatenate([hnt_ref[c] for c in range(nslab)], axis=1)

    def half(hc, carry):
        row0 = pl.multiple_of(hc * PEER_KEYS, PEER_KEYS)
        qt = _dot(wpq_ref[pl.ds(row0, PEER_KEYS), :], hnt)
        sub = _dot(sk_ref[hc & 1], qt.astype(BF16))
        sub_scr[hc] = sub
        top_scr[hc], rank_scr[hc] = _top16_ranked(sub)
        return carry

    lax.fori_loop(0, 2 * PEER_HEADS, half, 0, unroll=2)

    def slabs(ref, h, x):
        for c in range(nslab):
            ref[h, c] = x[:, c * 128:(c + 1) * 128].astype(ref.dtype)

    def head(h, carry):
        a = top_scr[2 * h]
        b = top_scr[2 * h + 1]
        pieces = [a[0:1] + b] + [a[i:i + 1] + b[0:8] for i in range(1, 8)] + [a[8:16] + b[0:1]]
        cand = jnp.concatenate(pieces, axis=0)
        tau = _kth_largest(cand, PEER_TOPK)
        top = a[0:1] + b[0:1]
        zsum = jnp.sum(jnp.where(cand >= tau, jnp.exp(cand - top), 0.0), axis=0, keepdims=True)
        cnt = jnp.zeros_like(a)
        for j in range(PEER_TOPK):
            cnt = cnt + jnp.where(a + b[j:j + 1] >= tau, 1.0, 0.0)
        rank1 = rank_scr[2 * h]
        n1 = jnp.zeros_like(rank1)
        for i in range(PEER_TOPK):
            n1 = jnp.where(rank1 == float(i), cnt[i:i + 1], n1)
        slabs(n1_ref, h, n1)
        slabs(r2_ref, h, rank_scr[2 * h + 1])
        slabs(q1_ref, h, jnp.exp(sub_scr[2 * h] - a[0:1]) / zsum)
        slabs(q2_ref, h, jnp.exp(sub_scr[2 * h + 1] - b[0:1]))
        return carry

    lax.fori_loop(0, PEER_HEADS, head, 0, unroll=2)


def _peer_select(hnt, wts, tt):
    t = hnt.shape[0] * 128
    out = pl.BlockSpec((PEER_HEADS, tt // 128, PEER_KEYS, 128), lambda i: (0, i, 0, 0))
    shape = lambda dt: jax.ShapeDtypeStruct((PEER_HEADS, t // 128, PEER_KEYS, 128), dt)
    return pl.pallas_call(
        _peer_select_kernel,
        grid=(t // tt,),
        in_specs=[pl.BlockSpec((tt // 128, D_MODEL, 128), lambda i: (i, 0, 0)),
                  pl.BlockSpec(wts["wpq_t"].shape, lambda i: (0, 0)),
                  pl.BlockSpec(wts["sub_keys"].shape, lambda i: (0, 0, 0))],
        out_specs=[out, out, out, out],
        out_shape=[shape(F32), shape(F32), shape(BF16), shape(BF16)],
        scratch_shapes=[pltpu.VMEM((2 * PEER_HEADS, PEER_KEYS, tt), F32), pltpu.VMEM((2 * PEER_HEADS, PEER_TOPK, tt), F32),
                        pltpu.VMEM((2 * PEER_HEADS, PEER_KEYS, tt), F32)],
        compiler_params=_params("parallel"),
        name="peer_select",
    )(hnt, wts["wpq_t"], wts["sub_keys"])


A_PER_TILE = 8


def _rows_bf16(row):
    return jnp.broadcast_to(row, (PEER_KEYS, 128)).astype(BF16)


def _peer_dense_kernel(hnt_ref, u_ref, vt_ref, n1_ref, q1_ref, r2_ref, q2_ref, h_ref, y_ref, acc_ref, pre_ref, w_ref,
                       r2_scr, q2_scr, *, tt):
    j = pl.program_id(1)
    nslab = tt // 128

    @pl.when(j == 0)
    def _():
        acc_ref[...] = jnp.zeros_like(acc_ref)
        r2_scr[...] = r2_ref[...]
        q2_scr[...] = q2_ref[...]

    per_sub = min(nslab, 2)
    nsub = nslab // per_sub
    zero = jnp.zeros((PEER_KEYS, 128), BF16)
    group = 2

    def mm_u(s):
        hn = jnp.concatenate([hnt_ref[s * per_sub + c] for c in range(per_sub)], axis=1)
        pre = _dot(u_ref[...], hn).astype(BF16)
        for c in range(per_sub):
            pre_ref[s * per_sub + c] = pre[:, c * 128:(c + 1) * 128]

    def build(tc, carry):
        for ia0 in range(0, A_PER_TILE, group):
            gates = [zero] * group
            for h in range(PEER_HEADS):
                r2 = r2_scr[h, tc]
                q2 = q2_scr[h, tc]
                for k in range(group):
                    n1 = _rows_bf16(n1_ref[h, tc, ia0 + k:ia0 + k + 1, :])
                    q1 = _rows_bf16(q1_ref[h, tc, ia0 + k:ia0 + k + 1, :])
                    gates[k] = gates[k] + jnp.where(r2 < n1, q2, zero) * q1
            for k in range(group):
                rows = slice((ia0 + k) * PEER_KEYS, (ia0 + k + 1) * PEER_KEYS)
                p = pre_ref[tc, rows, :]
                act = (0.5 * p) * (1.0 + lax.erf(p * (1.0 / math.sqrt(2.0))))
                w_ref[tc, rows, :] = gates[k] * act
        return carry

    def mm_v(s):
        w = jnp.concatenate([w_ref[s * per_sub + c] for c in range(per_sub)], axis=1)
        out = _dot(vt_ref[...], w)
        for c in range(per_sub):
            acc_ref[s * per_sub + c] += out[:, c * 128:(c + 1) * 128]

    for s in range(nsub):
        mm_u(s)
    lax.fori_loop(0, nslab, build, 0)
    for s in range(nsub):
        mm_v(s)

    @pl.when(j == pl.num_programs(1) - 1)
    def _():
        for c in range(nslab):
            y_ref[c * 128:(c + 1) * 128, :] = h_ref[c * 128:(c + 1) * 128, :] + acc_ref[c].T


def _peer_dense(hnt, h, sel, wts, tt):
    t = hnt.shape[0] * 128
    n1, q1, r2, q2 = sel
    et = A_PER_TILE * PEER_KEYS
    nslab = tt // 128
    a_spec = pl.BlockSpec((PEER_HEADS, nslab, A_PER_TILE, 128), lambda i, j: (0, i, j, 0))
    b_spec = pl.BlockSpec((PEER_HEADS, nslab, PEER_KEYS, 128), lambda i, j: (0, i, 0, 0))
    return pl.pallas_call(
        functools.partial(_peer_dense_kernel, tt=tt),
        grid=(t // tt, PEER_KEYS // A_PER_TILE),
        in_specs=[pl.BlockSpec((nslab, D_MODEL, 128), lambda i, j: (i, 0, 0)),
                  pl.BlockSpec((et, D_MODEL), lambda i, j: (j, 0)),
                  pl.BlockSpec((D_MODEL, et), lambda i, j: (0, j)),
                  a_spec, a_spec, b_spec, b_spec,
                  pl.BlockSpec((tt, D_MODEL), lambda i, j: (i, 0))],
        out_specs=pl.BlockSpec((tt, D_MODEL), lambda i, j: (i, 0)),
        out_shape=jax.ShapeDtypeStruct((t, D_MODEL), F32),
        scratch_shapes=[pltpu.VMEM((nslab, D_MODEL, 128), F32), pltpu.VMEM((nslab, et, 128), BF16),
                        pltpu.VMEM((nslab, et, 128), BF16),
                        pltpu.VMEM((PEER_HEADS, nslab, PEER_KEYS, 128), BF16),
                        pltpu.VMEM((PEER_HEADS, nslab, PEER_KEYS, 128), BF16)],
        compiler_params=_params("parallel", "arbitrary"),
        name="peer_dense",
    )(hnt, wts["u"], wts["v_t"], n1, q1, r2, q2, h)


def _rope_tables(pos, reps):
    half = A_HEAD_DIM // 2
    inv_freq = ROPE_THETA ** (-jnp.arange(half, dtype=F32) / half)
    ang = pos.astype(F32)[:, None] * inv_freq[None, :]
    cos = jnp.cos(ang)
    sin = jnp.sin(ang)
    cos_t = jnp.tile(jnp.concatenate([cos, cos], axis=-1), (reps, A_HEADS))
    sin_t = jnp.tile(jnp.concatenate([-sin, sin], axis=-1), (reps, A_HEADS))
    return cos_t, sin_t


def _prep_weights(norm_mix_w, w_in, q_norm_w, k_norm_w, a_log, dt_bias, w_out, norm_ffn_w, w_pq, sub_keys, expert_u, expert_v):
    na = 3 * A_WIDTH
    hb = B_HEADS * B_DIM
    wgate = w_in[:, na + B_QKV + hb:]
    wg = jnp.zeros((D_MODEL, B_HEADS, B_DIM), F32)
    wg = wg.at[:, :, 0].set(wgate[:, :B_HEADS]).at[:, :, 1].set(wgate[:, B_HEADS:])
    lane_head = jnp.arange(A_WIDTH) // A_HEAD_DIM
    put1 = lambda vec: jnp.zeros((B_HEADS, B_DIM), F32).at[:, 1].set(vec).reshape(1, hb)
    return {
        "norm_mix": norm_mix_w.reshape(1, D_MODEL),
        "wa": w_in[:, :na].astype(BF16),
        "wb": w_in[:, na:na + B_QKV].astype(BF16),
        "wz": w_in[:, na + B_QKV:na + B_QKV + hb].astype(BF16),
        "wg": wg.reshape(D_MODEL, hb).astype(BF16),
        "qnw": jnp.tile(q_norm_w, A_HEADS).reshape(1, A_WIDTH),
        "knw": jnp.tile(k_norm_w, A_HEADS).reshape(1, A_WIDTH),
        "headmat": (lane_head[:, None] == lane_head[None, :]).astype(BF16),
        "alog": put1(a_log),
        "dtb": put1(dt_bias),
        "wo_a": w_out[:A_WIDTH].astype(BF16),
        "wo_b": w_out[A_WIDTH:].astype(BF16),
        "norm_ffn": norm_ffn_w.reshape(1, D_MODEL),
        "wpq_t": w_pq.T.astype(BF16),
        "sub_keys": sub_keys.astype(BF16),
        "u": expert_u.astype(BF16),
        "v_t": expert_v.T.astype(BF16),
    }


def _layer(x, pos_cos, pos_sin, pos_blocks, wts, conv_w, dnw, mixer_a, hist, s0, chunk, tm, tt_sel, tt_dense):
    b, t, _ = x.shape
    xf = x.reshape(b * t, D_MODEL)
    q, k, v, raw, z, gb = _proj_in(xf, pos_cos, pos_sin, pos_blocks, wts, tm)
    a_out, k_buf, v_buf = mixer_a(q.reshape(b, t, A_WIDTH), k.reshape(b, t, A_WIDTH), v.reshape(b, t, A_WIDTH))
    raw = raw.reshape(b, t, B_QKV)
    conv_state = raw[:, t - (CONV_WIDTH - 1):]
    pad = (-t) % chunk
    pad3 = lambda arr: jnp.pad(arr.reshape(b, t, -1), ((0, 0), (0, pad), (0, 0)))
    b_out, s_fin = _delta(pad3(raw), hist, conv_w, pad3(gb), pad3(z), dnw, s0, chunk, min(512, t + pad))
    b_out = b_out[:, :t].reshape(b * t, B_HEADS * B_DIM)
    h, hnt = _proj_out(xf, a_out.reshape(b * t, A_WIDTH), b_out, wts, tm)
    sel = _peer_select(hnt, wts, tt_sel)
    y = _peer_dense(hnt, h, sel, wts, tt_dense)
    return y.reshape(b, t, D_MODEL), k_buf, v_buf, conv_state, s_fin


def kernel(x_prompt, x_sample, cache_k_win, cache_v_win, state_conv, state_delta, norm_mix_w, w_in, q_norm_w, k_norm_w, conv_w, a_log, dt_bias, delta_norm_w, w_out, norm_ffn_w, w_pq, sub_keys, expert_u, expert_v):
    assert w_in.shape[0] == 1, "one layer"
    bp, sp, _ = x_prompt.shape
    bs, ss, _ = x_sample.shape
    wts = _prep_weights(norm_mix_w[0], w_in[0], q_norm_w[0], k_norm_w[0], a_log[0], dt_bias[0], w_out[0], norm_ffn_w[0],
                        w_pq[0], sub_keys[0], expert_u[0], expert_v[0])
    cw = conv_w[0]
    dnw = delta_norm_w[0].reshape(1, B_DIM)
    lane_head = jnp.arange(A_WIDTH) // A_HEAD_DIM
    head_sum = (lane_head[:, None] == jnp.arange(128)[None, :]).astype(F32)
    head_expand = head_sum.T

    tm_p = 512
    cos_p, sin_p = _rope_tables(jnp.arange(sp), 1)

    def mixer_prompt(q, k, v):
        return _attn_prompt(q, k, v), k, v

    yp, kp, vp, cp, dp = _layer(
        x_prompt, cos_p, sin_p, sp // tm_p, wts, cw, dnw, mixer_prompt,
        jnp.zeros((bp, 8, B_QKV), F32), jnp.zeros((bp, B_HEADS, B_DIM, B_DIM), F32),
        chunk=64, tm=tm_p, tt_sel=256, tt_dense=1024)

    n_s = bs * ss
    cos_s, sin_s = _rope_tables(PAST_LEN + jnp.arange(ss), bs)

    def mixer_sample(q, k, v):
        kc = cache_k_win[0].reshape(bs, -1, A_WIDTH)
        vc = cache_v_win[0].reshape(bs, -1, A_WIDTH)
        return _attn_sample(q, k, v, kc, vc, head_sum, head_expand)

    hist_s = jnp.pad(state_conv[0], ((0, 0), (8 - (CONV_WIDTH - 1), 0), (0, 0)))
    ys, ks, vs, cs, ds = _layer(
        x_sample, cos_s, sin_s, 1, wts, cw, dnw, mixer_sample, hist_s, state_delta[0],
        chunk=8, tm=n_s, tt_sel=n_s, tt_dense=n_s)

    win_p = min(BAND * max(DILATIONS), sp)
    shape_kv = lambda arr, b: arr.reshape(1, b, -1, A_HEADS, A_HEAD_DIM)
    return (yp, ys,
            shape_kv(kp[:, sp - win_p:], bp), shape_kv(vp[:, sp - win_p:], bp), cp[None], dp[None],
            shape_kv(ks, bs), shape_kv(vs, bs), cs[None], ds[None])
```

```python
import functools
import math

import jax
import jax.numpy as jnp
from jax import lax
from jax.experimental import pallas as pl
from jax.experimental.pallas import tpu as pltpu

F32 = jnp.float32
BF16 = jnp.bfloat16
U32 = jnp.uint32
HIGHEST = lax.Precision.HIGHEST

D_MODEL = 1024
A_HEADS = 8
A_HEAD_DIM = 64
A_WIDTH = A_HEADS * A_HEAD_DIM
DILATIONS = (1, 4, 16)
BAND = 128
ROPE_THETA = 10000.0
PAST_LEN = 8192
B_HEADS = 4
B_DIM = 128
B_QKV = 3 * B_HEADS * B_DIM
CONV_WIDTH = 4
PEER_HEADS = 8
PEER_KEYS = 128
PEER_TOPK = 16
EPS = 1e-6
NEG_INF = float("-inf")
VMEM_LIMIT = 56 * 1024 * 1024


def _params(*sem, flags=None):
    return pltpu.CompilerParams(dimension_semantics=sem, vmem_limit_bytes=VMEM_LIMIT, flags=flags)


def _dot(a, b, **kw):
    return jnp.dot(a, b, preferred_element_type=F32, **kw)


def _dot_nt(a, b, **kw):
    return lax.dot_general(a, b, (((1,), (1,)), ((), ())), preferred_element_type=F32, **kw)


def _dot_tn(a, b, **kw):
    return lax.dot_general(a, b, (((0,), (0,)), ((), ())), preferred_element_type=F32, **kw)


def _dot_x01(a, b01):
    b = b01.astype(BF16)
    a1 = a.astype(BF16)
    r1 = a - a1.astype(F32)
    a2 = r1.astype(BF16)
    a3 = (r1 - a2.astype(F32)).astype(BF16)
    return _dot(a1, b) + (_dot(a2, b) + _dot(a3, b))


def _sigmoid(x):
    return 1.0 / (1.0 + jnp.exp(-x))


def _proj_in_kernel(x_ref, nw_ref, wa_ref, wb_ref, wz_ref, wg_ref, qnw_ref, knw_ref, cos_ref, sin_ref,
                    hm_ref, alog_ref, dtb_ref, q_ref, k_ref, v_ref, raw_ref, z_ref, gb_ref):
    x = x_ref[...]
    ms = jnp.mean(x * x, axis=-1, keepdims=True)
    xn = (x * lax.rsqrt(ms + EPS) * nw_ref[...]).astype(BF16)
    a = _dot(xn, wa_ref[...])
    hm = hm_ref[...]
    lane = lax.broadcasted_iota(jnp.int32, (1, A_WIDTH), 1)
    first_half = (lane & (A_HEAD_DIM // 2)) == 0
    cos = cos_ref[...]
    sin = sin_ref[...]

    def norm_rope(t, w):
        t2 = t * t
        hi = t2.astype(BF16)
        lo = (t2 - hi.astype(F32)).astype(BF16)
        ss = _dot(hi, hm) + _dot(lo, hm)
        tn = t * lax.rsqrt(ss * (1.0 / A_HEAD_DIM) + EPS) * w
        partner = jnp.where(first_half, pltpu.roll(tn, A_WIDTH - A_HEAD_DIM // 2, 1), pltpu.roll(tn, A_HEAD_DIM // 2, 1))
        return tn * cos + partner * sin

    q_ref[...] = norm_rope(a[:, :A_WIDTH], qnw_ref[...])
    k_ref[...] = norm_rope(a[:, A_WIDTH:2 * A_WIDTH], knw_ref[...])
    v_ref[...] = a[:, 2 * A_WIDTH:]
    raw_ref[...] = _dot(xn, wb_ref[...])
    z_ref[...] = _dot(xn, wz_ref[...])
    zg = _dot(xn, wg_ref[...])
    beta = _sigmoid(zg)
    t = zg + dtb_ref[...]
    softplus = jnp.maximum(t, 0.0) + jnp.log1p(jnp.exp(-jnp.abs(t)))
    g = -jnp.exp(alog_ref[...]) * softplus
    glane = lax.broadcasted_iota(jnp.int32, (1, B_HEADS * B_DIM), 1) & (B_DIM - 1)
    gb_ref[...] = jnp.where(glane == 0, beta, jnp.where(glane == 1, g, 0.0))


def _proj_in(x, pos_cos, pos_sin, pos_blocks, wts, tm):
    t = x.shape[0]
    n = t // tm
    row = lambda i: (i, 0)
    const = lambda i: (0, 0)
    posmap = lambda i: (i % pos_blocks, 0)
    full = lambda a: pl.BlockSpec(a.shape, const)
    outs = [(A_WIDTH, "q"), (A_WIDTH, "k"), (A_WIDTH, "v"), (B_QKV, "raw"), (B_HEADS * B_DIM, "z"), (B_HEADS * B_DIM, "gb")]
    return pl.pallas_call(
        _proj_in_kernel,
        grid=(n,),
        in_specs=[pl.BlockSpec((tm, D_MODEL), row), full(wts["norm_mix"]), full(wts["wa"]), full(wts["wb"]), full(wts["wz"]),
                  full(wts["wg"]), full(wts["qnw"]), full(wts["knw"]),
                  pl.BlockSpec((tm, A_WIDTH), posmap), pl.BlockSpec((tm, A_WIDTH), posmap),
                  full(wts["headmat"]), full(wts["alog"]), full(wts["dtb"])],
        out_specs=[pl.BlockSpec((tm, w), row) for w, _ in outs],
        out_shape=[jax.ShapeDtypeStruct((t, w), F32) for w, _ in outs],
        compiler_params=_params("parallel"),
        name="proj_in",
    )(x, wts["norm_mix"], wts["wa"], wts["wb"], wts["wz"], wts["wg"], wts["qnw"], wts["knw"], pos_cos, pos_sin,
      wts["headmat"], wts["alog"], wts["dtb"])


def _attn_prompt_kernel(q_ref, k_ref, v_ref, o_ref, ob_ref, lb_ref, *, seq):
    lane = lax.broadcasted_iota(jnp.int32, (BAND, 2 * A_HEAD_DIM), 1)
    head0 = lane < A_HEAD_DIM
    qi = lax.broadcasted_iota(jnp.int32, (BAND, 2 * BAND), 0)
    kj = lax.broadcasted_iota(jnp.int32, (BAND, 2 * BAND), 1)
    dist = BAND + qi - kj
    in_band = (dist >= 0) & (dist <= BAND)
    causal = in_band[:, BAND:]
    scale = 1.0 / math.sqrt(A_HEAD_DIM)

    for br, dil in enumerate(DILATIONS):
        nblk = seq // (BAND * dil)
        shift = dil.bit_length() - 1

        def rows(ref, start):
            if dil == 1:
                return ref[pl.ds(start, BAND), :]
            return ref[pl.ds(start, BAND, stride=dil), :]

        def unit(u, carry):
            r = u & (dil - 1)
            n = u >> shift
            start = r + n * (BAND * dil)
            qb = rows(q_ref, start)
            k_own = rows(k_ref, start).astype(BF16)
            v_own = rows(v_ref, start).astype(BF16)
            if nblk > 1:
                pstart = jnp.maximum(start - BAND * dil, r)
                kk = jnp.concatenate([rows(k_ref, pstart).astype(BF16), k_own], axis=0)
                vv = jnp.concatenate([rows(v_ref, pstart).astype(BF16), v_own], axis=0)
                mask = in_band & (kj >= jnp.where(n > 0, 0, BAND))
            else:
                kk, vv, mask = k_own, v_own, causal
            outs = []
            lses = []
            for hmask in (head0, ~head0):
                qh = jnp.where(hmask, qb, 0.0).astype(BF16)
                s = _dot_nt(qh, kk) * scale
                s = jnp.where(mask, s, NEG_INF)
                mx = jnp.max(s, axis=-1, keepdims=True)
                p = jnp.exp(s - mx)
                l = jnp.sum(p, axis=-1, keepdims=True)
                outs.append(_dot(p.astype(BF16), vv) / l)
                lses.append(mx + jnp.log(l))
            out = jnp.where(head0, outs[0], outs[1])
            lse = jnp.where(head0, lses[0], lses[1])
            if dil == 1:
                ob_ref[br, pl.ds(start, BAND), :] = out
                lb_ref[br, pl.ds(start, BAND), :] = lse
            else:
                ob_ref[br, pl.ds(start, BAND, stride=dil), :] = out
                lb_ref[br, pl.ds(start, BAND, stride=dil), :] = lse
            return carry

        lax.fori_loop(0, seq // BAND, unit, 0, unroll=8 if dil < 16 else 4)

    def merge(c, carry):
        rs = pl.ds(pl.multiple_of(c * 256, 256), 256)
        l0, l1, l2 = lb_ref[0, rs, :], lb_ref[1, rs, :], lb_ref[2, rs, :]
        m = jnp.maximum(jnp.maximum(l0, l1), l2)
        w0, w1, w2 = jnp.exp(l0 - m), jnp.exp(l1 - m), jnp.exp(l2 - m)
        o_ref[rs, :] = (w0 * ob_ref[0, rs, :] + w1 * ob_ref[1, rs, :] + w2 * ob_ref[2, rs, :]) / (w0 + w1 + w2)
        return carry

    lax.fori_loop(0, seq // 256, merge, 0)


def _attn_prompt(q, k, v):
    b, s, _ = q.shape
    spec = pl.BlockSpec((None, s, 2 * A_HEAD_DIM), lambda i, j: (i, 0, j))
    return pl.pallas_call(
        functools.partial(_attn_prompt_kernel, seq=s),
        grid=(b, A_HEADS // 2),
        in_specs=[spec, spec, spec],
        out_specs=spec,
        out_shape=jax.ShapeDtypeStruct((b, s, A_WIDTH), F32),
        scratch_shapes=[pltpu.VMEM((len(DILATIONS), s, 2 * A_HEAD_DIM), F32),
                        pltpu.VMEM((len(DILATIONS), s, 2 * A_HEAD_DIM), F32)],
        compiler_params=_params("parallel", "parallel"),
        name="attn_prompt",
    )(q, k, v)


def _attn_sample_kernel(q_ref, kn_ref, vn_ref, kc_ref, vc_ref, hs_ref, he_ref, o_ref, ko_ref, vo_ref, kbuf, vbuf,
                        *, win, new):
    nlb = A_WIDTH // 128
    for buf, cache, fresh, shifted in ((kbuf, kc_ref, kn_ref, ko_ref), (vbuf, vc_ref, vn_ref, vo_ref)):
        for j in range(nlb):
            ls = slice(j * 128, (j + 1) * 128)
            buf[j, 0:win, :] = cache[:, ls]
            buf[j, win:win + new, :] = fresh[:, ls]
            shifted[:, ls] = buf[j, new:new + win, :]

    def gather(buf, start, size, stride):
        if stride == 1:
            return jnp.concatenate([buf[j, pl.ds(start, size), :] for j in range(nlb)], axis=1)
        return jnp.concatenate([buf[j, pl.ds(start, size, stride=stride), :] for j in range(nlb)], axis=1)

    hs = hs_ref[...]
    he = he_ref[...]
    scale = 1.0 / math.sqrt(A_HEAD_DIM)
    for l in range(new):
        ql = q_ref[l:l + 1, :]
        outs, lses = [], []
        for dil in DILATIONS:
            start = win + l - BAND * dil
            ks = gather(kbuf, start, BAND, dil)
            vs = gather(vbuf, start, BAND, dil)
            kself = gather(kbuf, win + l, 1, 1)
            vself = gather(vbuf, win + l, 1, 1)
            s = _dot_x01(ks * ql, hs) * scale
            sself = _dot_x01(kself * ql, hs) * scale
            mx = jnp.maximum(jnp.max(s, axis=0, keepdims=True), sself)
            p = jnp.exp(s - mx)
            pself = jnp.exp(sself - mx)
            lsum = jnp.sum(p, axis=0, keepdims=True) + pself
            pe = _dot_x01(p, he)
            o = jnp.sum(pe * vs, axis=0, keepdims=True) + _dot_x01(pself, he) * vself
            outs.append(o / _dot_x01(lsum, he))
            lses.append(_dot_x01(mx + jnp.log(lsum), he))
        m = jnp.maximum(jnp.maximum(lses[0], lses[1]), lses[2])
        w = [jnp.exp(ls - m) for ls in lses]
        o_ref[l:l + 1, :] = (w[0] * outs[0] + w[1] * outs[1] + w[2] * outs[2]) / (w[0] + w[1] + w[2])


def _attn_sample(q, kn, vn, kc, vc, hs, he):
    b, new, _ = q.shape
    win = kc.shape[1]
    assert win >= BAND * max(DILATIONS)
    small = pl.BlockSpec((None, new, A_WIDTH), lambda i: (i, 0, 0))
    big = pl.BlockSpec((None, win, A_WIDTH), lambda i: (i, 0, 0))
    const = lambda a: pl.BlockSpec(a.shape, lambda i: (0, 0))
    return pl.pallas_call(
        functools.partial(_attn_sample_kernel, win=win, new=new),
        grid=(b,),
        in_specs=[small, small, small, big, big, const(hs), const(he)],
        out_specs=[small, big, big],
        out_shape=[jax.ShapeDtypeStruct((b, new, A_WIDTH), F32), jax.ShapeDtypeStruct((b, win, A_WIDTH), F32),
                   jax.ShapeDtypeStruct((b, win, A_WIDTH), F32)],
        scratch_shapes=[pltpu.VMEM((A_WIDTH // 128, win + 8, 128), F32), pltpu.VMEM((A_WIDTH // 128, win + 8, 128), F32)],
        compiler_params=_params("parallel"),
        name="attn_sample",
    )(q, kn, vn, kc, vc, hs, he)


def _dot3(a, b):
    ah = a.astype(BF16)
    al = (a - ah.astype(F32)).astype(BF16)
    bh = b.astype(BF16)
    bl = (b - bh.astype(F32)).astype(BF16)
    return _dot(ah, bh) + (_dot(ah, bl) + _dot(al, bh))


def _dot_01(a01, b):
    a = a01.astype(BF16)
    b1 = b.astype(BF16)
    r1 = b - b1.astype(F32)
    b2 = r1.astype(BF16)
    b3 = (r1 - b2.astype(F32)).astype(BF16)
    return _dot(a, b1) + (_dot(a, b2) + _dot(a, b3))


def _delta_kernel(raw_ref, hist_ref, cw_ref, gb_ref, z_ref, dnw_ref, s0_ref, o_ref, sf_ref, ext, s_scr, *, blk, chunk):
    c = chunk
    nh = B_HEADS
    r = nh * c
    ncol = 3 * nh
    shift = c.bit_length() - 1
    j = pl.program_id(1)

    @pl.when(j == 0)
    def _():
        for col in range(ncol):
            ext[col, 0:8, :] = hist_ref[:, col * B_DIM:(col + 1) * B_DIM]
        for k in range(nh):
            s_scr[k * B_DIM:(k + 1) * B_DIM, :] = s0_ref[k]

    for col in range(ncol):
        ext[col, 8:8 + blk, :] = raw_ref[:, col * B_DIM:(col + 1) * B_DIM]

    ii = lax.broadcasted_iota(jnp.int32, (r, r), 0)
    jj = lax.broadcasted_iota(jnp.int32, (r, r), 1)
    same = (ii >> shift) == (jj >> shift)
    causal = same & (ii >= jj)
    strict = same & (ii > jj)
    upper = same & (ii <= jj)
    eye = (ii == jj).astype(F32)
    tri = causal.astype(F32)
    ones = jnp.ones((r, r), F32)
    row_head = lax.broadcasted_iota(jnp.int32, (r, nh * B_DIM), 0) >> shift
    lane_head = lax.broadcasted_iota(jnp.int32, (r, nh * B_DIM), 1) >> (B_DIM.bit_length() - 1)
    own_head = row_head == lane_head
    dnw = dnw_ref[...]

    def conv_silu(col, base):
        y = cw_ref[0:1, col * B_DIM:(col + 1) * B_DIM] * ext[col, pl.ds(base + 5, c), :]
        for t in range(1, CONV_WIDTH):
            y = y + cw_ref[t:t + 1, col * B_DIM:(col + 1) * B_DIM] * ext[col, pl.ds(base + 5 + t, c), :]
        return y * _sigmoid(y)

    def l2n(t):
        return t * lax.rsqrt(jnp.sum(t * t, axis=-1, keepdims=True) + EPS)

    def stack(parts):
        return jnp.concatenate(parts, axis=0)

    def block_diag(x):
        return jnp.where(own_head, jnp.concatenate([x] * nh, axis=1), 0.0)

    def body(n, carry):
        base = pl.multiple_of(n * c, c)
        q = stack([l2n(conv_silu(k, base)) for k in range(nh)]) * (B_DIM ** -0.5)
        kk = stack([l2n(conv_silu(nh + k, base)) for k in range(nh)])
        v = stack([conv_silu(2 * nh + k, base) for k in range(nh)])
        gbs = [gb_ref[pl.ds(base, c), k * B_DIM:(k + 1) * B_DIM] for k in range(nh)]
        beta = stack([t[:, 0:1] for t in gbs])
        g = stack([t[:, 1:2] for t in gbs])
        gsq = jnp.broadcast_to(g, (r, r))
        gcol = _dot_01(tri, gsq)
        grow = _dot_01(ones, jnp.where(upper, gsq, 0.0))
        gcum = gcol[:, 0:1]
        glast = stack([jnp.broadcast_to(gcol[(k + 1) * c - 1:(k + 1) * c, 0:1], (c, 1)) for k in range(nh)])
        decay = jnp.exp(jnp.where(causal, gcol - grow, NEG_INF))
        kb = kk * beta
        kbh, kh, qh = kb.astype(BF16), kk.astype(BF16), q.astype(BF16)
        lower = jnp.where(strict, _dot_nt(kbh, kh) * decay, 0.0)
        tinv = eye - lower
        lp = lower
        span = 2
        while span < c:
            lp = _dot3(lp, lp)
            tinv = tinv + _dot3(tinv, lp)
            span *= 2
        eg = jnp.exp(gcum)
        u = _dot3(tinv, v * beta)
        w = _dot3(tinv, kb * eg)
        intra = jnp.where(causal, _dot_nt(qh, kh) * decay, 0.0)
        state = s_scr[...]
        sh = state.astype(BF16)
        v_new = u - _dot(block_diag(w).astype(BF16), sh)
        vh = v_new.astype(BF16)
        out = _dot(block_diag(q * eg).astype(BF16), sh) + _dot(intra.astype(BF16), vh)
        kdec = kk * jnp.exp(glast - gcum)
        cd = stack([jnp.broadcast_to(jnp.exp(gcol[(k + 1) * c - 1:(k + 1) * c, 0:1]), (B_DIM, 1)) for k in range(nh)])
        s_scr[...] = state * cd + _dot_tn(block_diag(kdec).astype(BF16), vh)
        on = out * lax.rsqrt(jnp.mean(out * out, axis=-1, keepdims=True) + EPS) * dnw
        for k in range(nh):
            zc = z_ref[pl.ds(base, c), k * B_DIM:(k + 1) * B_DIM]
            o_ref[pl.ds(base, c), k * B_DIM:(k + 1) * B_DIM] = on[k * c:(k + 1) * c] * (zc * _sigmoid(zc))
        return carry

    lax.fori_loop(0, blk // c, body, 0, unroll=2 if blk // c >= 2 else 1)
    for col in range(ncol):
        ext[col, 0:8, :] = ext[col, blk:blk + 8, :]

    @pl.when(j == pl.num_programs(1) - 1)
    def _():
        for k in range(nh):
            sf_ref[k] = s_scr[k * B_DIM:(k + 1) * B_DIM, :]


def _delta(raw, hist, conv_w, gb, z, dnw, s0, chunk, blk):
    b, seq, _ = raw.shape
    hb = B_HEADS * B_DIM
    tok = lambda w: pl.BlockSpec((None, blk, w), lambda i, j: (i, j, 0))
    st = pl.BlockSpec((None, B_HEADS, B_DIM, B_DIM), lambda i, j: (i, 0, 0, 0))
    return pl.pallas_call(
        functools.partial(_delta_kernel, blk=blk, chunk=chunk),
        grid=(b, seq // blk),
        in_specs=[tok(B_QKV), pl.BlockSpec((None, 8, B_QKV), lambda i, j: (i, 0, 0)),
                  pl.BlockSpec((CONV_WIDTH, B_QKV), lambda i, j: (0, 0)), tok(hb), tok(hb),
                  pl.BlockSpec((1, B_DIM), lambda i, j: (0, 0)), st],
        out_specs=[tok(hb), st],
        out_shape=[jax.ShapeDtypeStruct((b, seq, hb), F32), jax.ShapeDtypeStruct((b, B_HEADS, B_DIM, B_DIM), F32)],
        scratch_shapes=[pltpu.VMEM((3 * B_HEADS, blk + 8, B_DIM), F32), pltpu.VMEM((B_HEADS * B_DIM, B_DIM), F32)],
        compiler_params=_params("parallel", "arbitrary"),
        name="delta",
    )(raw, hist, conv_w, gb, z, dnw, s0)


def _proj_out_kernel(x_ref, a_ref, b_ref, wa_ref, wb_ref, nw_ref, h_ref, hnt_ref):
    h = x_ref[...] + _dot(a_ref[...].astype(BF16), wa_ref[...]) + _dot(b_ref[...].astype(BF16), wb_ref[...])
    h_ref[...] = h
    hn = h * lax.rsqrt(jnp.mean(h * h, axis=-1, keepdims=True) + EPS) * nw_ref[...]
    hnt = hn.T.astype(BF16)
    for c in range(hnt_ref.shape[0]):
        hnt_ref[c] = hnt[:, c * 128:(c + 1) * 128]


def _proj_out(x, a, b, wts, tm):
    t = x.shape[0]
    row = lambda i: (i, 0)
    full = lambda arr: pl.BlockSpec(arr.shape, lambda i: (0, 0))
    return pl.pallas_call(
        _proj_out_kernel,
        grid=(t // tm,),
        in_specs=[pl.BlockSpec((tm, D_MODEL), row), pl.BlockSpec((tm, A_WIDTH), row), pl.BlockSpec((tm, A_WIDTH), row),
                  full(wts["wo_a"]), full(wts["wo_b"]), full(wts["norm_ffn"])],
        out_specs=[pl.BlockSpec((tm, D_MODEL), row), pl.BlockSpec((tm // 128, D_MODEL, 128), lambda i: (i, 0, 0))],
        out_shape=[jax.ShapeDtypeStruct((t, D_MODEL), F32), jax.ShapeDtypeStruct((t // 128, D_MODEL, 128), BF16)],
        compiler_params=_params("parallel"),
        name="proj_out",
    )(x, a, b, wts["wo_a"], wts["wo_b"], wts["norm_ffn"])


def _top16_ranked(x):
    t = x.shape[1]
    slot = lax.broadcasted_iota(jnp.int32, (PEER_TOPK, t), 0).astype(F32)
    vals = jnp.full((PEER_TOPK, t), NEG_INF, F32)
    rank = jnp.full(x.shape, float(PEER_TOPK), F32)
    filled = jnp.zeros((1, t), F32)
    for _ in range(PEER_TOPK):
        m = jnp.max(x, axis=0, keepdims=True)
        eq = x == m
        cnt = jnp.sum(jnp.where(eq, 1.0, 0.0), axis=0, keepdims=True)
        vals = jnp.where((slot >= filled) & (slot < filled + cnt), m, vals)
        rank = jnp.where(eq, jnp.minimum(filled, float(PEER_TOPK)), rank)
        filled = filled + cnt
        x = jnp.where(eq, NEG_INF, x)
    return vals, rank


def _kth_largest(x, k):
    t = x.shape[1]
    kth = jnp.full((1, t), NEG_INF, F32)
    filled = jnp.zeros((1, t), F32)
    for _ in range(k):
        m = jnp.max(x, axis=0, keepdims=True)
        eq = x == m
        cnt = jnp.sum(jnp.where(eq, 1.0, 0.0), axis=0, keepdims=True)
        kth = jnp.where((filled < k) & (filled + cnt >= k), m, kth)
        filled = filled + cnt
        x = jnp.where(eq, NEG_INF, x)
    return kth


def _peer_select_kernel(hnt_ref, wpq_ref, sk_ref, n1_ref, q1_ref, r2_ref, q2_ref, sub_scr, top_scr, rank_scr):
    nslab = hnt_ref.shape[0]
    hnt = jnp.concatenate([hnt_ref[c] for c in range(nslab)], axis=1)

    def half(hc, carry):
        row0 = pl.multiple_of(hc * PEER_KEYS, PEER_KEYS)
        qt = _dot(wpq_ref[pl.ds(row0, PEER_KEYS), :], hnt)
        sub = _dot(sk_ref[hc & 1], qt.astype(BF16))
        sub_scr[hc] = sub
        top_scr[hc], rank_scr[hc] = _top16_ranked(sub)
        return carry

    lax.fori_loop(0, 2 * PEER_HEADS, half, 0, unroll=2)

    def slabs(ref, h, x):
        for c in range(nslab):
            ref[h, c] = x[:, c * 128:(c + 1) * 128].astype(ref.dtype)

    def head(h, carry):
        a = top_scr[2 * h]
        b = top_scr[2 * h + 1]
        pieces = [a[0:1] + b] + [a[i:i + 1] + b[0:8] for i in range(1, 8)] + [a[8:16] + b[0:1]]
        cand = jnp.concatenate(pieces, axis=0)
        tau = _kth_largest(cand, PEER_TOPK)
        top = a[0:1] + b[0:1]
        zsum = jnp.sum(jnp.where(cand >= tau, jnp.exp(cand - top), 0.0), axis=0, keepdims=True)
        cnt = jnp.zeros_like(a)
        for j in range(PEER_TOPK):
            cnt = cnt + jnp.where(a + b[j:j + 1] >= tau, 1.0, 0.0)
        rank1 = rank_scr[2 * h]
        n1 = jnp.zeros_like(rank1)
        for i in range(PEER_TOPK):
            n1 = jnp.where(rank1 == float(i), cnt[i:i + 1], n1)
        slabs(n1_ref, h, n1)
        slabs(r2_ref, h, rank_scr[2 * h + 1])
        slabs(q1_ref, h, jnp.exp(sub_scr[2 * h] - a[0:1]) / zsum)
        slabs(q2_ref, h, jnp.exp(sub_scr[2 * h + 1] - b[0:1]))
        return carry

    lax.fori_loop(0, PEER_HEADS, head, 0, unroll=2)


def _peer_select(hnt, wts, tt):
    t = hnt.shape[0] * 128
    out = pl.BlockSpec((PEER_HEADS, tt // 128, PEER_KEYS, 128), lambda i: (0, i, 0, 0))
    shape = lambda dt: jax.ShapeDtypeStruct((PEER_HEADS, t // 128, PEER_KEYS, 128), dt)
    return pl.pallas_call(
        _peer_select_kernel,
        grid=(t // tt,),
        in_specs=[pl.BlockSpec((tt // 128, D_MODEL, 128), lambda i: (i, 0, 0)),
                  pl.BlockSpec(wts["wpq_t"].shape, lambda i: (0, 0)),
                  pl.BlockSpec(wts["sub_keys"].shape, lambda i: (0, 0, 0))],
        out_specs=[out, out, out, out],
        out_shape=[shape(F32), shape(F32), shape(BF16), shape(BF16)],
        scratch_shapes=[pltpu.VMEM((2 * PEER_HEADS, PEER_KEYS, tt), F32), pltpu.VMEM((2 * PEER_HEADS, PEER_TOPK, tt), F32),
                        pltpu.VMEM((2 * PEER_HEADS, PEER_KEYS, tt), F32)],
        compiler_params=_params("parallel"),
        name="peer_select",
    )(hnt, wts["wpq_t"], wts["sub_keys"])


A_PER_TILE = 8


def _rows_bf16(row):
    return jnp.broadcast_to(row, (PEER_KEYS, 128)).astype(BF16)


def _peer_dense_kernel(hnt_ref, u_ref, vt_ref, n1_ref, q1_ref, r2_ref, q2_ref, h_ref, y_ref, acc_ref, pre_ref, w_ref,
                       r2_scr, q2_scr, *, tt):
    j = pl.program_id(1)
    nslab = tt // 128

    @pl.when(j == 0)
    def _():
        acc_ref[...] = jnp.zeros_like(acc_ref)
        r2_scr[...] = r2_ref[...]
        q2_scr[...] = q2_ref[...]

    per_sub = min(nslab, 2)
    nsub = nslab // per_sub
    zero = jnp.zeros((PEER_KEYS, 128), BF16)
    group = 2

    def mm_u(s):
        hn = jnp.concatenate([hnt_ref[s * per_sub + c] for c in range(per_sub)], axis=1)
        pre = _dot(u_ref[...], hn).astype(BF16)
        for c in range(per_sub):
            pre_ref[s * per_sub + c] = pre[:, c * 128:(c + 1) * 128]

    def build(tc, carry):
        for ia0 in range(0, A_PER_TILE, group):
            gates = [zero] * group
            for h in range(PEER_HEADS):
                r2 = r2_scr[h, tc]
                q2 = q2_scr[h, tc]
                for k in range(group):
                    n1 = _rows_bf16(n1_ref[h, tc, ia0 + k:ia0 + k + 1, :])
                    q1 = _rows_bf16(q1_ref[h, tc, ia0 + k:ia0 + k + 1, :])
                    gates[k] = gates[k] + jnp.where(r2 < n1, q2, zero) * q1
            for k in range(group):
                rows = slice((ia0 + k) * PEER_KEYS, (ia0 + k + 1) * PEER_KEYS)
                p = pre_ref[tc, rows, :]
                act = (0.5 * p) * (1.0 + lax.erf(p * (1.0 / math.sqrt(2.0))))
                w_ref[tc, rows, :] = gates[k] * act
        return carry

    def mm_v(s):
        w = jnp.concatenate([w_ref[s * per_sub + c] for c in range(per_sub)], axis=1)
        out = _dot(vt_ref[...], w)
        for c in range(per_sub):
            acc_ref[s * per_sub + c] += out[:, c * 128:(c + 1) * 128]

    for s in range(nsub):
        mm_u(s)
    lax.fori_loop(0, nslab, build, 0)
    for s in range(nsub):
        mm_v(s)

    @pl.when(j == pl.num_programs(1) - 1)
    def _():
        for c in range(nslab):
            y_ref[c * 128:(c + 1) * 128, :] = h_ref[c * 128:(c + 1) * 128, :] + acc_ref[c].T


def _peer_dense(hnt, h, sel, wts, tt):
    t = hnt.shape[0] * 128
    n1, q1, r2, q2 = sel
    et = A_PER_TILE * PEER_KEYS
    nslab = tt // 128
    a_spec = pl.BlockSpec((PEER_HEADS, nslab, A_PER_TILE, 128), lambda i, j: (0, i, j, 0))
    b_spec = pl.BlockSpec((PEER_HEADS, nslab, PEER_KEYS, 128), lambda i, j: (0, i, 0, 0))
    return pl.pallas_call(
        functools.partial(_peer_dense_kernel, tt=tt),
        grid=(t // tt, PEER_KEYS // A_PER_TILE),
        in_specs=[pl.BlockSpec((nslab, D_MODEL, 128), lambda i, j: (i, 0, 0)),
                  pl.BlockSpec((et, D_MODEL), lambda i, j: (j, 0)),
                  pl.BlockSpec((D_MODEL, et), lambda i, j: (0, j)),
                  a_spec, a_spec, b_spec, b_spec,
                  pl.BlockSpec((tt, D_MODEL), lambda i, j: (i, 0))],
        out_specs=pl.BlockSpec((tt, D_MODEL), lambda i, j: (i, 0)),
        out_shape=jax.ShapeDtypeStruct((t, D_MODEL), F32),
        scratch_shapes=[pltpu.VMEM((nslab, D_MODEL, 128), F32), pltpu.VMEM((nslab, et, 128), BF16),
                        pltpu.VMEM((nslab, et, 128), BF16),
                        pltpu.VMEM((PEER_HEADS, nslab, PEER_KEYS, 128), BF16),
                        pltpu.VMEM((PEER_HEADS, nslab, PEER_KEYS, 128), BF16)],
        compiler_params=_params("parallel", "arbitrary"),
        name="peer_dense",
    )(hnt, wts["u"], wts["v_t"], n1, q1, r2, q2, h)


def _rope_tables(pos, reps):
    half = A_HEAD_DIM // 2
    inv_freq = ROPE_THETA ** (-jnp.arange(half, dtype=F32) / half)
    ang = pos.astype(F32)[:, None] * inv_freq[None, :]
    cos = jnp.cos(ang)
    sin = jnp.sin(ang)
    cos_t = jnp.tile(jnp.concatenate([cos, cos], axis=-1), (reps, A_HEADS))
    sin_t = jnp.tile(jnp.concatenate([-sin, sin], axis=-1), (reps, A_HEADS))
    return cos_t, sin_t


def _prep_weights(norm_mix_w, w_in, q_norm_w, k_norm_w, a_log, dt_bias, w_out, norm_ffn_w, w_pq, sub_keys, expert_u, expert_v):
    na = 3 * A_WIDTH
    hb = B_HEADS * B_DIM
    wgate = w_in[:, na + B_QKV + hb:]
    wg = jnp.zeros((D_MODEL, B_HEADS, B_DIM), F32)
    wg = wg.at[:, :, 0].set(wgate[:, :B_HEADS]).at[:, :, 1].set(wgate[:, B_HEADS:])
    lane_head = jnp.arange(A_WIDTH) // A_HEAD_DIM
    put1 = lambda vec: jnp.zeros((B_HEADS, B_DIM), F32).at[:, 1].set(vec).reshape(1, hb)
    return {
        "norm_mix": norm_mix_w.reshape(1, D_MODEL),
        "wa": w_in[:, :na].astype(BF16),
        "wb": w_in[:, na:na + B_QKV].astype(BF16),
        "wz": w_in[:, na + B_QKV:na + B_QKV + hb].astype(BF16),
        "wg": wg.reshape(D_MODEL, hb).astype(BF16),
        "qnw": jnp.tile(q_norm_w, A_HEADS).reshape(1, A_WIDTH),
        "knw": jnp.tile(k_norm_w, A_HEADS).reshape(1, A_WIDTH),
        "headmat": (lane_head[:, None] == lane_head[None, :]).astype(BF16),
        "alog": put1(a_log),
        "dtb": put1(dt_bias),
        "wo_a": w_out[:A_WIDTH].astype(BF16),
        "wo_b": w_out[A_WIDTH:].astype(BF16),
        "norm_ffn": norm_ffn_w.reshape(1, D_MODEL),
        "wpq_t": w_pq.T.astype(BF16),
        "sub_keys": sub_keys.astype(BF16),
        "u": expert_u.astype(BF16),
        "v_t": expert_v.T.astype(BF16),
    }


def _layer(x, pos_cos, pos_sin, pos_blocks, wts, conv_w, dnw, mixer_a, hist, s0, chunk, tm, tt_sel, tt_dense):
    b, t, _ = x.shape
    xf = x.reshape(b * t, D_MODEL)
    q, k, v, raw, z, gb = _proj_in(xf, pos_cos, pos_sin, pos_blocks, wts, tm)
    a_out, k_buf, v_buf = mixer_a(q.reshape(b, t, A_WIDTH), k.reshape(b, t, A_WIDTH), v.reshape(b, t, A_WIDTH))
    raw = raw.reshape(b, t, B_QKV)
    conv_state = raw[:, t - (CONV_WIDTH - 1):]
    pad = (-t) % chunk
    pad3 = lambda arr: jnp.pad(arr.reshape(b, t, -1), ((0, 0), (0, pad), (0, 0)))
    b_out, s_fin = _delta(pad3(raw), hist, conv_w, pad3(gb), pad3(z), dnw, s0, chunk, min(512, t + pad))
    b_out = b_out[:, :t].reshape(b * t, B_HEADS * B_DIM)
    h, hnt = _proj_out(xf, a_out.reshape(b * t, A_WIDTH), b_out, wts, tm)
    sel = _peer_select(hnt, wts, tt_sel)
    y = _peer_dense(hnt, h, sel, wts, tt_dense)
    return y.reshape(b, t, D_MODEL), k_buf, v_buf, conv_state, s_fin


def kernel(x_prompt, x_sample, cache_k_win, cache_v_win, state_conv, state_delta, norm_mix_w, w_in, q_norm_w, k_norm_w, conv_w, a_log, dt_bias, delta_norm_w, w_out, norm_ffn_w, w_pq, sub_keys, expert_u, expert_v):
    assert w_in.shape[0] == 1, "one layer"
    bp, sp, _ = x_prompt.shape
    bs, ss, _ = x_sample.shape
    wts = _prep_weights(norm_mix_w[0], w_in[0], q_norm_w[0], k_norm_w[0], a_log[0], dt_bias[0], w_out[0], norm_ffn_w[0],
                        w_pq[0], sub_keys[0], expert_u[0], expert_v[0])
    cw = conv_w[0]
    dnw = delta_norm_w[0].reshape(1, B_DIM)
    lane_head = jnp.arange(A_WIDTH) // A_HEAD_DIM
    head_sum = (lane_head[:, None] == jnp.arange(128)[None, :]).astype(F32)
    head_expand = head_sum.T

    tm_p = 512
    cos_p, sin_p = _rope_tables(jnp.arange(sp), 1)

    def mixer_prompt(q, k, v):
        return _attn_prompt(q, k, v), k, v

    yp, kp, vp, cp, dp = _layer(
        x_prompt, cos_p, sin_p, sp // tm_p, wts, cw, dnw, mixer_prompt,
        jnp.zeros((bp, 8, B_QKV), F32), jnp.zeros((bp, B_HEADS, B_DIM, B_DIM), F32),
        chunk=64, tm=tm_p, tt_sel=256, tt_dense=1024)

    n_s = bs * ss
    cos_s, sin_s = _rope_tables(PAST_LEN + jnp.arange(ss), bs)

    def mixer_sample(q, k, v):
        kc = cache_k_win[0].reshape(bs, -1, A_WIDTH)
        vc = cache_v_win[0].reshape(bs, -1, A_WIDTH)
        return _attn_sample(q, k, v, kc, vc, head_sum, head_expand)

    hist_s = jnp.pad(state_conv[0], ((0, 0), (8 - (CONV_WIDTH - 1), 0), (0, 0)))
    ys, ks, vs, cs, ds = _layer(
        x_sample, cos_s, sin_s, 1, wts, cw, dnw, mixer_sample, hist_s, state_delta[0],
        chunk=8, tm=n_s, tt_sel=n_s, tt_dense=n_s)

    win_p = min(BAND * max(DILATIONS), sp)
    shape_kv = lambda arr, b: arr.reshape(1, b, -1, A_HEADS, A_HEAD_DIM)
    return (yp, ys,
            shape_kv(kp[:, sp - win_p:], bp), shape_kv(vp[:, sp - win_p:], bp), cp[None], dp[None],
            shape_kv(ks, bs), shape_kv(vs, bs), cs[None], ds[None])
```

```python
import functools
import math

import jax
import jax.numpy as jnp
from jax import lax
from jax.experimental import pallas as pl
from jax.experimental.pallas import tpu as pltpu

F32 = jnp.float32
BF16 = jnp.bfloat16
U32 = jnp.uint32
HIGHEST = lax.Precision.HIGHEST

D_MODEL = 1024
A_HEADS = 8
A_HEAD_DIM = 64
A_WIDTH = A_HEADS * A_HEAD_DIM
DILATIONS = (1, 4, 16)
BAND = 128
ROPE_THETA = 10000.0
PAST_LEN = 8192
B_HEADS = 4
B_DIM = 128
B_QKV = 3 * B_HEADS * B_DIM
CONV_WIDTH = 4
PEER_HEADS = 8
PEER_KEYS = 128
PEER_TOPK = 16
EPS = 1e-6
NEG_INF = float("-inf")
VMEM_LIMIT = 56 * 1024 * 1024


def _params(*sem, flags=None):
    return pltpu.CompilerParams(dimension_semantics=sem, vmem_limit_bytes=VMEM_LIMIT, flags=flags)


def _dot(a, b, **kw):
    return jnp.dot(a, b, preferred_element_type=F32, **kw)


def _dot_nt(a, b, **kw):
    return lax.dot_general(a, b, (((1,), (1,)), ((), ())), preferred_element_type=F32, **kw)


def _dot_tn(a, b, **kw):
    return lax.dot_general(a, b, (((0,), (0,)), ((), ())), preferred_element_type=F32, **kw)


def _dot_x01(a, b01):
    b = b01.astype(BF16)
    a1 = a.astype(BF16)
    r1 = a - a1.astype(F32)
    a2 = r1.astype(BF16)
    a3 = (r1 - a2.astype(F32)).astype(BF16)
    return _dot(a1, b) + (_dot(a2, b) + _dot(a3, b))


def _sigmoid(x):
    return 1.0 / (1.0 + jnp.exp(-x))


def _proj_in_kernel(x_ref, nw_ref, wa_ref, wb_ref, wz_ref, wg_ref, qnw_ref, knw_ref, cos_ref, sin_ref,
                    hm_ref, alog_ref, dtb_ref, q_ref, k_ref, v_ref, raw_ref, z_ref, gb_ref):
    x = x_ref[...]
    ms = jnp.mean(x * x, axis=-1, keepdims=True)
    xn = (x * lax.rsqrt(ms + EPS) * nw_ref[...]).astype(BF16)
    a = _dot(xn, wa_ref[...])
    hm = hm_ref[...]
    lane = lax.broadcasted_iota(jnp.int32, (1, A_WIDTH), 1)
    first_half = (lane & (A_HEAD_DIM // 2)) == 0
    cos = cos_ref[...]
    sin = sin_ref[...]

    def norm_rope(t, w):
        t2 = t * t
        hi = t2.astype(BF16)
        lo = (t2 - hi.astype(F32)).astype(BF16)
        ss = _dot(hi, hm) + _dot(lo, hm)
        tn = t * lax.rsqrt(ss * (1.0 / A_HEAD_DIM) + EPS) * w
        partner = jnp.where(first_half, pltpu.roll(tn, A_WIDTH - A_HEAD_DIM // 2, 1), pltpu.roll(tn, A_HEAD_DIM // 2, 1))
        return tn * cos + partner * sin

    q_ref[...] = norm_rope(a[:, :A_WIDTH], qnw_ref[...])
    k_ref[...] = norm_rope(a[:, A_WIDTH:2 * A_WIDTH], knw_ref[...])
    v_ref[...] = a[:, 2 * A_WIDTH:]
    raw_ref[...] = _dot(xn, wb_ref[...])
    z_ref[...] = _dot(xn, wz_ref[...])
    zg = _dot(xn, wg_ref[...])
    beta = _sigmoid(zg)
    t = zg + dtb_ref[...]
    softplus = jnp.maximum(t, 0.0) + jnp.log1p(jnp.exp(-jnp.abs(t)))
    g = -jnp.exp(alog_ref[...]) * softplus
    glane = lax.broadcasted_iota(jnp.int32, (1, B_HEADS * B_DIM), 1) & (B_DIM - 1)
    gb_ref[...] = jnp.where(glane == 0, beta, jnp.where(glane == 1, g, 0.0))


def _proj_in(x, pos_cos, pos_sin, pos_blocks, wts, tm):
    t = x.shape[0]
    n = t // tm
    row = lambda i: (i, 0)
    const = lambda i: (0, 0)
    posmap = lambda i: (i % pos_blocks, 0)
    full = lambda a: pl.BlockSpec(a.shape, const)
    outs = [(A_WIDTH, "q"), (A_WIDTH, "k"), (A_WIDTH, "v"), (B_QKV, "raw"), (B_HEADS * B_DIM, "z"), (B_HEADS * B_DIM, "gb")]
    return pl.pallas_call(
        _proj_in_kernel,
        grid=(n,),
        in_specs=[pl.BlockSpec((tm, D_MODEL), row), full(wts["norm_mix"]), full(wts["wa"]), full(wts["wb"]), full(wts["wz"]),
                  full(wts["wg"]), full(wts["qnw"]), full(wts["knw"]),
                  pl.BlockSpec((tm, A_WIDTH), posmap), pl.BlockSpec((tm, A_WIDTH), posmap),
                  full(wts["headmat"]), full(wts["alog"]), full(wts["dtb"])],
        out_specs=[pl.BlockSpec((tm, w), row) for w, _ in outs],
        out_shape=[jax.ShapeDtypeStruct((t, w), F32) for w, _ in outs],
        compiler_params=_params("parallel"),
        name="proj_in",
    )(x, wts["norm_mix"], wts["wa"], wts["wb"], wts["wz"], wts["wg"], wts["qnw"], wts["knw"], pos_cos, pos_sin,
      wts["headmat"], wts["alog"], wts["dtb"])


def _attn_prompt_kernel(q_ref, k_ref, v_ref, o_ref, ob_ref, lb_ref, *, seq):
    lane = lax.broadcasted_iota(jnp.int32, (BAND, 2 * A_HEAD_DIM), 1)
    head0 = lane < A_HEAD_DIM
    qi = lax.broadcasted_iota(jnp.int32, (BAND, 2 * BAND), 0)
    kj = lax.broadcasted_iota(jnp.int32, (BAND, 2 * BAND), 1)
    dist = BAND + qi - kj
    in_band = (dist >= 0) & (dist <= BAND)
    causal = in_band[:, BAND:]
    scale = 1.0 / math.sqrt(A_HEAD_DIM)

    for br, dil in enumerate(DILATIONS):
        nblk = seq // (BAND * dil)
        shift = dil.bit_length() - 1

        def rows(ref, start):
            if dil == 1:
                return ref[pl.ds(start, BAND), :]
            return ref[pl.ds(start, BAND, stride=dil), :]

        def load(u):
            r = u & (dil - 1)
            n = u >> shift
            start = r + n * (BAND * dil)
            qb = rows(q_ref, start)
            k_own = rows(k_ref, start).astype(BF16)
            v_own = rows(v_ref, start).astype(BF16)
            if nblk > 1:
                pstart = jnp.maximum(start - BAND * dil, r)
                kk = jnp.concatenate([rows(k_ref, pstart).astype(BF16), k_own], axis=0)
                vv = jnp.concatenate([rows(v_ref, pstart).astype(BF16), v_own], axis=0)
                mask = in_band & (kj >= jnp.where(n > 0, 0, BAND))
            else:
                kk = jnp.concatenate([k_own, k_own], axis=0)
                vv = jnp.concatenate([v_own, v_own], axis=0)
                mask = in_band & (kj >= BAND)
            return start, qb, kk, vv, mask

        def attend(qb, kk, vv, mask):
            outs = []
            lses = []
            for hmask in (head0, ~head0):
                qh = jnp.where(hmask, qb, 0.0).astype(BF16)
                s = _dot_nt(qh, kk) * scale
                s = jnp.where(mask, s, NEG_INF)
                mx = jnp.max(s, axis=-1, keepdims=True)
                p = jnp.exp(s - mx)
                l = jnp.sum(p, axis=-1, keepdims=True)
                outs.append(_dot(p.astype(BF16), vv) / l)
                lses.append(mx + jnp.log(l))
            return jnp.where(head0, outs[0], outs[1]), jnp.where(head0, lses[0], lses[1])

        per_group = 8

        def group(g, carry):
            loaded = [load(g * per_group + i) for i in range(per_group)]
            results = [attend(*item[1:]) for item in loaded]
            for (start, *_), (out, lse) in zip(loaded, results):
                if dil == 1:
                    ob_ref[br, pl.ds(start, BAND), :] = out
                    lb_ref[br, pl.ds(start, BAND), :] = lse
                else:
                    ob_ref[br, pl.ds(start, BAND, stride=dil), :] = out
                    lb_ref[br, pl.ds(start, BAND, stride=dil), :] = lse
            return carry

        lax.fori_loop(0, seq // BAND // per_group, group, 0)

    def merge(c, carry):
        rs = pl.ds(pl.multiple_of(c * 256, 256), 256)
        l0, l1, l2 = lb_ref[0, rs, :], lb_ref[1, rs, :], lb_ref[2, rs, :]
        m = jnp.maximum(jnp.maximum(l0, l1), l2)
        w0, w1, w2 = jnp.exp(l0 - m), jnp.exp(l1 - m), jnp.exp(l2 - m)
        o_ref[rs, :] = (w0 * ob_ref[0, rs, :] + w1 * ob_ref[1, rs, :] + w2 * ob_ref[2, rs, :]) / (w0 + w1 + w2)
        return carry

    lax.fori_loop(0, seq // 256, merge, 0)


def _attn_prompt(q, k, v):
    b, s, _ = q.shape
    spec = pl.BlockSpec((None, s, 2 * A_HEAD_DIM), lambda i, j: (i, 0, j))
    return pl.pallas_call(
        functools.partial(_attn_prompt_kernel, seq=s),
        grid=(b, A_HEADS // 2),
        in_specs=[spec, spec, spec],
        out_specs=spec,
        out_shape=jax.ShapeDtypeStruct((b, s, A_WIDTH), F32),
        scratch_shapes=[pltpu.VMEM((len(DILATIONS), s, 2 * A_HEAD_DIM), F32),
                        pltpu.VMEM((len(DILATIONS), s, 2 * A_HEAD_DIM), F32)],
        compiler_params=_params("parallel", "parallel"),
        name="attn_prompt",
    )(q, k, v)


def _attn_sample_kernel(q_ref, kn_ref, vn_ref, kc_ref, vc_ref, hs_ref, he_ref, o_ref, ko_ref, vo_ref, kbuf, vbuf,
                        *, win, new):
    nlb = A_WIDTH // 128
    for buf, cache, fresh, shifted in ((kbuf, kc_ref, kn_ref, ko_ref), (vbuf, vc_ref, vn_ref, vo_ref)):
        for j in range(nlb):
            ls = slice(j * 128, (j + 1) * 128)
            buf[j, 0:win, :] = cache[:, ls]
            buf[j, win:win + new, :] = fresh[:, ls]
            shifted[:, ls] = buf[j, new:new + win, :]

    def gather(buf, start, size, stride):
        if stride == 1:
            return jnp.concatenate([buf[j, pl.ds(start, size), :] for j in range(nlb)], axis=1)
        return jnp.concatenate([buf[j, pl.ds(start, size, stride=stride), :] for j in range(nlb)], axis=1)

    hs = hs_ref[...]
    he = he_ref[...]
    scale = 1.0 / math.sqrt(A_HEAD_DIM)
    for l in range(new):
        ql = q_ref[l:l + 1, :]
        outs, lses = [], []
        for dil in DILATIONS:
            start = win + l - BAND * dil
            ks = gather(kbuf, start, BAND, dil)
            vs = gather(vbuf, start, BAND, dil)
            kself = gather(kbuf, win + l, 1, 1)
            vself = gather(vbuf, win + l, 1, 1)
            s = _dot_x01(ks * ql, hs) * scale
            sself = _dot_x01(kself * ql, hs) * scale
            mx = jnp.maximum(jnp.max(s, axis=0, keepdims=True), sself)
            p = jnp.exp(s - mx)
            pself = jnp.exp(sself - mx)
            lsum = jnp.sum(p, axis=0, keepdims=True) + pself
            pe = _dot_x01(p, he)
            o = jnp.sum(pe * vs, axis=0, keepdims=True) + _dot_x01(pself, he) * vself
            outs.append(o / _dot_x01(lsum, he))
            lses.append(_dot_x01(mx + jnp.log(lsum), he))
        m = jnp.maximum(jnp.maximum(lses[0], lses[1]), lses[2])
        w = [jnp.exp(ls - m) for ls in lses]
        o_ref[l:l + 1, :] = (w[0] * outs[0] + w[1] * outs[1] + w[2] * outs[2]) / (w[0] + w[1] + w[2])


def _attn_sample(q, kn, vn, kc, vc, hs, he):
    b, new, _ = q.shape
    win = kc.shape[1]
    assert win >= BAND * max(DILATIONS)
    small = pl.BlockSpec((None, new, A_WIDTH), lambda i: (i, 0, 0))
    big = pl.BlockSpec((None, win, A_WIDTH), lambda i: (i, 0, 0))
    const = lambda a: pl.BlockSpec(a.shape, lambda i: (0, 0))
    return pl.pallas_call(
        functools.partial(_attn_sample_kernel, win=win, new=new),
        grid=(b,),
        in_specs=[small, small, small, big, big, const(hs), const(he)],
        out_specs=[small, big, big],
        out_shape=[jax.ShapeDtypeStruct((b, new, A_WIDTH), F32), jax.ShapeDtypeStruct((b, win, A_WIDTH), F32),
                   jax.ShapeDtypeStruct((b, win, A_WIDTH), F32)],
        scratch_shapes=[pltpu.VMEM((A_WIDTH // 128, win + 8, 128), F32), pltpu.VMEM((A_WIDTH // 128, win + 8, 128), F32)],
        compiler_params=_params("parallel"),
        name="attn_sample",
    )(q, kn, vn, kc, vc, hs, he)


def _dot3(a, b):
    ah = a.astype(BF16)
    al = (a - ah.astype(F32)).astype(BF16)
    bh = b.astype(BF16)
    bl = (b - bh.astype(F32)).astype(BF16)
    return _dot(ah, bh) + (_dot(ah, bl) + _dot(al, bh))


def _dot_01(a01, b):
    a = a01.astype(BF16)
    b1 = b.astype(BF16)
    r1 = b - b1.astype(F32)
    b2 = r1.astype(BF16)
    b3 = (r1 - b2.astype(F32)).astype(BF16)
    return _dot(a, b1) + (_dot(a, b2) + _dot(a, b3))


def _delta_kernel(raw_ref, hist_ref, cw_ref, gb_ref, z_ref, dnw_ref, s0_ref, o_ref, sf_ref, ext, s_scr, *, blk, chunk):
    c = chunk
    nh = B_HEADS
    r = nh * c
    ncol = 3 * nh
    shift = c.bit_length() - 1
    j = pl.program_id(1)

    @pl.when(j == 0)
    def _():
        for col in range(ncol):
            ext[col, 0:8, :] = hist_ref[:, col * B_DIM:(col + 1) * B_DIM]
        for k in range(nh):
            s_scr[k * B_DIM:(k + 1) * B_DIM, :] = s0_ref[k]

    for col in range(ncol):
        ext[col, 8:8 + blk, :] = raw_ref[:, col * B_DIM:(col + 1) * B_DIM]

    ii = lax.broadcasted_iota(jnp.int32, (r, r), 0)
    jj = lax.broadcasted_iota(jnp.int32, (r, r), 1)
    same = (ii >> shift) == (jj >> shift)
    causal = same & (ii >= jj)
    strict = same & (ii > jj)
    upper = same & (ii <= jj)
    eye = (ii == jj).astype(F32)
    tri = causal.astype(F32)
    ones = jnp.ones((r, r), F32)
    row_head = lax.broadcasted_iota(jnp.int32, (r, nh * B_DIM), 0) >> shift
    lane_head = lax.broadcasted_iota(jnp.int32, (r, nh * B_DIM), 1) >> (B_DIM.bit_length() - 1)
    own_head = row_head == lane_head
    dnw = dnw_ref[...]

    def conv_silu(col, base):
        y = cw_ref[0:1, col * B_DIM:(col + 1) * B_DIM] * ext[col, pl.ds(base + 5, c), :]
        for t in range(1, CONV_WIDTH):
            y = y + cw_ref[t:t + 1, col * B_DIM:(col + 1) * B_DIM] * ext[col, pl.ds(base + 5 + t, c), :]
        return y * _sigmoid(y)

    def l2n(t):
        return t * lax.rsqrt(jnp.sum(t * t, axis=-1, keepdims=True) + EPS)

    def stack(parts):
        return jnp.concatenate(parts, axis=0)

    def block_diag(x):
        return jnp.where(own_head, jnp.concatenate([x] * nh, axis=1), 0.0)

    def body(n, carry):
        base = pl.multiple_of(n * c, c)
        q = stack([l2n(conv_silu(k, base)) for k in range(nh)]) * (B_DIM ** -0.5)
        kk = stack([l2n(conv_silu(nh + k, base)) for k in range(nh)])
        v = stack([conv_silu(2 * nh + k, base) for k in range(nh)])
        gbs = [gb_ref[pl.ds(base, c), k * B_DIM:(k + 1) * B_DIM] for k in range(nh)]
        beta = stack([t[:, 0:1] for t in gbs])
        g = stack([t[:, 1:2] for t in gbs])
        gsq = jnp.broadcast_to(g, (r, r))
        gcol = _dot_01(tri, gsq)
        grow = _dot_01(ones, jnp.where(upper, gsq, 0.0))
        gcum = gcol[:, 0:1]
        glast = stack([jnp.broadcast_to(gcol[(k + 1) * c - 1:(k + 1) * c, 0:1], (c, 1)) for k in range(nh)])
        decay = jnp.exp(jnp.where(causal, gcol - grow, NEG_INF))
        kb = kk * beta
        kbh, kh, qh = kb.astype(BF16), kk.astype(BF16), q.astype(BF16)
        lower = jnp.where(strict, _dot_nt(kbh, kh) * decay, 0.0)
        tinv = eye - lower
        lp = lower
        span = 2
        while span < c:
            lp = _dot3(lp, lp)
            tinv = tinv + _dot3(tinv, lp)
            span *= 2
        eg = jnp.exp(gcum)
        u = _dot3(tinv, v * beta)
        w = _dot3(tinv, kb * eg)
        intra = jnp.where(causal, _dot_nt(qh, kh) * decay, 0.0)
        state = s_scr[...]
        sh = state.astype(BF16)
        v_new = u - _dot(block_diag(w).astype(BF16), sh)
        vh = v_new.astype(BF16)
        out = _dot(block_diag(q * eg).astype(BF16), sh) + _dot(intra.astype(BF16), vh)
        kdec = kk * jnp.exp(glast - gcum)
        cd = stack([jnp.broadcast_to(jnp.exp(gcol[(k + 1) * c - 1:(k + 1) * c, 0:1]), (B_DIM, 1)) for k in range(nh)])
        s_scr[...] = state * cd + _dot_tn(block_diag(kdec).astype(BF16), vh)
        on = out * lax.rsqrt(jnp.mean(out * out, axis=-1, keepdims=True) + EPS) * dnw
        for k in range(nh):
            zc = z_ref[pl.ds(base, c), k * B_DIM:(k + 1) * B_DIM]
            o_ref[pl.ds(base, c), k * B_DIM:(k + 1) * B_DIM] = on[k * c:(k + 1) * c] * (zc * _sigmoid(zc))
        return carry

    lax.fori_loop(0, blk // c, body, 0, unroll=2 if blk // c >= 2 else 1)
    for col in range(ncol):
        ext[col, 0:8, :] = ext[col, blk:blk + 8, :]

    @pl.when(j == pl.num_programs(1) - 1)
    def _():
        for k in range(nh):
            sf_ref[k] = s_scr[k * B_DIM:(k + 1) * B_DIM, :]


def _delta(raw, hist, conv_w, gb, z, dnw, s0, chunk, blk):
    b, seq, _ = raw.shape
    hb = B_HEADS * B_DIM
    tok = lambda w: pl.BlockSpec((None, blk, w), lambda i, j: (i, j, 0))
    st = pl.BlockSpec((None, B_HEADS, B_DIM, B_DIM), lambda i, j: (i, 0, 0, 0))
    return pl.pallas_call(
        functools.partial(_delta_kernel, blk=blk, chunk=chunk),
        grid=(b, seq // blk),
        in_specs=[tok(B_QKV), pl.BlockSpec((None, 8, B_QKV), lambda i, j: (i, 0, 0)),
                  pl.BlockSpec((CONV_WIDTH, B_QKV), lambda i, j: (0, 0)), tok(hb), tok(hb),
                  pl.BlockSpec((1, B_DIM), lambda i, j: (0, 0)), st],
        out_specs=[tok(hb), st],
        out_shape=[jax.ShapeDtypeStruct((b, seq, hb), F32), jax.ShapeDtypeStruct((b, B_HEADS, B_DIM, B_DIM), F32)],
        scratch_shapes=[pltpu.VMEM((3 * B_HEADS, blk + 8, B_DIM), F32), pltpu.VMEM((B_HEADS * B_DIM, B_DIM), F32)],
        compiler_params=_params("parallel", "arbitrary"),
        name="delta",
    )(raw, hist, conv_w, gb, z, dnw, s0)


def _proj_out_kernel(x_ref, a_ref, b_ref, wa_ref, wb_ref, nw_ref, h_ref, hnt_ref):
    h = x_ref[...] + _dot(a_ref[...].astype(BF16), wa_ref[...]) + _dot(b_ref[...].astype(BF16), wb_ref[...])
    h_ref[...] = h
    hn = h * lax.rsqrt(jnp.mean(h * h, axis=-1, keepdims=True) + EPS) * nw_ref[...]
    hnt = hn.T.astype(BF16)
    for c in range(hnt_ref.shape[0]):
        hnt_ref[c] = hnt[:, c * 128:(c + 1) * 128]


def _proj_out(x, a, b, wts, tm):
    t = x.shape[0]
    row = lambda i: (i, 0)
    full = lambda arr: pl.BlockSpec(arr.shape, lambda i: (0, 0))
    return pl.pallas_call(
        _proj_out_kernel,
        grid=(t // tm,),
        in_specs=[pl.BlockSpec((tm, D_MODEL), row), pl.BlockSpec((tm, A_WIDTH), row), pl.BlockSpec((tm, A_WIDTH), row),
                  full(wts["wo_a"]), full(wts["wo_b"]), full(wts["norm_ffn"])],
        out_specs=[pl.BlockSpec((tm, D_MODEL), row), pl.BlockSpec((tm // 128, D_MODEL, 128), lambda i: (i, 0, 0))],
        out_shape=[jax.ShapeDtypeStruct((t, D_MODEL), F32), jax.ShapeDtypeStruct((t // 128, D_MODEL, 128), BF16)],
        compiler_params=_params("parallel"),
        name="proj_out",
    )(x, a, b, wts["wo_a"], wts["wo_b"], wts["norm_ffn"])


def _top16_ranked(x):
    t = x.shape[1]
    slot = lax.broadcasted_iota(jnp.int32, (PEER_TOPK, t), 0).astype(F32)
    vals = jnp.full((PEER_TOPK, t), NEG_INF, F32)
    rank = jnp.full(x.shape, float(PEER_TOPK), F32)
    filled = jnp.zeros((1, t), F32)
    for _ in range(PEER_TOPK):
        m = jnp.max(x, axis=0, keepdims=True)
        eq = x == m
        cnt = jnp.sum(jnp.where(eq, 1.0, 0.0), axis=0, keepdims=True)
        vals = jnp.where((slot >= filled) & (slot < filled + cnt), m, vals)
        rank = jnp.where(eq, jnp.minimum(filled, float(PEER_TOPK)), rank)
        filled = filled + cnt
        x = jnp.where(eq, NEG_INF, x)
    return vals, rank


def _kth_largest(x, k):
    t = x.shape[1]
    kth = jnp.full((1, t), NEG_INF, F32)
    filled = jnp.zeros((1, t), F32)
    for _ in range(k):
        m = jnp.max(x, axis=0, keepdims=True)
        eq = x == m
        cnt = jnp.sum(jnp.where(eq, 1.0, 0.0), axis=0, keepdims=True)
        kth = jnp.where((filled < k) & (filled + cnt >= k), m, kth)
        filled = filled + cnt
        x = jnp.where(eq, NEG_INF, x)
    return kth


def _peer_select_kernel(hnt_ref, wpq_ref, sk_ref, n1_ref, q1_ref, r2_ref, q2_ref, sub_scr, top_scr, rank_scr):
    nslab = hnt_ref.shape[0]
    hnt = jnp.concatenate([hnt_ref[c] for c in range(nslab)], axis=1)

    def half(hc, carry):
        row0 = pl.multiple_of(hc * PEER_KEYS, PEER_KEYS)
        qt = _dot(wpq_ref[pl.ds(row0, PEER_KEYS), :], hnt)
        sub = _dot(sk_ref[hc & 1], qt.astype(BF16))
        sub_scr[hc] = sub
        top_scr[hc], rank_scr[hc] = _top16_ranked(sub)
        return carry

    lax.fori_loop(0, 2 * PEER_HEADS, half, 0, unroll=2)

    def slabs(ref, h, x):
        for c in range(nslab):
            ref[h, c] = x[:, c * 128:(c + 1) * 128].astype(ref.dtype)

    def head(h, carry):
        a = top_scr[2 * h]
        b = top_scr[2 * h + 1]
        pieces = [a[0:1] + b] + [a[i:i + 1] + b[0:8] for i in range(1, 8)] + [a[8:16] + b[0:1]]
        cand = jnp.concatenate(pieces, axis=0)
        tau = _kth_largest(cand, PEER_TOPK)
        top = a[0:1] + b[0:1]
        zsum = jnp.sum(jnp.where(cand >= tau, jnp.exp(cand - top), 0.0), axis=0, keepdims=True)
        cnt = jnp.zeros_like(a)
        for j in range(PEER_TOPK):
            cnt = cnt + jnp.where(a + b[j:j + 1] >= tau, 1.0, 0.0)
        rank1 = rank_scr[2 * h]
        n1 = jnp.zeros_like(rank1)
        for i in range(PEER_TOPK):
            n1 = jnp.where(rank1 == float(i), cnt[i:i + 1], n1)
        slabs(n1_ref, h, n1)
        slabs(r2_ref, h, rank_scr[2 * h + 1])
        slabs(q1_ref, h, jnp.exp(sub_scr[2 * h] - a[0:1]) / zsum)
        slabs(q2_ref, h, jnp.exp(sub_scr[2 * h + 1] - b[0:1]))
        return carry

    lax.fori_loop(0, PEER_HEADS, head, 0, unroll=2)


def _peer_select(hnt, wts, tt):
    t = hnt.shape[0] * 128
    out = pl.BlockSpec((PEER_HEADS, tt // 128, PEER_KEYS, 128), lambda i: (0, i, 0, 0))
    shape = lambda dt: jax.ShapeDtypeStruct((PEER_HEADS, t // 128, PEER_KEYS, 128), dt)
    return pl.pallas_call(
        _peer_select_kernel,
        grid=(t // tt,),
        in_specs=[pl.BlockSpec((tt // 128, D_MODEL, 128), lambda i: (i, 0, 0)),
                  pl.BlockSpec(wts["wpq_t"].shape, lambda i: (0, 0)),
                  pl.BlockSpec(wts["sub_keys"].shape, lambda i: (0, 0, 0))],
        out_specs=[out, out, out, out],
        out_shape=[shape(F32), shape(F32), shape(BF16), shape(BF16)],
        scratch_shapes=[pltpu.VMEM((2 * PEER_HEADS, PEER_KEYS, tt), F32), pltpu.VMEM((2 * PEER_HEADS, PEER_TOPK, tt), F32),
                        pltpu.VMEM((2 * PEER_HEADS, PEER_KEYS, tt), F32)],
        compiler_params=_params("parallel"),
        name="peer_select",
    )(hnt, wts["wpq_t"], wts["sub_keys"])


A_PER_TILE = 8


def _rows_bf16(row):
    return jnp.broadcast_to(row, (PEER_KEYS, 128)).astype(BF16)


def _peer_dense_kernel(hnt_ref, u_ref, vt_ref, n1_ref, q1_ref, r2_ref, q2_ref, h_ref, y_ref, acc_ref, pre_ref, w_ref,
                       r2_scr, q2_scr, *, tt):
    j = pl.program_id(1)
    nslab = tt // 128

    @pl.when(j == 0)
    def _():
        acc_ref[...] = jnp.zeros_like(acc_ref)
        r2_scr[...] = r2_ref[...]
        q2_scr[...] = q2_ref[...]

    per_sub = min(nslab, 2)
    nsub = nslab // per_sub
    zero = jnp.zeros((PEER_KEYS, 128), BF16)
    group = 2

    def mm_u(s):
        hn = jnp.concatenate([hnt_ref[s * per_sub + c] for c in range(per_sub)], axis=1)
        pre = _dot(u_ref[...], hn).astype(BF16)
        for c in range(per_sub):
            pre_ref[s * per_sub + c] = pre[:, c * 128:(c + 1) * 128]

    def build(tc, carry):
        for ia0 in range(0, A_PER_TILE, group):
            gates = [zero] * group
            for h in range(PEER_HEADS):
                r2 = r2_scr[h, tc]
                q2 = q2_scr[h, tc]
                for k in range(group):
                    n1 = _rows_bf16(n1_ref[h, tc, ia0 + k:ia0 + k + 1, :])
                    q1 = _rows_bf16(q1_ref[h, tc, ia0 + k:ia0 + k + 1, :])
                    gates[k] = gates[k] + jnp.where(r2 < n1, q2, zero) * q1
            for k in range(group):
                rows = slice((ia0 + k) * PEER_KEYS, (ia0 + k + 1) * PEER_KEYS)
                p = pre_ref[tc, rows, :]
                act = (0.5 * p) * (1.0 + lax.erf(p * (1.0 / math.sqrt(2.0))))
                w_ref[tc, rows, :] = gates[k] * act
        return carry

    def mm_v(s):
        w = jnp.concatenate([w_ref[s * per_sub + c] for c in range(per_sub)], axis=1)
        out = _dot(vt_ref[...], w)
        for c in range(per_sub):
            acc_ref[s * per_sub + c] += out[:, c * 128:(c + 1) * 128]

    for s in range(nsub):
        mm_u(s)
    lax.fori_loop(0, nslab, build, 0)
    for s in range(nsub):
        mm_v(s)

    @pl.when(j == pl.num_programs(1) - 1)
    def _():
        for c in range(nslab):
            y_ref[c * 128:(c + 1) * 128, :] = h_ref[c * 128:(c + 1) * 128, :] + acc_ref[c].T


def _peer_dense(hnt, h, sel, wts, tt):
    t = hnt.shape[0] * 128
    n1, q1, r2, q2 = sel
    et = A_PER_TILE * PEER_KEYS
    nslab = tt // 128
    a_spec = pl.BlockSpec((PEER_HEADS, nslab, A_PER_TILE, 128), lambda i, j: (0, i, j, 0))
    b_spec = pl.BlockSpec((PEER_HEADS, nslab, PEER_KEYS, 128), lambda i, j: (0, i, 0, 0))
    return pl.pallas_call(
        functools.partial(_peer_dense_kernel, tt=tt),
        grid=(t // tt, PEER_KEYS // A_PER_TILE),
        in_specs=[pl.BlockSpec((nslab, D_MODEL, 128), lambda i, j: (i, 0, 0)),
                  pl.BlockSpec((et, D_MODEL), lambda i, j: (j, 0)),
                  pl.BlockSpec((D_MODEL, et), lambda i, j: (0, j)),
                  a_spec, a_spec, b_spec, b_spec,
                  pl.BlockSpec((tt, D_MODEL), lambda i, j: (i, 0))],
        out_specs=pl.BlockSpec((tt, D_MODEL), lambda i, j: (i, 0)),
        out_shape=jax.ShapeDtypeStruct((t, D_MODEL), F32),
        scratch_shapes=[pltpu.VMEM((nslab, D_MODEL, 128), F32), pltpu.VMEM((nslab, et, 128), BF16),
                        pltpu.VMEM((nslab, et, 128), BF16),
                        pltpu.VMEM((PEER_HEADS, nslab, PEER_KEYS, 128), BF16),
                        pltpu.VMEM((PEER_HEADS, nslab, PEER_KEYS, 128), BF16)],
        compiler_params=_params("parallel", "arbitrary"),
        name="peer_dense",
    )(hnt, wts["u"], wts["v_t"], n1, q1, r2, q2, h)


def _rope_tables(pos, reps):
    half = A_HEAD_DIM // 2
    inv_freq = ROPE_THETA ** (-jnp.arange(half, dtype=F32) / half)
    ang = pos.astype(F32)[:, None] * inv_freq[None, :]
    cos = jnp.cos(ang)
    sin = jnp.sin(ang)
    cos_t = jnp.tile(jnp.concatenate([cos, cos], axis=-1), (reps, A_HEADS))
    sin_t = jnp.tile(jnp.concatenate([-sin, sin], axis=-1), (reps, A_HEADS))
    return cos_t, sin_t


def _prep_weights(norm_mix_w, w_in, q_norm_w, k_norm_w, a_log, dt_bias, w_out, norm_ffn_w, w_pq, sub_keys, expert_u, expert_v):
    na = 3 * A_WIDTH
    hb = B_HEADS * B_DIM
    wgate = w_in[:, na + B_QKV + hb:]
    wg = jnp.zeros((D_MODEL, B_HEADS, B_DIM), F32)
    wg = wg.at[:, :, 0].set(wgate[:, :B_HEADS]).at[:, :, 1].set(wgate[:, B_HEADS:])
    lane_head = jnp.arange(A_WIDTH) // A_HEAD_DIM
    put1 = lambda vec: jnp.zeros((B_HEADS, B_DIM), F32).at[:, 1].set(vec).reshape(1, hb)
    return {
        "norm_mix": norm_mix_w.reshape(1, D_MODEL),
        "wa": w_in[:, :na].astype(BF16),
        "wb": w_in[:, na:na + B_QKV].astype(BF16),
        "wz": w_in[:, na + B_QKV:na + B_QKV + hb].astype(BF16),
        "wg": wg.reshape(D_MODEL, hb).astype(BF16),
        "qnw": jnp.tile(q_norm_w, A_HEADS).reshape(1, A_WIDTH),
        "knw": jnp.tile(k_norm_w, A_HEADS).reshape(1, A_WIDTH),
        "headmat": (lane_head[:, None] == lane_head[None, :]).astype(BF16),
        "alog": put1(a_log),
        "dtb": put1(dt_bias),
        "wo_a": w_out[:A_WIDTH].astype(BF16),
        "wo_b": w_out[A_WIDTH:].astype(BF16),
        "norm_ffn": norm_ffn_w.reshape(1, D_MODEL),
        "wpq_t": w_pq.T.astype(BF16),
        "sub_keys": sub_keys.astype(BF16),
        "u": expert_u.astype(BF16),
        "v_t": expert_v.T.astype(BF16),
    }


def _layer(x, pos_cos, pos_sin, pos_blocks, wts, conv_w, dnw, mixer_a, hist, s0, chunk, tm, tt_sel, tt_dense):
    b, t, _ = x.shape
    xf = x.reshape(b * t, D_MODEL)
    q, k, v, raw, z, gb = _proj_in(xf, pos_cos, pos_sin, pos_blocks, wts, tm)
    a_out, k_buf, v_buf = mixer_a(q.reshape(b, t, A_WIDTH), k.reshape(b, t, A_WIDTH), v.reshape(b, t, A_WIDTH))
    raw = raw.reshape(b, t, B_QKV)
    conv_state = raw[:, t - (CONV_WIDTH - 1):]
    pad = (-t) % chunk
    pad3 = lambda arr: jnp.pad(arr.reshape(b, t, -1), ((0, 0), (0, pad), (0, 0)))
    b_out, s_fin = _delta(pad3(raw), hist, conv_w, pad3(gb), pad3(z), dnw, s0, chunk, min(512, t + pad))
    b_out = b_out[:, :t].reshape(b * t, B_HEADS * B_DIM)
    h, hnt = _proj_out(xf, a_out.reshape(b * t, A_WIDTH), b_out, wts, tm)
    sel = _peer_select(hnt, wts, tt_sel)
    y = _peer_dense(hnt, h, sel, wts, tt_dense)
    return y.reshape(b, t, D_MODEL), k_buf, v_buf, conv_state, s_fin


def kernel(x_prompt, x_sample, cache_k_win, cache_v_win, state_conv, state_delta, norm_mix_w, w_in, q_norm_w, k_norm_w, conv_w, a_log, dt_bias, delta_norm_w, w_out, norm_ffn_w, w_pq, sub_keys, expert_u, expert_v):
    assert w_in.shape[0] == 1, "one layer"
    bp, sp, _ = x_prompt.shape
    bs, ss, _ = x_sample.shape
    wts = _prep_weights(norm_mix_w[0], w_in[0], q_norm_w[0], k_norm_w[0], a_log[0], dt_bias[0], w_out[0], norm_ffn_w[0],
                        w_pq[0], sub_keys[0], expert_u[0], expert_v[0])
    cw = conv_w[0]
    dnw = delta_norm_w[0].reshape(1, B_DIM)
    lane_head = jnp.arange(A_WIDTH) // A_HEAD_DIM
    head_sum = (lane_head[:, None] == jnp.arange(128)[None, :]).astype(F32)
    head_expand = head_sum.T

    tm_p = 512
    cos_p, sin_p = _rope_tables(jnp.arange(sp), 1)

    def mixer_prompt(q, k, v):
        return _attn_prompt(q, k, v), k, v

    yp, kp, vp, cp, dp = _layer(
        x_prompt, cos_p, sin_p, sp // tm_p, wts, cw, dnw, mixer_prompt,
        jnp.zeros((bp, 8, B_QKV), F32), jnp.zeros((bp, B_HEADS, B_DIM, B_DIM), F32),
        chunk=64, tm=tm_p, tt_sel=256, tt_dense=1024)

    n_s = bs * ss
    cos_s, sin_s = _rope_tables(PAST_LEN + jnp.arange(ss), bs)

    def mixer_sample(q, k, v):
        kc = cache_k_win[0].reshape(bs, -1, A_WIDTH)
        vc = cache_v_win[0].reshape(bs, -1, A_WIDTH)
        return _attn_sample(q, k, v, kc, vc, head_sum, head_expand)

    hist_s = jnp.pad(state_conv[0], ((0, 0), (8 - (CONV_WIDTH - 1), 0), (0, 0)))
    ys, ks, vs, cs, ds = _layer(
        x_sample, cos_s, sin_s, 1, wts, cw, dnw, mixer_sample, hist_s, state_delta[0],
        chunk=8, tm=n_s, tt_sel=n_s, tt_dense=n_s)

    win_p = min(BAND * max(DILATIONS), sp)
    shape_kv = lambda arr, b: arr.reshape(1, b, -1, A_HEADS, A_HEAD_DIM)
    return (yp, ys,
            shape_kv(kp[:, sp - win_p:], bp), shape_kv(vp[:, sp - win_p:], bp), cp[None], dp[None],
            shape_kv(ks, bs), shape_kv(vs, bs), cs[None], ds[None])
```

```python
import functools
import math

import jax
import jax.numpy as jnp
from jax import lax
from jax.experimental import pallas as pl
from jax.experimental.pallas import tpu as pltpu

F32 = jnp.float32
BF16 = jnp.bfloat16

D_MODEL = 1024
A_HEADS = 8
A_HEAD_DIM = 64
A_WIDTH = A_HEADS * A_HEAD_DIM
DILATIONS = (1, 4, 16)
BAND = 128
ROPE_THETA = 10000.0
PAST_LEN = 8192
B_HEADS = 4
B_DIM = 128
B_QKV = 3 * B_HEADS * B_DIM
CONV_WIDTH = 4
PEER_HEADS = 8
PEER_KEYS = 128
PEER_TOPK = 16
EPS = 1e-6
NEG_INF = float("-inf")
VMEM_LIMIT = 56 * 1024 * 1024


def _params(*sem):
    return pltpu.CompilerParams(dimension_semantics=sem, vmem_limit_bytes=VMEM_LIMIT)


def _dot(a, b, **kw):
    return jnp.dot(a, b, preferred_element_type=F32, **kw)


def _dot_nt(a, b, **kw):
    return lax.dot_general(a, b, (((1,), (1,)), ((), ())), preferred_element_type=F32, **kw)


def _dot_tn(a, b, **kw):
    return lax.dot_general(a, b, (((0,), (0,)), ((), ())), preferred_element_type=F32, **kw)


def _dot_x01(a, b01):
    b = b01.astype(BF16)
    a1 = a.astype(BF16)
    r1 = a - a1.astype(F32)
    a2 = r1.astype(BF16)
    a3 = (r1 - a2.astype(F32)).astype(BF16)
    return _dot(a1, b) + (_dot(a2, b) + _dot(a3, b))


def _sigmoid(x):
    return 1.0 / (1.0 + jnp.exp(-x))


def _proj_in_kernel(x_ref, nw_ref, wa_ref, wb_ref, wz_ref, wg_ref, qnw_ref, knw_ref, cos_ref, sin_ref,
                    hm_ref, alog_ref, dtb_ref, q_ref, k_ref, v_ref, raw_ref, z_ref, gb_ref):
    x = x_ref[...]
    ms = jnp.mean(x * x, axis=-1, keepdims=True)
    xn = (x * lax.rsqrt(ms + EPS) * nw_ref[...]).astype(BF16)
    a = _dot(xn, wa_ref[...])
    hm = hm_ref[...]
    lane = lax.broadcasted_iota(jnp.int32, (1, A_WIDTH), 1)
    first_half = (lane & (A_HEAD_DIM // 2)) == 0
    cos = cos_ref[...]
    sin = sin_ref[...]

    def norm_rope(t, w):
        t2 = t * t
        hi = t2.astype(BF16)
        lo = (t2 - hi.astype(F32)).astype(BF16)
        ss = _dot(hi, hm) + _dot(lo, hm)
        tn = t * lax.rsqrt(ss * (1.0 / A_HEAD_DIM) + EPS) * w
        partner = jnp.where(first_half, pltpu.roll(tn, A_WIDTH - A_HEAD_DIM // 2, 1), pltpu.roll(tn, A_HEAD_DIM // 2, 1))
        return tn * cos + partner * sin

    q_ref[...] = norm_rope(a[:, :A_WIDTH], qnw_ref[...])
    k_ref[...] = norm_rope(a[:, A_WIDTH:2 * A_WIDTH], knw_ref[...])
    v_ref[...] = a[:, 2 * A_WIDTH:]
    raw_ref[...] = _dot(xn, wb_ref[...])
    z_ref[...] = _dot(xn, wz_ref[...])
    zg = _dot(xn, wg_ref[...])
    beta = _sigmoid(zg)
    t = zg + dtb_ref[...]
    softplus = jnp.maximum(t, 0.0) + jnp.log1p(jnp.exp(-jnp.abs(t)))
    g = -jnp.exp(alog_ref[...]) * softplus
    glane = lax.broadcasted_iota(jnp.int32, (1, B_HEADS * B_DIM), 1) & (B_DIM - 1)
    gb_ref[...] = jnp.where(glane == 0, beta, jnp.where(glane == 1, g, 0.0))


def _proj_in(x, pos_cos, pos_sin, pos_blocks, wts, tm):
    t = x.shape[0]
    n = t // tm
    row = lambda i: (i, 0)
    const = lambda i: (0, 0)
    posmap = lambda i: (i % pos_blocks, 0)
    full = lambda a: pl.BlockSpec(a.shape, const)
    outs = [(A_WIDTH, "q"), (A_WIDTH, "k"), (A_WIDTH, "v"), (B_QKV, "raw"), (B_HEADS * B_DIM, "z"), (B_HEADS * B_DIM, "gb")]
    return pl.pallas_call(
        _proj_in_kernel,
        grid=(n,),
        in_specs=[pl.BlockSpec((tm, D_MODEL), row), full(wts["norm_mix"]), full(wts["wa"]), full(wts["wb"]), full(wts["wz"]),
                  full(wts["wg"]), full(wts["qnw"]), full(wts["knw"]),
                  pl.BlockSpec((tm, A_WIDTH), posmap), pl.BlockSpec((tm, A_WIDTH), posmap),
                  full(wts["headmat"]), full(wts["alog"]), full(wts["dtb"])],
        out_specs=[pl.BlockSpec((tm, w), row) for w, _ in outs],
        out_shape=[jax.ShapeDtypeStruct((t, w), F32) for w, _ in outs],
        compiler_params=_params("parallel"),
        name="proj_in",
    )(x, wts["norm_mix"], wts["wa"], wts["wb"], wts["wz"], wts["wg"], wts["qnw"], wts["knw"], pos_cos, pos_sin,
      wts["headmat"], wts["alog"], wts["dtb"])


def _attn_prompt_kernel(q_ref, k_ref, v_ref, o_ref, ob_ref, lb_ref, *, seq):
    lane = lax.broadcasted_iota(jnp.int32, (BAND, 2 * A_HEAD_DIM), 1)
    head0 = lane < A_HEAD_DIM
    qi = lax.broadcasted_iota(jnp.int32, (BAND, 2 * BAND), 0)
    kj = lax.broadcasted_iota(jnp.int32, (BAND, 2 * BAND), 1)
    dist = BAND + qi - kj
    in_band = (dist >= 0) & (dist <= BAND)
    scale = 1.0 / math.sqrt(A_HEAD_DIM)

    for br, dil in enumerate(DILATIONS):
        nblk = seq // (BAND * dil)
        shift = dil.bit_length() - 1

        def rows(ref, start):
            if dil == 1:
                return ref[pl.ds(start, BAND), :]
            return ref[pl.ds(start, BAND, stride=dil), :]

        def load(u):
            r = u & (dil - 1)
            n = u >> shift
            start = r + n * (BAND * dil)
            qb = rows(q_ref, start)
            k_own = rows(k_ref, start).astype(BF16)
            v_own = rows(v_ref, start).astype(BF16)
            if nblk > 1:
                pstart = jnp.maximum(start - BAND * dil, r)
                kk = jnp.concatenate([rows(k_ref, pstart).astype(BF16), k_own], axis=0)
                vv = jnp.concatenate([rows(v_ref, pstart).astype(BF16), v_own], axis=0)
                mask = in_band & (kj >= jnp.where(n > 0, 0, BAND))
            else:
                kk = jnp.concatenate([k_own, k_own], axis=0)
                vv = jnp.concatenate([v_own, v_own], axis=0)
                mask = in_band & (kj >= BAND)
            return start, qb, kk, vv, mask

        def attend(qb, kk, vv, mask):
            outs = []
            lses = []
            for hmask in (head0, ~head0):
                qh = jnp.where(hmask, qb, 0.0).astype(BF16)
                s = _dot_nt(qh, kk) * scale
                s = jnp.where(mask, s, NEG_INF)
                mx = jnp.max(s, axis=-1, keepdims=True)
                p = jnp.exp(s - mx)
                l = jnp.sum(p, axis=-1, keepdims=True)
                outs.append(_dot(p.astype(BF16), vv) / l)
                lses.append(mx + jnp.log(l))
            return jnp.where(head0, outs[0], outs[1]), jnp.where(head0, lses[0], lses[1])

        per_group = 8

        def group(g, carry):
            loaded = [load(g * per_group + i) for i in range(per_group)]
            results = [attend(*item[1:]) for item in loaded]
            for (start, *_), (out, lse) in zip(loaded, results):
                if dil == 1:
                    ob_ref[br, pl.ds(start, BAND), :] = out
                    lb_ref[br, pl.ds(start, BAND), :] = lse
                else:
                    ob_ref[br, pl.ds(start, BAND, stride=dil), :] = out
                    lb_ref[br, pl.ds(start, BAND, stride=dil), :] = lse
            return carry

        lax.fori_loop(0, seq // BAND // per_group, group, 0)

    def merge(c, carry):
        rs = pl.ds(pl.multiple_of(c * 256, 256), 256)
        l0, l1, l2 = lb_ref[0, rs, :], lb_ref[1, rs, :], lb_ref[2, rs, :]
        m = jnp.maximum(jnp.maximum(l0, l1), l2)
        w0, w1, w2 = jnp.exp(l0 - m), jnp.exp(l1 - m), jnp.exp(l2 - m)
        o_ref[rs, :] = (w0 * ob_ref[0, rs, :] + w1 * ob_ref[1, rs, :] + w2 * ob_ref[2, rs, :]) / (w0 + w1 + w2)
        return carry

    lax.fori_loop(0, seq // 256, merge, 0)


def _attn_prompt(q, k, v):
    b, s, _ = q.shape
    spec = pl.BlockSpec((None, s, 2 * A_HEAD_DIM), lambda i, j: (i, 0, j))
    return pl.pallas_call(
        functools.partial(_attn_prompt_kernel, seq=s),
        grid=(b, A_HEADS // 2),
        in_specs=[spec, spec, spec],
        out_specs=spec,
        out_shape=jax.ShapeDtypeStruct((b, s, A_WIDTH), F32),
        scratch_shapes=[pltpu.VMEM((len(DILATIONS), s, 2 * A_HEAD_DIM), F32),
                        pltpu.VMEM((len(DILATIONS), s, 2 * A_HEAD_DIM), F32)],
        compiler_params=_params("parallel", "parallel"),
        name="attn_prompt",
    )(q, k, v)


SAMPLE_PAD = 16


def _attn_sample_kernel(q_ref, kn_ref, vn_ref, kc_ref, vc_ref, hs_ref, he_ref, o_ref, ko_ref, vo_ref, kbuf, vbuf,
                        *, win, new):
    nlb = A_WIDTH // 128
    for buf, cache, fresh, shifted in ((kbuf, kc_ref, kn_ref, ko_ref), (vbuf, vc_ref, vn_ref, vo_ref)):
        for j in range(nlb):
            ls = slice(j * 128, (j + 1) * 128)
            buf[j, 0:win, :] = cache[:, ls]
            buf[j, win:win + new, :] = fresh[:, ls]
            buf[j, win + new:win + SAMPLE_PAD, :] = jnp.zeros((SAMPLE_PAD - new, 128), F32)
            shifted[:, ls] = buf[j, new:new + win, :]

    def gather(buf, start, size, stride):
        if stride == 1:
            return jnp.concatenate([buf[j, pl.ds(start, size), :] for j in range(nlb)], axis=1)
        return jnp.concatenate([buf[j, pl.ds(start, size, stride=stride), :] for j in range(nlb)], axis=1)

    hs = hs_ref[...]
    he = he_ref[...]
    scale = 1.0 / math.sqrt(A_HEAD_DIM)
    for l in range(new):
        ql = q_ref[l:l + 1, :]
        outs, lses = [], []
        for dil in DILATIONS:
            start = win + l - BAND * dil
            ks = gather(kbuf, start, BAND, dil)
            vs = gather(vbuf, start, BAND, dil)
            kself = gather(kbuf, win + l, 8, 1)
            vself = gather(vbuf, win + l, 1, 1)
            s_all = _dot_x01(jnp.concatenate([ks, kself], axis=0) * ql, hs) * scale
            s = s_all[:BAND]
            sself = s_all[BAND:BAND + 1]
            mx = jnp.maximum(jnp.max(s, axis=0, keepdims=True), sself)
            p = jnp.exp(s - mx)
            pself = jnp.exp(sself - mx)
            lsum = jnp.sum(p, axis=0, keepdims=True) + pself
            pe = _dot_x01(p, he)
            rows8 = jnp.concatenate([pself, lsum, mx + jnp.log(lsum), jnp.zeros((5, 128), F32)], axis=0)
            exp8 = _dot_x01(rows8, he)
            o = jnp.sum(pe * vs, axis=0, keepdims=True) + exp8[0:1] * vself
            outs.append(o / exp8[1:2])
            lses.append(exp8[2:3])
        m = jnp.maximum(jnp.maximum(lses[0], lses[1]), lses[2])
        w = [jnp.exp(ls - m) for ls in lses]
        o_ref[l:l + 1, :] = (w[0] * outs[0] + w[1] * outs[1] + w[2] * outs[2]) / (w[0] + w[1] + w[2])


def _attn_sample(q, kn, vn, kc, vc, hs, he):
    b, new, _ = q.shape
    win = kc.shape[1]
    assert win >= BAND * max(DILATIONS)
    small = pl.BlockSpec((None, new, A_WIDTH), lambda i: (i, 0, 0))
    big = pl.BlockSpec((None, win, A_WIDTH), lambda i: (i, 0, 0))
    const = lambda a: pl.BlockSpec(a.shape, lambda i: (0, 0))
    return pl.pallas_call(
        functools.partial(_attn_sample_kernel, win=win, new=new),
        grid=(b,),
        in_specs=[small, small, small, big, big, const(hs), const(he)],
        out_specs=[small, big, big],
        out_shape=[jax.ShapeDtypeStruct((b, new, A_WIDTH), F32), jax.ShapeDtypeStruct((b, win, A_WIDTH), F32),
                   jax.ShapeDtypeStruct((b, win, A_WIDTH), F32)],
        scratch_shapes=[pltpu.VMEM((A_WIDTH // 128, win + SAMPLE_PAD, 128), F32),
                        pltpu.VMEM((A_WIDTH // 128, win + SAMPLE_PAD, 128), F32)],
        compiler_params=_params("parallel"),
        name="attn_sample",
    )(q, kn, vn, kc, vc, hs, he)


def _dot3(a, b):
    ah = a.astype(BF16)
    al = (a - ah.astype(F32)).astype(BF16)
    bh = b.astype(BF16)
    bl = (b - bh.astype(F32)).astype(BF16)
    return _dot(ah, bh) + (_dot(ah, bl) + _dot(al, bh))


def _dot_01(a01, b):
    a = a01.astype(BF16)
    b1 = b.astype(BF16)
    r1 = b - b1.astype(F32)
    b2 = r1.astype(BF16)
    b3 = (r1 - b2.astype(F32)).astype(BF16)
    return _dot(a, b1) + (_dot(a, b2) + _dot(a, b3))


def _delta_kernel(raw_ref, hist_ref, cw_ref, gb_ref, z_ref, dnw_ref, s0_ref, o_ref, sf_ref, ext, s_scr, *, blk, chunk):
    c = chunk
    nh = B_HEADS
    r = nh * c
    ncol = 3 * nh
    shift = c.bit_length() - 1
    j = pl.program_id(1)

    @pl.when(j == 0)
    def _():
        for col in range(ncol):
            ext[col, 0:8, :] = hist_ref[:, col * B_DIM:(col + 1) * B_DIM]
        for k in range(nh):
            s_scr[k * B_DIM:(k + 1) * B_DIM, :] = s0_ref[k]

    for col in range(ncol):
        ext[col, 8:8 + blk, :] = raw_ref[:, col * B_DIM:(col + 1) * B_DIM]

    ii = lax.broadcasted_iota(jnp.int32, (r, r), 0)
    jj = lax.broadcasted_iota(jnp.int32, (r, r), 1)
    same = (ii >> shift) == (jj >> shift)
    causal = same & (ii >= jj)
    strict = same & (ii > jj)
    upper = same & (ii <= jj)
    eye = (ii == jj).astype(F32)
    tri = causal.astype(F32)
    ones = jnp.ones((r, r), F32)
    row_head = lax.broadcasted_iota(jnp.int32, (r, nh * B_DIM), 0) >> shift
    lane_head = lax.broadcasted_iota(jnp.int32, (r, nh * B_DIM), 1) >> (B_DIM.bit_length() - 1)
    own_head = row_head == lane_head
    dnw = dnw_ref[...]

    def conv_silu(col, base):
        y = cw_ref[0:1, col * B_DIM:(col + 1) * B_DIM] * ext[col, pl.ds(base + 5, c), :]
        for t in range(1, CONV_WIDTH):
            y = y + cw_ref[t:t + 1, col * B_DIM:(col + 1) * B_DIM] * ext[col, pl.ds(base + 5 + t, c), :]
        return y * _sigmoid(y)

    def l2n(t):
        return t * lax.rsqrt(jnp.sum(t * t, axis=-1, keepdims=True) + EPS)

    def stack(parts):
        return jnp.concatenate(parts, axis=0)

    def block_diag(x):
        return jnp.where(own_head, jnp.concatenate([x] * nh, axis=1), 0.0)

    def body(n, carry):
        base = pl.multiple_of(n * c, c)
        q = stack([l2n(conv_silu(k, base)) for k in range(nh)]) * (B_DIM ** -0.5)
        kk = stack([l2n(conv_silu(nh + k, base)) for k in range(nh)])
        v = stack([conv_silu(2 * nh + k, base) for k in range(nh)])
        gbs = [gb_ref[pl.ds(base, c), k * B_DIM:(k + 1) * B_DIM] for k in range(nh)]
        beta = stack([t[:, 0:1] for t in gbs])
        g = stack([t[:, 1:2] for t in gbs])
        gsq = jnp.broadcast_to(g, (r, r))
        gcol = _dot_01(tri, gsq)
        grow = _dot_01(ones, jnp.where(upper, gsq, 0.0))
        gcum = gcol[:, 0:1]
        glast = stack([jnp.broadcast_to(gcol[(k + 1) * c - 1:(k + 1) * c, 0:1], (c, 1)) for k in range(nh)])
        decay = jnp.exp(jnp.where(causal, gcol - grow, NEG_INF))
        kb = kk * beta
        kbh, kh, qh = kb.astype(BF16), kk.astype(BF16), q.astype(BF16)
        lower = jnp.where(strict, _dot_nt(kbh, kh) * decay, 0.0)
        tinv = eye - lower
        lp = lower
        span = 2
        while span < c:
            lp = _dot3(lp, lp)
            tinv = tinv + _dot3(tinv, lp)
            span *= 2
        eg = jnp.exp(gcum)
        u = _dot3(tinv, v * beta)
        w = _dot3(tinv, kb * eg)
        intra = jnp.where(causal, _dot_nt(qh, kh) * decay, 0.0)
        state = s_scr[...]
        sh = state.astype(BF16)
        v_new = u - _dot(block_diag(w).astype(BF16), sh)
        vh = v_new.astype(BF16)
        out = _dot(block_diag(q * eg).astype(BF16), sh) + _dot(intra.astype(BF16), vh)
        kdec = kk * jnp.exp(glast - gcum)
        cd = stack([jnp.broadcast_to(jnp.exp(gcol[(k + 1) * c - 1:(k + 1) * c, 0:1]), (B_DIM, 1)) for k in range(nh)])
        s_scr[...] = state * cd + _dot_tn(block_diag(kdec).astype(BF16), vh)
        on = out * lax.rsqrt(jnp.mean(out * out, axis=-1, keepdims=True) + EPS) * dnw
        for k in range(nh):
            zc = z_ref[pl.ds(base, c), k * B_DIM:(k + 1) * B_DIM]
            o_ref[pl.ds(base, c), k * B_DIM:(k + 1) * B_DIM] = on[k * c:(k + 1) * c] * (zc * _sigmoid(zc))
        return carry

    lax.fori_loop(0, blk // c, body, 0, unroll=2 if blk // c >= 2 else 1)
    for col in range(ncol):
        ext[col, 0:8, :] = ext[col, blk:blk + 8, :]

    @pl.when(j == pl.num_programs(1) - 1)
    def _():
        for k in range(nh):
            sf_ref[k] = s_scr[k * B_DIM:(k + 1) * B_DIM, :]


def _delta(raw, hist, conv_w, gb, z, dnw, s0, chunk, blk):
    b, seq, _ = raw.shape
    hb = B_HEADS * B_DIM
    tok = lambda w: pl.BlockSpec((None, blk, w), lambda i, j: (i, j, 0))
    st = pl.BlockSpec((None, B_HEADS, B_DIM, B_DIM), lambda i, j: (i, 0, 0, 0))
    return pl.pallas_call(
        functools.partial(_delta_kernel, blk=blk, chunk=chunk),
        grid=(b, seq // blk),
        in_specs=[tok(B_QKV), pl.BlockSpec((None, 8, B_QKV), lambda i, j: (i, 0, 0)),
                  pl.BlockSpec((CONV_WIDTH, B_QKV), lambda i, j: (0, 0)), tok(hb), tok(hb),
                  pl.BlockSpec((1, B_DIM), lambda i, j: (0, 0)), st],
        out_specs=[tok(hb), st],
        out_shape=[jax.ShapeDtypeStruct((b, seq, hb), F32), jax.ShapeDtypeStruct((b, B_HEADS, B_DIM, B_DIM), F32)],
        scratch_shapes=[pltpu.VMEM((3 * B_HEADS, blk + 8, B_DIM), F32), pltpu.VMEM((B_HEADS * B_DIM, B_DIM), F32)],
        compiler_params=_params("parallel", "arbitrary"),
        name="delta",
    )(raw, hist, conv_w, gb, z, dnw, s0)


def _proj_out_kernel(x_ref, a_ref, b_ref, wa_ref, wb_ref, nw_ref, h_ref, hnt_ref):
    h = x_ref[...] + _dot(a_ref[...].astype(BF16), wa_ref[...]) + _dot(b_ref[...].astype(BF16), wb_ref[...])
    h_ref[...] = h
    hn = h * lax.rsqrt(jnp.mean(h * h, axis=-1, keepdims=True) + EPS) * nw_ref[...]
    hnt = hn.T.astype(BF16)
    for c in range(hnt_ref.shape[0]):
        hnt_ref[c] = hnt[:, c * 128:(c + 1) * 128]


def _proj_out(x, a, b, wts, tm):
    t = x.shape[0]
    row = lambda i: (i, 0)
    full = lambda arr: pl.BlockSpec(arr.shape, lambda i: (0, 0))
    return pl.pallas_call(
        _proj_out_kernel,
        grid=(t // tm,),
        in_specs=[pl.BlockSpec((tm, D_MODEL), row), pl.BlockSpec((tm, A_WIDTH), row), pl.BlockSpec((tm, A_WIDTH), row),
                  full(wts["wo_a"]), full(wts["wo_b"]), full(wts["norm_ffn"])],
        out_specs=[pl.BlockSpec((tm, D_MODEL), row), pl.BlockSpec((tm // 128, D_MODEL, 128), lambda i: (i, 0, 0))],
        out_shape=[jax.ShapeDtypeStruct((t, D_MODEL), F32), jax.ShapeDtypeStruct((t // 128, D_MODEL, 128), BF16)],
        compiler_params=_params("parallel"),
        name="proj_out",
    )(x, a, b, wts["wo_a"], wts["wo_b"], wts["norm_ffn"])


def _top16_ranked(x):
    t = x.shape[1]
    slot = lax.broadcasted_iota(jnp.int32, (PEER_TOPK, t), 0).astype(F32)
    vals = jnp.full((PEER_TOPK, t), NEG_INF, F32)
    rank = jnp.full(x.shape, float(PEER_TOPK), F32)
    filled = jnp.zeros((1, t), F32)
    for _ in range(PEER_TOPK):
        m = jnp.max(x, axis=0, keepdims=True)
        eq = x == m
        cnt = jnp.sum(jnp.where(eq, 1.0, 0.0), axis=0, keepdims=True)
        vals = jnp.where((slot >= filled) & (slot < filled + cnt), m, vals)
        rank = jnp.where(eq, jnp.minimum(filled, float(PEER_TOPK)), rank)
        filled = filled + cnt
        x = jnp.where(eq, NEG_INF, x)
    return vals, rank


def _kth_largest(x, k):
    t = x.shape[1]
    kth = jnp.full((1, t), NEG_INF, F32)
    filled = jnp.zeros((1, t), F32)
    for _ in range(k):
        m = jnp.max(x, axis=0, keepdims=True)
        eq = x == m
        cnt = jnp.sum(jnp.where(eq, 1.0, 0.0), axis=0, keepdims=True)
        kth = jnp.where((filled < k) & (filled + cnt >= k), m, kth)
        filled = filled + cnt
        x = jnp.where(eq, NEG_INF, x)
    return kth


def _peer_select_kernel(hnt_ref, wpq_ref, sk_ref, n1_ref, q1_ref, r2_ref, q2_ref, sub_scr, top_scr, rank_scr):
    nslab = hnt_ref.shape[0]
    hnt = jnp.concatenate([hnt_ref[c] for c in range(nslab)], axis=1)

    def half(hc, carry):
        row0 = pl.multiple_of(hc * PEER_KEYS, PEER_KEYS)
        qt = _dot(wpq_ref[pl.ds(row0, PEER_KEYS), :], hnt)
        sub = _dot(sk_ref[hc & 1], qt.astype(BF16))
        sub_scr[hc] = sub
        top_scr[hc], rank_scr[hc] = _top16_ranked(sub)
        return carry

    lax.fori_loop(0, 2 * PEER_HEADS, half, 0, unroll=2)

    def slabs(ref, h, x):
        for c in range(nslab):
            ref[h, c] = x[:, c * 128:(c + 1) * 128].astype(ref.dtype)

    def head(h, carry):
        a = top_scr[2 * h]
        b = top_scr[2 * h + 1]
        pieces = [a[0:1] + b] + [a[i:i + 1] + b[0:8] for i in range(1, 8)] + [a[8:16] + b[0:1]]
        cand = jnp.concatenate(pieces, axis=0)
        tau = _kth_largest(cand, PEER_TOPK)
        top = a[0:1] + b[0:1]
        zsum = jnp.sum(jnp.where(cand >= tau, jnp.exp(cand - top), 0.0), axis=0, keepdims=True)
        cnt = jnp.zeros_like(a)
        for j in range(PEER_TOPK):
            cnt = cnt + jnp.where(a + b[j:j + 1] >= tau, 1.0, 0.0)
        rank1 = rank_scr[2 * h]
        n1 = jnp.zeros_like(rank1)
        for i in range(PEER_TOPK):
            n1 = jnp.where(rank1 == float(i), cnt[i:i + 1], n1)
        slabs(n1_ref, h, n1)
        slabs(r2_ref, h, rank_scr[2 * h + 1])
        slabs(q1_ref, h, jnp.exp(sub_scr[2 * h] - a[0:1]) / zsum)
        slabs(q2_ref, h, jnp.exp(sub_scr[2 * h + 1] - b[0:1]))
        return carry

    lax.fori_loop(0, PEER_HEADS, head, 0, unroll=2)


def _peer_select(hnt, wts, tt):
    t = hnt.shape[0] * 128
    out = pl.BlockSpec((PEER_HEADS, tt // 128, PEER_KEYS, 128), lambda i: (0, i, 0, 0))
    shape = lambda dt: jax.ShapeDtypeStruct((PEER_HEADS, t // 128, PEER_KEYS, 128), dt)
    return pl.pallas_call(
        _peer_select_kernel,
        grid=(t // tt,),
        in_specs=[pl.BlockSpec((tt // 128, D_MODEL, 128), lambda i: (i, 0, 0)),
                  pl.BlockSpec(wts["wpq_t"].shape, lambda i: (0, 0)),
                  pl.BlockSpec(wts["sub_keys"].shape, lambda i: (0, 0, 0))],
        out_specs=[out, out, out, out],
        out_shape=[shape(F32), shape(F32), shape(BF16), shape(BF16)],
        scratch_shapes=[pltpu.VMEM((2 * PEER_HEADS, PEER_KEYS, tt), F32), pltpu.VMEM((2 * PEER_HEADS, PEER_TOPK, tt), F32),
                        pltpu.VMEM((2 * PEER_HEADS, PEER_KEYS, tt), F32)],
        compiler_params=_params("parallel"),
        name="peer_select",
    )(hnt, wts["wpq_t"], wts["sub_keys"])


A_PER_TILE = 8


def _rows_bf16(row):
    return jnp.broadcast_to(row, (PEER_KEYS, 128)).astype(BF16)


def _peer_dense_kernel(hnt_ref, u_ref, vt_ref, n1_ref, q1_ref, r2_ref, q2_ref, h_ref, y_ref, acc_ref, pre_ref, w_ref,
                       r2_scr, q2_scr, *, tt):
    j = pl.program_id(1)
    nslab = tt // 128

    @pl.when(j == 0)
    def _():
        acc_ref[...] = jnp.zeros_like(acc_ref)
        r2_scr[...] = r2_ref[...]
        q2_scr[...] = q2_ref[...]

    per_sub = min(nslab, 2)
    nsub = nslab // per_sub
    zero = jnp.zeros((PEER_KEYS, 128), BF16)
    group = 2

    def mm_u(s):
        hn = jnp.concatenate([hnt_ref[s * per_sub + c] for c in range(per_sub)], axis=1)
        pre = _dot(u_ref[...], hn).astype(BF16)
        for c in range(per_sub):
            pre_ref[s * per_sub + c] = pre[:, c * 128:(c + 1) * 128]

    def build(tc, carry):
        for ia0 in range(0, A_PER_TILE, group):
            gates = [zero] * group
            for h in range(PEER_HEADS):
                r2 = r2_scr[h, tc]
                q2 = q2_scr[h, tc]
                for k in range(group):
                    n1 = _rows_bf16(n1_ref[h, tc, ia0 + k:ia0 + k + 1, :])
                    q1 = _rows_bf16(q1_ref[h, tc, ia0 + k:ia0 + k + 1, :])
                    gates[k] = gates[k] + jnp.where(r2 < n1, q2, zero) * q1
            for k in range(group):
                rows = slice((ia0 + k) * PEER_KEYS, (ia0 + k + 1) * PEER_KEYS)
                p = pre_ref[tc, rows, :]
                act = (0.5 * p) * (1.0 + lax.erf(p * (1.0 / math.sqrt(2.0))))
                w_ref[tc, rows, :] = gates[k] * act
        return carry

    def mm_v(s):
        w = jnp.concatenate([w_ref[s * per_sub + c] for c in range(per_sub)], axis=1)
        out = _dot(vt_ref[...], w)
        for c in range(per_sub):
            acc_ref[s * per_sub + c] += out[:, c * 128:(c + 1) * 128]

    for s in range(nsub):
        mm_u(s)
    lax.fori_loop(0, nslab, build, 0)
    for s in range(nsub):
        mm_v(s)

    @pl.when(j == pl.num_programs(1) - 1)
    def _():
        for c in range(nslab):
            y_ref[c * 128:(c + 1) * 128, :] = h_ref[c * 128:(c + 1) * 128, :] + acc_ref[c].T


def _peer_dense(hnt, h, sel, wts, tt):
    t = hnt.shape[0] * 128
    n1, q1, r2, q2 = sel
    et = A_PER_TILE * PEER_KEYS
    nslab = tt // 128
    a_spec = pl.BlockSpec((PEER_HEADS, nslab, A_PER_TILE, 128), lambda i, j: (0, i, j, 0))
    b_spec = pl.BlockSpec((PEER_HEADS, nslab, PEER_KEYS, 128), lambda i, j: (0, i, 0, 0))
    return pl.pallas_call(
        functools.partial(_peer_dense_kernel, tt=tt),
        grid=(t // tt, PEER_KEYS // A_PER_TILE),
        in_specs=[pl.BlockSpec((nslab, D_MODEL, 128), lambda i, j: (i, 0, 0)),
                  pl.BlockSpec((et, D_MODEL), lambda i, j: (j, 0)),
                  pl.BlockSpec((D_MODEL, et), lambda i, j: (0, j)),
                  a_spec, a_spec, b_spec, b_spec,
                  pl.BlockSpec((tt, D_MODEL), lambda i, j: (i, 0))],
        out_specs=pl.BlockSpec((tt, D_MODEL), lambda i, j: (i, 0)),
        out_shape=jax.ShapeDtypeStruct((t, D_MODEL), F32),
        scratch_shapes=[pltpu.VMEM((nslab, D_MODEL, 128), F32), pltpu.VMEM((nslab, et, 128), BF16),
                        pltpu.VMEM((nslab, et, 128), BF16),
                        pltpu.VMEM((PEER_HEADS, nslab, PEER_KEYS, 128), BF16),
                        pltpu.VMEM((PEER_HEADS, nslab, PEER_KEYS, 128), BF16)],
        compiler_params=_params("parallel", "arbitrary"),
        name="peer_dense",
    )(hnt, wts["u"], wts["v_t"], n1, q1, r2, q2, h)


def _rope_tables(pos, reps):
    half = A_HEAD_DIM // 2
    inv_freq = ROPE_THETA ** (-jnp.arange(half, dtype=F32) / half)
    ang = pos.astype(F32)[:, None] * inv_freq[None, :]
    cos = jnp.cos(ang)
    sin = jnp.sin(ang)
    cos_t = jnp.tile(jnp.concatenate([cos, cos], axis=-1), (reps, A_HEADS))
    sin_t = jnp.tile(jnp.concatenate([-sin, sin], axis=-1), (reps, A_HEADS))
    return cos_t, sin_t


def _prep_weights(norm_mix_w, w_in, q_norm_w, k_norm_w, a_log, dt_bias, w_out, norm_ffn_w, w_pq, sub_keys, expert_u, expert_v):
    na = 3 * A_WIDTH
    hb = B_HEADS * B_DIM
    wgate = w_in[:, na + B_QKV + hb:]
    wg = jnp.zeros((D_MODEL, B_HEADS, B_DIM), F32)
    wg = wg.at[:, :, 0].set(wgate[:, :B_HEADS]).at[:, :, 1].set(wgate[:, B_HEADS:])
    lane_head = jnp.arange(A_WIDTH) // A_HEAD_DIM
    put1 = lambda vec: jnp.zeros((B_HEADS, B_DIM), F32).at[:, 1].set(vec).reshape(1, hb)
    return {
        "norm_mix": norm_mix_w.reshape(1, D_MODEL),
        "wa": w_in[:, :na].astype(BF16),
        "wb": w_in[:, na:na + B_QKV].astype(BF16),
        "wz": w_in[:, na + B_QKV:na + B_QKV + hb].astype(BF16),
        "wg": wg.reshape(D_MODEL, hb).astype(BF16),
        "qnw": jnp.tile(q_norm_w, A_HEADS).reshape(1, A_WIDTH),
        "knw": jnp.tile(k_norm_w, A_HEADS).reshape(1, A_WIDTH),
        "headmat": (lane_head[:, None] == lane_head[None, :]).astype(BF16),
        "alog": put1(a_log),
        "dtb": put1(dt_bias),
        "wo_a": w_out[:A_WIDTH].astype(BF16),
        "wo_b": w_out[A_WIDTH:].astype(BF16),
        "norm_ffn": norm_ffn_w.reshape(1, D_MODEL),
        "wpq_t": w_pq.T.astype(BF16),
        "sub_keys": sub_keys.astype(BF16),
        "u": expert_u.astype(BF16),
        "v_t": expert_v.T.astype(BF16),
    }


def _layer(x, pos_cos, pos_sin, pos_blocks, wts, conv_w, dnw, mixer_a, hist, s0, chunk, tm, tt_sel, tt_dense):
    b, t, _ = x.shape
    xf = x.reshape(b * t, D_MODEL)
    q, k, v, raw, z, gb = _proj_in(xf, pos_cos, pos_sin, pos_blocks, wts, tm)
    a_out, k_buf, v_buf = mixer_a(q.reshape(b, t, A_WIDTH), k.reshape(b, t, A_WIDTH), v.reshape(b, t, A_WIDTH))
    raw = raw.reshape(b, t, B_QKV)
    conv_state = raw[:, t - (CONV_WIDTH - 1):]
    pad = (-t) % chunk
    pad3 = lambda arr: jnp.pad(arr.reshape(b, t, -1), ((0, 0), (0, pad), (0, 0)))
    b_out, s_fin = _delta(pad3(raw), hist, conv_w, pad3(gb), pad3(z), dnw, s0, chunk, min(512, t + pad))
    b_out = b_out[:, :t].reshape(b * t, B_HEADS * B_DIM)
    h, hnt = _proj_out(xf, a_out.reshape(b * t, A_WIDTH), b_out, wts, tm)
    sel = _peer_select(hnt, wts, tt_sel)
    y = _peer_dense(hnt, h, sel, wts, tt_dense)
    return y.reshape(b, t, D_MODEL), k_buf, v_buf, conv_state, s_fin


def kernel(x_prompt, x_sample, cache_k_win, cache_v_win, state_conv, state_delta, norm_mix_w, w_in, q_norm_w, k_norm_w, conv_w, a_log, dt_bias, delta_norm_w, w_out, norm_ffn_w, w_pq, sub_keys, expert_u, expert_v):
    assert w_in.shape[0] == 1, "one layer"
    bp, sp, _ = x_prompt.shape
    bs, ss, _ = x_sample.shape
    wts = _prep_weights(norm_mix_w[0], w_in[0], q_norm_w[0], k_norm_w[0], a_log[0], dt_bias[0], w_out[0], norm_ffn_w[0],
                        w_pq[0], sub_keys[0], expert_u[0], expert_v[0])
    cw = conv_w[0]
    dnw = delta_norm_w[0].reshape(1, B_DIM)
    lane_head = jnp.arange(A_WIDTH) // A_HEAD_DIM
    head_sum = (lane_head[:, None] == jnp.arange(128)[None, :]).astype(F32)
    head_expand = head_sum.T

    tm_p = 512
    cos_p, sin_p = _rope_tables(jnp.arange(sp), 1)

    def mixer_prompt(q, k, v):
        return _attn_prompt(q, k, v), k, v

    yp, kp, vp, cp, dp = _layer(
        x_prompt, cos_p, sin_p, sp // tm_p, wts, cw, dnw, mixer_prompt,
        jnp.zeros((bp, 8, B_QKV), F32), jnp.zeros((bp, B_HEADS, B_DIM, B_DIM), F32),
        chunk=64, tm=tm_p, tt_sel=256, tt_dense=1024)

    n_s = bs * ss
    cos_s, sin_s = _rope_tables(PAST_LEN + jnp.arange(ss), bs)

    def mixer_sample(q, k, v):
        kc = cache_k_win[0].reshape(bs, -1, A_WIDTH)
        vc = cache_v_win[0].reshape(bs, -1, A_WIDTH)
        return _attn_sample(q, k, v, kc, vc, head_sum, head_expand)

    hist_s = jnp.pad(state_conv[0], ((0, 0), (8 - (CONV_WIDTH - 1), 0), (0, 0)))
    ys, ks, vs, cs, ds = _layer(
        x_sample, cos_s, sin_s, 1, wts, cw, dnw, mixer_sample, hist_s, state_delta[0],
        chunk=8, tm=n_s, tt_sel=n_s, tt_dense=n_s)

    win_p = min(BAND * max(DILATIONS), sp)
    shape_kv = lambda arr, b: arr.reshape(1, b, -1, A_HEADS, A_HEAD_DIM)
    return (yp, ys,
            shape_kv(kp[:, sp - win_p:], bp), shape_kv(vp[:, sp - win_p:], bp), cp[None], dp[None],
            shape_kv(ks, bs), shape_kv(vs, bs), cs[None], ds[None])
```

```python
import functools
import math

import jax
import jax.numpy as jnp
from jax import lax
from jax.experimental import pallas as pl
from jax.experimental.pallas import tpu as pltpu

F32 = jnp.float32
BF16 = jnp.bfloat16

D_MODEL = 1024
A_HEADS = 8
A_HEAD_DIM = 64
A_WIDTH = A_HEADS * A_HEAD_DIM
DILATIONS = (1, 4, 16)
BAND = 128
ROPE_THETA = 10000.0
PAST_LEN = 8192
B_HEADS = 4
B_DIM = 128
B_QKV = 3 * B_HEADS * B_DIM
CONV_WIDTH = 4
PEER_HEADS = 8
PEER_KEYS = 128
PEER_TOPK = 16
EPS = 1e-6
NEG_INF = float("-inf")
VMEM_LIMIT = 56 * 1024 * 1024


def _params(*sem):
    return pltpu.CompilerParams(dimension_semantics=sem, vmem_limit_bytes=VMEM_LIMIT)


def _dot(a, b, **kw):
    return jnp.dot(a, b, preferred_element_type=F32, **kw)


def _dot_nt(a, b, **kw):
    return lax.dot_general(a, b, (((1,), (1,)), ((), ())), preferred_element_type=F32, **kw)


def _dot_tn(a, b, **kw):
    return lax.dot_general(a, b, (((0,), (0,)), ((), ())), preferred_element_type=F32, **kw)


def _dot_x01(a, b01):
    b = b01.astype(BF16)
    a1 = a.astype(BF16)
    r1 = a - a1.astype(F32)
    a2 = r1.astype(BF16)
    a3 = (r1 - a2.astype(F32)).astype(BF16)
    return _dot(a1, b) + (_dot(a2, b) + _dot(a3, b))


def _sigmoid(x):
    return 1.0 / (1.0 + jnp.exp(-x))


def _proj_in_kernel(x_ref, nw_ref, wa_ref, wb_ref, wz_ref, wg_ref, qnw_ref, knw_ref, cos_ref, sin_ref,
                    hm_ref, alog_ref, dtb_ref, q_ref, k_ref, v_ref, raw_ref, z_ref, gb_ref):
    x = x_ref[...]
    ms = jnp.mean(x * x, axis=-1, keepdims=True)
    xn = (x * lax.rsqrt(ms + EPS) * nw_ref[...]).astype(BF16)
    a = _dot(xn, wa_ref[...])
    hm = hm_ref[...]
    lane = lax.broadcasted_iota(jnp.int32, (1, A_WIDTH), 1)
    first_half = (lane & (A_HEAD_DIM // 2)) == 0
    cos = cos_ref[...]
    sin = sin_ref[...]

    def norm_rope(t, w):
        t2 = t * t
        hi = t2.astype(BF16)
        lo = (t2 - hi.astype(F32)).astype(BF16)
        ss = _dot(hi, hm) + _dot(lo, hm)
        tn = t * lax.rsqrt(ss * (1.0 / A_HEAD_DIM) + EPS) * w
        partner = jnp.where(first_half, pltpu.roll(tn, A_WIDTH - A_HEAD_DIM // 2, 1), pltpu.roll(tn, A_HEAD_DIM // 2, 1))
        return tn * cos + partner * sin

    q_ref[...] = norm_rope(a[:, :A_WIDTH], qnw_ref[...])
    k_ref[...] = norm_rope(a[:, A_WIDTH:2 * A_WIDTH], knw_ref[...])
    v_ref[...] = a[:, 2 * A_WIDTH:]
    raw_ref[...] = _dot(xn, wb_ref[...])
    z_ref[...] = _dot(xn, wz_ref[...])
    zg = _dot(xn, wg_ref[...])
    beta = _sigmoid(zg)
    t = zg + dtb_ref[...]
    softplus = jnp.maximum(t, 0.0) + jnp.log1p(jnp.exp(-jnp.abs(t)))
    g = -jnp.exp(alog_ref[...]) * softplus
    glane = lax.broadcasted_iota(jnp.int32, (1, B_HEADS * B_DIM), 1) & (B_DIM - 1)
    gb_ref[...] = jnp.where(glane == 0, beta, jnp.where(glane == 1, g, 0.0))


def _proj_in(x, pos_cos, pos_sin, pos_blocks, wts, tm):
    t = x.shape[0]
    n = t // tm
    row = lambda i: (i, 0)
    const = lambda i: (0, 0)
    posmap = lambda i: (i % pos_blocks, 0)
    full = lambda a: pl.BlockSpec(a.shape, const)
    outs = [(A_WIDTH, "q"), (A_WIDTH, "k"), (A_WIDTH, "v"), (B_QKV, "raw"), (B_HEADS * B_DIM, "z"), (B_HEADS * B_DIM, "gb")]
    return pl.pallas_call(
        _proj_in_kernel,
        grid=(n,),
        in_specs=[pl.BlockSpec((tm, D_MODEL), row), full(wts["norm_mix"]), full(wts["wa"]), full(wts["wb"]), full(wts["wz"]),
                  full(wts["wg"]), full(wts["qnw"]), full(wts["knw"]),
                  pl.BlockSpec((tm, A_WIDTH), posmap), pl.BlockSpec((tm, A_WIDTH), posmap),
                  full(wts["headmat"]), full(wts["alog"]), full(wts["dtb"])],
        out_specs=[pl.BlockSpec((tm, w), row) for w, _ in outs],
        out_shape=[jax.ShapeDtypeStruct((t, w), F32) for w, _ in outs],
        compiler_params=_params("parallel"),
        name="proj_in",
    )(x, wts["norm_mix"], wts["wa"], wts["wb"], wts["wz"], wts["wg"], wts["qnw"], wts["knw"], pos_cos, pos_sin,
      wts["headmat"], wts["alog"], wts["dtb"])


def _attn_prompt_kernel(q_ref, k_ref, v_ref, o_ref, ob_ref, lb_ref, *, seq):
    lane = lax.broadcasted_iota(jnp.int32, (BAND, 2 * A_HEAD_DIM), 1)
    head0 = lane < A_HEAD_DIM
    qi = lax.broadcasted_iota(jnp.int32, (BAND, 2 * BAND), 0)
    kj = lax.broadcasted_iota(jnp.int32, (BAND, 2 * BAND), 1)
    dist = BAND + qi - kj
    in_band = (dist >= 0) & (dist <= BAND)
    scale = 1.0 / math.sqrt(A_HEAD_DIM)

    for br, dil in enumerate(DILATIONS):
        nblk = seq // (BAND * dil)
        shift = dil.bit_length() - 1

        def rows(ref, start):
            if dil == 1:
                return ref[pl.ds(start, BAND), :]
            return ref[pl.ds(start, BAND, stride=dil), :]

        def load(u):
            r = u & (dil - 1)
            n = u >> shift
            start = r + n * (BAND * dil)
            qb = rows(q_ref, start)
            k_own = rows(k_ref, start).astype(BF16)
            v_own = rows(v_ref, start).astype(BF16)
            if nblk > 1:
                pstart = jnp.maximum(start - BAND * dil, r)
                kk = jnp.concatenate([rows(k_ref, pstart).astype(BF16), k_own], axis=0)
                vv = jnp.concatenate([rows(v_ref, pstart).astype(BF16), v_own], axis=0)
                mask = in_band & (kj >= jnp.where(n > 0, 0, BAND))
            else:
                kk = jnp.concatenate([k_own, k_own], axis=0)
                vv = jnp.concatenate([v_own, v_own], axis=0)
                mask = in_band & (kj >= BAND)
            return start, qb, kk, vv, mask

        def attend(qb, kk, vv, mask):
            outs = []
            lses = []
            for hmask in (head0, ~head0):
                qh = jnp.where(hmask, qb, 0.0).astype(BF16)
                s = _dot_nt(qh, kk) * scale
                s = jnp.where(mask, s, NEG_INF)
                mx = jnp.max(s, axis=-1, keepdims=True)
                p = jnp.exp(s - mx)
                l = jnp.sum(p, axis=-1, keepdims=True)
                outs.append(_dot(p.astype(BF16), vv) / l)
                lses.append(mx + jnp.log(l))
            return jnp.where(head0, outs[0], outs[1]), jnp.where(head0, lses[0], lses[1])

        per_group = 8

        def group(g, carry):
            loaded = [load(g * per_group + i) for i in range(per_group)]
            results = [attend(*item[1:]) for item in loaded]
            for (start, *_), (out, lse) in zip(loaded, results):
                if dil == 1:
                    ob_ref[br, pl.ds(start, BAND), :] = out
                    lb_ref[br, pl.ds(start, BAND), :] = lse
                else:
                    ob_ref[br, pl.ds(start, BAND, stride=dil), :] = out
                    lb_ref[br, pl.ds(start, BAND, stride=dil), :] = lse
            return carry

        lax.fori_loop(0, seq // BAND // per_group, group, 0)

    def merge(c, carry):
        rs = pl.ds(pl.multiple_of(c * 256, 256), 256)
        l0, l1, l2 = lb_ref[0, rs, :], lb_ref[1, rs, :], lb_ref[2, rs, :]
        m = jnp.maximum(jnp.maximum(l0, l1), l2)
        w0, w1, w2 = jnp.exp(l0 - m), jnp.exp(l1 - m), jnp.exp(l2 - m)
        o_ref[rs, :] = (w0 * ob_ref[0, rs, :] + w1 * ob_ref[1, rs, :] + w2 * ob_ref[2, rs, :]) / (w0 + w1 + w2)
        return carry

    lax.fori_loop(0, seq // 256, merge, 0)


def _attn_prompt(q, k, v):
    b, s, _ = q.shape
    spec = pl.BlockSpec((None, s, 2 * A_HEAD_DIM), lambda i, j: (i, 0, j))
    return pl.pallas_call(
        functools.partial(_attn_prompt_kernel, seq=s),
        grid=(b, A_HEADS // 2),
        in_specs=[spec, spec, spec],
        out_specs=spec,
        out_shape=jax.ShapeDtypeStruct((b, s, A_WIDTH), F32),
        scratch_shapes=[pltpu.VMEM((len(DILATIONS), s, 2 * A_HEAD_DIM), F32),
                        pltpu.VMEM((len(DILATIONS), s, 2 * A_HEAD_DIM), F32)],
        compiler_params=_params("parallel", "parallel"),
        name="attn_prompt",
    )(q, k, v)


SAMPLE_PAD = 16


def _attn_sample_kernel(q_ref, kn_ref, vn_ref, kc_ref, vc_ref, hs_ref, he_ref, o_ref, ko_ref, vo_ref, kbuf, vbuf,
                        *, win, new):
    nlb = A_WIDTH // 128
    for buf, cache, fresh, shifted in ((kbuf, kc_ref, kn_ref, ko_ref), (vbuf, vc_ref, vn_ref, vo_ref)):
        for j in range(nlb):
            ls = slice(j * 128, (j + 1) * 128)
            buf[j, 0:win, :] = cache[:, ls]
            buf[j, win:win + new, :] = fresh[:, ls]
            buf[j, win + new:win + SAMPLE_PAD, :] = jnp.zeros((SAMPLE_PAD - new, 128), F32)
            shifted[:, ls] = buf[j, new:new + win, :]

    def gather(buf, start, size, stride):
        if stride == 1:
            return jnp.concatenate([buf[j, pl.ds(start, size), :] for j in range(nlb)], axis=1)
        return jnp.concatenate([buf[j, pl.ds(start, size, stride=stride), :] for j in range(nlb)], axis=1)

    hs = hs_ref[...]
    he = he_ref[...]
    scale = 1.0 / math.sqrt(A_HEAD_DIM)
    for l in range(new):
        ql = q_ref[l:l + 1, :]
        outs, lses = [], []
        for dil in DILATIONS:
            start = win + l - BAND * dil
            ks = gather(kbuf, start, BAND, dil)
            vs = gather(vbuf, start, BAND, dil)
            kself = gather(kbuf, win + l, 8, 1)
            vself = gather(vbuf, win + l, 1, 1)
            s_all = _dot_x01(jnp.concatenate([ks, kself], axis=0) * ql, hs) * scale
            s = s_all[:BAND]
            sself = s_all[BAND:BAND + 1]
            mx = jnp.maximum(jnp.max(s, axis=0, keepdims=True), sself)
            p = jnp.exp(s - mx)
            pself = jnp.exp(sself - mx)
            lsum = jnp.sum(p, axis=0, keepdims=True) + pself
            pe = _dot_x01(p, he)
            rows8 = jnp.concatenate([pself, lsum, mx + jnp.log(lsum), jnp.zeros((5, 128), F32)], axis=0)
            exp8 = _dot_x01(rows8, he)
            o = jnp.sum(pe * vs, axis=0, keepdims=True) + exp8[0:1] * vself
            outs.append(o / exp8[1:2])
            lses.append(exp8[2:3])
        m = jnp.maximum(jnp.maximum(lses[0], lses[1]), lses[2])
        w = [jnp.exp(ls - m) for ls in lses]
        o_ref[l:l + 1, :] = (w[0] * outs[0] + w[1] * outs[1] + w[2] * outs[2]) / (w[0] + w[1] + w[2])


def _attn_sample(q, kn, vn, kc, vc, hs, he):
    b, new, _ = q.shape
    win = kc.shape[1]
    assert win >= BAND * max(DILATIONS)
    small = pl.BlockSpec((None, new, A_WIDTH), lambda i: (i, 0, 0))
    big = pl.BlockSpec((None, win, A_WIDTH), lambda i: (i, 0, 0))
    const = lambda a: pl.BlockSpec(a.shape, lambda i: (0, 0))
    return pl.pallas_call(
        functools.partial(_attn_sample_kernel, win=win, new=new),
        grid=(b,),
        in_specs=[small, small, small, big, big, const(hs), const(he)],
        out_specs=[small, big, big],
        out_shape=[jax.ShapeDtypeStruct((b, new, A_WIDTH), F32), jax.ShapeDtypeStruct((b, win, A_WIDTH), F32),
                   jax.ShapeDtypeStruct((b, win, A_WIDTH), F32)],
        scratch_shapes=[pltpu.VMEM((A_WIDTH // 128, win + SAMPLE_PAD, 128), F32),
                        pltpu.VMEM((A_WIDTH // 128, win + SAMPLE_PAD, 128), F32)],
        compiler_params=_params("parallel"),
        name="attn_sample",
    )(q, kn, vn, kc, vc, hs, he)


def _dot3(a, b):
    ah = a.astype(BF16)
    al = (a - ah.astype(F32)).astype(BF16)
    bh = b.astype(BF16)
    bl = (b - bh.astype(F32)).astype(BF16)
    return _dot(ah, bh) + (_dot(ah, bl) + _dot(al, bh))


def _dot_01(a01, b):
    a = a01.astype(BF16)
    b1 = b.astype(BF16)
    r1 = b - b1.astype(F32)
    b2 = r1.astype(BF16)
    b3 = (r1 - b2.astype(F32)).astype(BF16)
    return _dot(a, b1) + (_dot(a, b2) + _dot(a, b3))


def _delta_kernel(raw_ref, hist_ref, cw_ref, gb_ref, z_ref, dnw_ref, s0_ref, o_ref, sf_ref, ext, s_scr, *, blk, chunk):
    c = chunk
    nh = B_HEADS
    r = nh * c
    ncol = 3 * nh
    shift = c.bit_length() - 1
    j = pl.program_id(1)

    @pl.when(j == 0)
    def _():
        for col in range(ncol):
            ext[col, 0:8, :] = hist_ref[:, col * B_DIM:(col + 1) * B_DIM]
        for k in range(nh):
            s_scr[k * B_DIM:(k + 1) * B_DIM, :] = s0_ref[k]

    for col in range(ncol):
        ext[col, 8:8 + blk, :] = raw_ref[:, col * B_DIM:(col + 1) * B_DIM]

    ii = lax.broadcasted_iota(jnp.int32, (r, r), 0)
    jj = lax.broadcasted_iota(jnp.int32, (r, r), 1)
    same = (ii >> shift) == (jj >> shift)
    causal = same & (ii >= jj)
    strict = same & (ii > jj)
    upper = same & (ii <= jj)
    eye = (ii == jj).astype(F32)
    tri = causal.astype(F32)
    ones = jnp.ones((r, r), F32)
    row_head = lax.broadcasted_iota(jnp.int32, (r, nh * B_DIM), 0) >> shift
    lane_head = lax.broadcasted_iota(jnp.int32, (r, nh * B_DIM), 1) >> (B_DIM.bit_length() - 1)
    own_head = row_head == lane_head
    dnw = dnw_ref[...]

    def conv_silu(col, base):
        y = cw_ref[0:1, col * B_DIM:(col + 1) * B_DIM] * ext[col, pl.ds(base + 5, c), :]
        for t in range(1, CONV_WIDTH):
            y = y + cw_ref[t:t + 1, col * B_DIM:(col + 1) * B_DIM] * ext[col, pl.ds(base + 5 + t, c), :]
        return y * _sigmoid(y)

    def l2n(t):
        return t * lax.rsqrt(jnp.sum(t * t, axis=-1, keepdims=True) + EPS)

    def stack(parts):
        return jnp.concatenate(parts, axis=0)

    def block_diag(x):
        return jnp.where(own_head, jnp.concatenate([x] * nh, axis=1), 0.0)

    def body(n, carry):
        base = pl.multiple_of(n * c, c)
        q = stack([l2n(conv_silu(k, base)) for k in range(nh)]) * (B_DIM ** -0.5)
        kk = stack([l2n(conv_silu(nh + k, base)) for k in range(nh)])
        v = stack([conv_silu(2 * nh + k, base) for k in range(nh)])
        gbs = [gb_ref[pl.ds(base, c), k * B_DIM:(k + 1) * B_DIM] for k in range(nh)]
        beta = stack([t[:, 0:1] for t in gbs])
        g = stack([t[:, 1:2] for t in gbs])
        gsq = jnp.broadcast_to(g, (r, r))
        gcol = _dot_01(tri, gsq)
        grow = _dot_01(ones, jnp.where(upper, gsq, 0.0))
        gcum = gcol[:, 0:1]
        glast = stack([jnp.broadcast_to(gcol[(k + 1) * c - 1:(k + 1) * c, 0:1], (c, 1)) for k in range(nh)])
        decay = jnp.exp(jnp.where(causal, gcol - grow, NEG_INF))
        kb = kk * beta
        kbh, kh, qh = kb.astype(BF16), kk.astype(BF16), q.astype(BF16)
        lower = jnp.where(strict, _dot_nt(kbh, kh) * decay, 0.0)
        tinv = eye - lower
        lp = lower
        span = 2
        while span < c:
            lp = _dot3(lp, lp)
            tinv = tinv + _dot3(tinv, lp)
            span *= 2
        eg = jnp.exp(gcum)
        u = _dot3(tinv, v * beta)
        w = _dot3(tinv, kb * eg)
        intra = jnp.where(causal, _dot_nt(qh, kh) * decay, 0.0)
        state = s_scr[...]
        sh = state.astype(BF16)
        v_new = u - _dot(block_diag(w).astype(BF16), sh)
        vh = v_new.astype(BF16)
        out = _dot(block_diag(q * eg).astype(BF16), sh) + _dot(intra.astype(BF16), vh)
        kdec = kk * jnp.exp(glast - gcum)
        cd = stack([jnp.broadcast_to(jnp.exp(gcol[(k + 1) * c - 1:(k + 1) * c, 0:1]), (B_DIM, 1)) for k in range(nh)])
        s_scr[...] = state * cd + _dot_tn(block_diag(kdec).astype(BF16), vh)
        on = out * lax.rsqrt(jnp.mean(out * out, axis=-1, keepdims=True) + EPS) * dnw
        for k in range(nh):
            zc = z_ref[pl.ds(base, c), k * B_DIM:(k + 1) * B_DIM]
            o_ref[pl.ds(base, c), k * B_DIM:(k + 1) * B_DIM] = on[k * c:(k + 1) * c] * (zc * _sigmoid(zc))
        return carry

    lax.fori_loop(0, blk // c, body, 0, unroll=2 if blk // c >= 2 else 1)
    for col in range(ncol):
        ext[col, 0:8, :] = ext[col, blk:blk + 8, :]

    @pl.when(j == pl.num_programs(1) - 1)
    def _():
        for k in range(nh):
            sf_ref[k] = s_scr[k * B_DIM:(k + 1) * B_DIM, :]


def _delta(raw, hist, conv_w, gb, z, dnw, s0, chunk, blk):
    b, seq, _ = raw.shape
    hb = B_HEADS * B_DIM
    tok = lambda w: pl.BlockSpec((None, blk, w), lambda i, j: (i, j, 0))
    st = pl.BlockSpec((None, B_HEADS, B_DIM, B_DIM), lambda i, j: (i, 0, 0, 0))
    return pl.pallas_call(
        functools.partial(_delta_kernel, blk=blk, chunk=chunk),
        grid=(b, seq // blk),
        in_specs=[tok(B_QKV), pl.BlockSpec((None, 8, B_QKV), lambda i, j: (i, 0, 0)),
                  pl.BlockSpec((CONV_WIDTH, B_QKV), lambda i, j: (0, 0)), tok(hb), tok(hb),
                  pl.BlockSpec((1, B_DIM), lambda i, j: (0, 0)), st],
        out_specs=[tok(hb), st],
        out_shape=[jax.ShapeDtypeStruct((b, seq, hb), F32), jax.ShapeDtypeStruct((b, B_HEADS, B_DIM, B_DIM), F32)],
        scratch_shapes=[pltpu.VMEM((3 * B_HEADS, blk + 8, B_DIM), F32), pltpu.VMEM((B_HEADS * B_DIM, B_DIM), F32)],
        compiler_params=_params("parallel", "arbitrary"),
        name="delta",
    )(raw, hist, conv_w, gb, z, dnw, s0)


def _proj_out_kernel(x_ref, a_ref, b_ref, wa_ref, wb_ref, nw_ref, h_ref, hnt_ref):
    h = x_ref[...] + _dot(a_ref[...].astype(BF16), wa_ref[...]) + _dot(b_ref[...].astype(BF16), wb_ref[...])
    h_ref[...] = h
    hn = h * lax.rsqrt(jnp.mean(h * h, axis=-1, keepdims=True) + EPS) * nw_ref[...]
    hnt = hn.T.astype(BF16)
    for c in range(hnt_ref.shape[0]):
        hnt_ref[c] = hnt[:, c * 128:(c + 1) * 128]


def _proj_out(x, a, b, wts, tm):
    t = x.shape[0]
    row = lambda i: (i, 0)
    full = lambda arr: pl.BlockSpec(arr.shape, lambda i: (0, 0))
    return pl.pallas_call(
        _proj_out_kernel,
        grid=(t // tm,),
        in_specs=[pl.BlockSpec((tm, D_MODEL), row), pl.BlockSpec((tm, A_WIDTH), row), pl.BlockSpec((tm, A_WIDTH), row),
                  full(wts["wo_a"]), full(wts["wo_b"]), full(wts["norm_ffn"])],
        out_specs=[pl.BlockSpec((tm, D_MODEL), row), pl.BlockSpec((tm // 128, D_MODEL, 128), lambda i: (i, 0, 0))],
        out_shape=[jax.ShapeDtypeStruct((t, D_MODEL), F32), jax.ShapeDtypeStruct((t // 128, D_MODEL, 128), BF16)],
        compiler_params=_params("parallel"),
        name="proj_out",
    )(x, a, b, wts["wo_a"], wts["wo_b"], wts["norm_ffn"])


def _top16_ranked(x):
    t = x.shape[1]
    slot = lax.broadcasted_iota(jnp.int32, (PEER_TOPK, t), 0).astype(F32)
    vals = jnp.full((PEER_TOPK, t), NEG_INF, F32)
    rank = jnp.full(x.shape, float(PEER_TOPK), F32)
    filled = jnp.zeros((1, t), F32)
    for _ in range(PEER_TOPK):
        m = jnp.max(x, axis=0, keepdims=True)
        eq = x == m
        cnt = jnp.sum(jnp.where(eq, 1.0, 0.0), axis=0, keepdims=True)
        vals = jnp.where((slot >= filled) & (slot < filled + cnt), m, vals)
        rank = jnp.where(eq, jnp.minimum(filled, float(PEER_TOPK)), rank)
        filled = filled + cnt
        x = jnp.where(eq, NEG_INF, x)
    return vals, rank


def _kth_largest(x, k):
    t = x.shape[1]
    kth = jnp.full((1, t), NEG_INF, F32)
    filled = jnp.zeros((1, t), F32)
    for _ in range(k):
        m = jnp.max(x, axis=0, keepdims=True)
        eq = x == m
        cnt = jnp.sum(jnp.where(eq, 1.0, 0.0), axis=0, keepdims=True)
        kth = jnp.where((filled < k) & (filled + cnt >= k), m, kth)
        filled = filled + cnt
        x = jnp.where(eq, NEG_INF, x)
    return kth


def _peer_select_kernel(hnt_ref, wpq_ref, sk_ref, n1_ref, q1_ref, r2_ref, q2_ref, sub_scr, top_scr, rank_scr):
    nslab = hnt_ref.shape[0]
    hnt = jnp.concatenate([hnt_ref[c] for c in range(nslab)], axis=1)

    def half(hc, carry):
        row0 = pl.multiple_of(hc * PEER_KEYS, PEER_KEYS)
        qt = _dot(wpq_ref[pl.ds(row0, PEER_KEYS), :], hnt)
        sub = _dot(sk_ref[hc & 1], qt.astype(BF16))
        sub_scr[hc] = sub
        top_scr[hc], rank_scr[hc] = _top16_ranked(sub)
        return carry

    lax.fori_loop(0, 2 * PEER_HEADS, half, 0, unroll=2)

    def slabs(ref, h, x):
        for c in range(nslab):
            ref[h, c] = x[:, c * 128:(c + 1) * 128].astype(ref.dtype)

    def head(h, carry):
        a = top_scr[2 * h]
        b = top_scr[2 * h + 1]
        pieces = [a[0:1] + b] + [a[i:i + 1] + b[0:8] for i in range(1, 8)] + [a[8:16] + b[0:1]]
        cand = jnp.concatenate(pieces, axis=0)
        tau = _kth_largest(cand, PEER_TOPK)
        top = a[0:1] + b[0:1]
        zsum = jnp.sum(jnp.where(cand >= tau, jnp.exp(cand - top), 0.0), axis=0, keepdims=True)
        cnt = jnp.zeros_like(a)
        for j in range(PEER_TOPK):
            cnt = cnt + jnp.where(a + b[j:j + 1] >= tau, 1.0, 0.0)
        rank1 = rank_scr[2 * h]
        n1 = jnp.zeros_like(rank1)
        for i in range(PEER_TOPK):
            n1 = jnp.where(rank1 == float(i), cnt[i:i + 1], n1)
        slabs(n1_ref, h, n1)
        slabs(r2_ref, h, rank_scr[2 * h + 1])
        slabs(q1_ref, h, jnp.exp(sub_scr[2 * h] - a[0:1]) / zsum)
        slabs(q2_ref, h, jnp.exp(sub_scr[2 * h + 1] - b[0:1]))
        return carry

    lax.fori_loop(0, PEER_HEADS, head, 0, unroll=2)


def _peer_select(hnt, wts, tt):
    t = hnt.shape[0] * 128
    out = pl.BlockSpec((PEER_HEADS, tt // 128, PEER_KEYS, 128), lambda i: (0, i, 0, 0))
    shape = lambda dt: jax.ShapeDtypeStruct((PEER_HEADS, t // 128, PEER_KEYS, 128), dt)
    return pl.pallas_call(
        _peer_select_kernel,
        grid=(t // tt,),
        in_specs=[pl.BlockSpec((tt // 128, D_MODEL, 128), lambda i: (i, 0, 0)),
                  pl.BlockSpec(wts["wpq_t"].shape, lambda i: (0, 0)),
                  pl.BlockSpec(wts["sub_keys"].shape, lambda i: (0, 0, 0))],
        out_specs=[out, out, out, out],
        out_shape=[shape(F32), shape(F32), shape(BF16), shape(BF16)],
        scratch_shapes=[pltpu.VMEM((2 * PEER_HEADS, PEER_KEYS, tt), F32), pltpu.VMEM((2 * PEER_HEADS, PEER_TOPK, tt), F32),
                        pltpu.VMEM((2 * PEER_HEADS, PEER_KEYS, tt), F32)],
        compiler_params=_params("parallel"),
        name="peer_select",
    )(hnt, wts["wpq_t"], wts["sub_keys"])


A_PER_TILE = 8


def _rows_bf16(row):
    return jnp.broadcast_to(row, (PEER_KEYS, 128)).astype(BF16)


def _peer_dense_kernel(hnt_ref, u_ref, vt_ref, n1_ref, q1_ref, r2_ref, q2_ref, h_ref, y_ref, acc_ref, pre_ref, w_ref,
                       r2_scr, q2_scr, *, tt):
    j = pl.program_id(1)
    nslab = tt // 128

    @pl.when(j == 0)
    def _():
        acc_ref[...] = jnp.zeros_like(acc_ref)
        r2_scr[...] = r2_ref[...]
        q2_scr[...] = q2_ref[...]

    per_sub = min(nslab, 2)
    nsub = nslab // per_sub
    zero = jnp.zeros((PEER_KEYS, 128), BF16)
    group = 2

    def mm_u(s):
        hn = jnp.concatenate([hnt_ref[s * per_sub + c] for c in range(per_sub)], axis=1)
        pre = _dot(u_ref[...], hn).astype(BF16)
        for c in range(per_sub):
            pre_ref[s * per_sub + c] = pre[:, c * 128:(c + 1) * 128]

    def build(tc, carry):
        for ia0 in range(0, A_PER_TILE, group):
            gates = [zero] * group
            for h in range(PEER_HEADS):
                r2 = r2_scr[h, tc]
                q2 = q2_scr[h, tc]
                for k in range(group):
                    n1 = _rows_bf16(n1_ref[h, tc, ia0 + k:ia0 + k + 1, :])
                    q1 = _rows_bf16(q1_ref[h, tc, ia0 + k:ia0 + k + 1, :])
                    gates[k] = gates[k] + jnp.where(r2 < n1, q2, zero) * q1
            for k in range(group):
                rows = slice((ia0 + k) * PEER_KEYS, (ia0 + k + 1) * PEER_KEYS)
                p = pre_ref[tc, rows, :]
                act = (0.5 * p) * (1.0 + lax.erf(p * (1.0 / math.sqrt(2.0))))
                w_ref[tc, rows, :] = gates[k] * act
        return carry

    def mm_v(s):
        w = jnp.concatenate([w_ref[s * per_sub + c] for c in range(per_sub)], axis=1)
        out = _dot(vt_ref[...], w)
        for c in range(per_sub):
            acc_ref[s * per_sub + c] += out[:, c * 128:(c + 1) * 128]

    for s in range(nsub):
        mm_u(s)
    lax.fori_loop(0, nslab, build, 0)
    for s in range(nsub):
        mm_v(s)

    @pl.when(j == pl.num_programs(1) - 1)
    def _():
        for c in range(nslab):
            y_ref[c * 128:(c + 1) * 128, :] = h_ref[c * 128:(c + 1) * 128, :] + acc_ref[c].T


def _peer_dense(hnt, h, sel, wts, tt):
    t = hnt.shape[0] * 128
    n1, q1, r2, q2 = sel
    et = A_PER_TILE * PEER_KEYS
    nslab = tt // 128
    a_spec = pl.BlockSpec((PEER_HEADS, nslab, A_PER_TILE, 128), lambda i, j: (0, i, j, 0))
    b_spec = pl.BlockSpec((PEER_HEADS, nslab, PEER_KEYS, 128), lambda i, j: (0, i, 0, 0))
    return pl.pallas_call(
        functools.partial(_peer_dense_kernel, tt=tt),
        grid=(t // tt, PEER_KEYS // A_PER_TILE),
        in_specs=[pl.BlockSpec((nslab, D_MODEL, 128), lambda i, j: (i, 0, 0)),
                  pl.BlockSpec((et, D_MODEL), lambda i, j: (j, 0)),
                  pl.BlockSpec((D_MODEL, et), lambda i, j: (0, j)),
                  a_spec, a_spec, b_spec, b_spec,
                  pl.BlockSpec((tt, D_MODEL), lambda i, j: (i, 0))],
        out_specs=pl.BlockSpec((tt, D_MODEL), lambda i, j: (i, 0)),
        out_shape=jax.ShapeDtypeStruct((t, D_MODEL), F32),
        scratch_shapes=[pltpu.VMEM((nslab, D_MODEL, 128), F32), pltpu.VMEM((nslab, et, 128), BF16),
                        pltpu.VMEM((nslab, et, 128), BF16),
                        pltpu.VMEM((PEER_HEADS, nslab, PEER_KEYS, 128), BF16),
                        pltpu.VMEM((PEER_HEADS, nslab, PEER_KEYS, 128), BF16)],
        compiler_params=_params("parallel", "arbitrary"),
        name="peer_dense",
    )(hnt, wts["u"], wts["v_t"], n1, q1, r2, q2, h)


def _rope_tables(pos, reps):
    half = A_HEAD_DIM // 2
    inv_freq = ROPE_THETA ** (-jnp.arange(half, dtype=F32) / half)
    ang = pos.astype(F32)[:, None] * inv_freq[None, :]
    cos = jnp.cos(ang)
    sin = jnp.sin(ang)
    cos_t = jnp.tile(jnp.concatenate([cos, cos], axis=-1), (reps, A_HEADS))
    sin_t = jnp.tile(jnp.concatenate([-sin, sin], axis=-1), (reps, A_HEADS))
    return cos_t, sin_t


def _prep_weights(norm_mix_w, w_in, q_norm_w, k_norm_w, a_log, dt_bias, w_out, norm_ffn_w, w_pq, sub_keys, expert_u, expert_v):
    na = 3 * A_WIDTH
    hb = B_HEADS * B_DIM
    wgate = w_in[:, na + B_QKV + hb:]
    wg = jnp.zeros((D_MODEL, B_HEADS, B_DIM), F32)
    wg = wg.at[:, :, 0].set(wgate[:, :B_HEADS]).at[:, :, 1].set(wgate[:, B_HEADS:])
    lane_head = jnp.arange(A_WIDTH) // A_HEAD_DIM
    put1 = lambda vec: jnp.zeros((B_HEADS, B_DIM), F32).at[:, 1].set(vec).reshape(1, hb)
    return {
        "norm_mix": norm_mix_w.reshape(1, D_MODEL),
        "wa": w_in[:, :na].astype(BF16),
        "wb": w_in[:, na:na + B_QKV].astype(BF16),
        "wz": w_in[:, na + B_QKV:na + B_QKV + hb].astype(BF16),
        "wg": wg.reshape(D_MODEL, hb).astype(BF16),
        "qnw": jnp.tile(q_norm_w, A_HEADS).reshape(1, A_WIDTH),
        "knw": jnp.tile(k_norm_w, A_HEADS).reshape(1, A_WIDTH),
        "headmat": (lane_head[:, None] == lane_head[None, :]).astype(BF16),
        "alog": put1(a_log),
        "dtb": put1(dt_bias),
        "wo_a": w_out[:A_WIDTH].astype(BF16),
        "wo_b": w_out[A_WIDTH:].astype(BF16),
        "norm_ffn": norm_ffn_w.reshape(1, D_MODEL),
        "wpq_t": w_pq.T.astype(BF16),
        "sub_keys": sub_keys.astype(BF16),
        "u": expert_u.astype(BF16),
        "v_t": expert_v.T.astype(BF16),
    }


def _layer(x, pos_cos, pos_sin, pos_blocks, wts, conv_w, dnw, mixer_a, hist, s0, chunk, tm, tt_sel, tt_dense):
    b, t, _ = x.shape
    xf = x.reshape(b * t, D_MODEL)
    q, k, v, raw, z, gb = _proj_in(xf, pos_cos, pos_sin, pos_blocks, wts, tm)
    a_out, k_buf, v_buf = mixer_a(q.reshape(b, t, A_WIDTH), k.reshape(b, t, A_WIDTH), v.reshape(b, t, A_WIDTH))
    raw = raw.reshape(b, t, B_QKV)
    conv_state = raw[:, t - (CONV_WIDTH - 1):]
    pad = (-t) % chunk
    pad3 = lambda arr: jnp.pad(arr.reshape(b, t, -1), ((0, 0), (0, pad), (0, 0)))
    b_out, s_fin = _delta(pad3(raw), hist, conv_w, pad3(gb), pad3(z), dnw, s0, chunk, min(512, t + pad))
    b_out = b_out[:, :t].reshape(b * t, B_HEADS * B_DIM)
    h, hnt = _proj_out(xf, a_out.reshape(b * t, A_WIDTH), b_out, wts, tm)
    sel = _peer_select(hnt, wts, tt_sel)
    y = _peer_dense(hnt, h, sel, wts, tt_dense)
    return y.reshape(b, t, D_MODEL), k_buf, v_buf, conv_state, s_fin


def kernel(x_prompt, x_sample, cache_k_win, cache_v_win, state_conv, state_delta, norm_mix_w, w_in, q_norm_w, k_norm_w, conv_w, a_log, dt_bias, delta_norm_w, w_out, norm_ffn_w, w_pq, sub_keys, expert_u, expert_v):
    assert w_in.shape[0] == 1, "one layer"
    bp, sp, _ = x_prompt.shape
    bs, ss, _ = x_sample.shape
    wts = _prep_weights(norm_mix_w[0], w_in[0], q_norm_w[0], k_norm_w[0], a_log[0], dt_bias[0], w_out[0], norm_ffn_w[0],
                        w_pq[0], sub_keys[0], expert_u[0], expert_v[0])
    cw = conv_w[0]
    dnw = delta_norm_w[0].reshape(1, B_DIM)
    lane_head = jnp.arange(A_WIDTH) // A_HEAD_DIM
    head_sum = (lane_head[:, None] == jnp.arange(128)[None, :]).astype(F32)
    head_expand = head_sum.T

    tm_p = 512
    cos_p, sin_p = _rope_tables(jnp.arange(sp), 1)

    def mixer_prompt(q, k, v):
        return _attn_prompt(q, k, v), k, v

    yp, kp, vp, cp, dp = _layer(
        x_prompt, cos_p, sin_p, sp // tm_p, wts, cw, dnw, mixer_prompt,
        jnp.zeros((bp, 8, B_QKV), F32), jnp.zeros((bp, B_HEADS, B_DIM, B_DIM), F32),
        chunk=64, tm=tm_p, tt_sel=512, tt_dense=1024)

    n_s = bs * ss
    cos_s, sin_s = _rope_tables(PAST_LEN + jnp.arange(ss), bs)

    def mixer_sample(q, k, v):
        kc = cache_k_win[0].reshape(bs, -1, A_WIDTH)
        vc = cache_v_win[0].reshape(bs, -1, A_WIDTH)
        return _attn_sample(q, k, v, kc, vc, head_sum, head_expand)

    hist_s = jnp.pad(state_conv[0], ((0, 0), (8 - (CONV_WIDTH - 1), 0), (0, 0)))
    ys, ks, vs, cs, ds = _layer(
        x_sample, cos_s, sin_s, 1, wts, cw, dnw, mixer_sample, hist_s, state_delta[0],
        chunk=8, tm=n_s, tt_sel=n_s, tt_dense=n_s)

    win_p = min(BAND * max(DILATIONS), sp)
    shape_kv = lambda arr, b: arr.reshape(1, b, -1, A_HEADS, A_HEAD_DIM)
    return (yp, ys,
            shape_kv(kp[:, sp - win_p:], bp), shape_kv(vp[:, sp - win_p:], bp), cp[None], dp[None],
            shape_kv(ks, bs), shape_kv(vs, bs), cs[None], ds[None])
```

```python
import functools
import math

import jax
import jax.numpy as jnp
from jax import lax
from jax.experimental import pallas as pl
from jax.experimental.pallas import tpu as pltpu

F32 = jnp.float32
BF16 = jnp.bfloat16

D_MODEL = 1024
A_HEADS = 8
A_HEAD_DIM = 64
A_WIDTH = A_HEADS * A_HEAD_DIM
DILATIONS = (1, 4, 16)
BAND = 128
ROPE_THETA = 10000.0
PAST_LEN = 8192
B_HEADS = 4
B_DIM = 128
B_QKV = 3 * B_HEADS * B_DIM
CONV_WIDTH = 4
PEER_HEADS = 8
PEER_KEYS = 128
PEER_TOPK = 16
EPS = 1e-6
NEG_INF = float("-inf")
VMEM_LIMIT = 56 * 1024 * 1024


def _params(*sem):
    return pltpu.CompilerParams(dimension_semantics=sem, vmem_limit_bytes=VMEM_LIMIT)


def _dot(a, b, **kw):
    return jnp.dot(a, b, preferred_element_type=F32, **kw)


def _dot_nt(a, b, **kw):
    return lax.dot_general(a, b, (((1,), (1,)), ((), ())), preferred_element_type=F32, **kw)


def _dot_tn(a, b, **kw):
    return lax.dot_general(a, b, (((0,), (0,)), ((), ())), preferred_element_type=F32, **kw)


def _dot_x01(a, b01):
    b = b01.astype(BF16)
    a1 = a.astype(BF16)
    r1 = a - a1.astype(F32)
    a2 = r1.astype(BF16)
    a3 = (r1 - a2.astype(F32)).astype(BF16)
    return _dot(a1, b) + (_dot(a2, b) + _dot(a3, b))


def _sigmoid(x):
    return 1.0 / (1.0 + jnp.exp(-x))


def _proj_in_kernel(x_ref, nw_ref, wa_ref, wb_ref, wz_ref, wg_ref, qnw_ref, knw_ref, cos_ref, sin_ref,
                    hm_ref, alog_ref, dtb_ref, q_ref, k_ref, v_ref, raw_ref, z_ref, gb_ref):
    x = x_ref[...]
    ms = jnp.mean(x * x, axis=-1, keepdims=True)
    xn = (x * lax.rsqrt(ms + EPS) * nw_ref[...]).astype(BF16)
    a = _dot(xn, wa_ref[...])
    hm = hm_ref[...]
    lane = lax.broadcasted_iota(jnp.int32, (1, A_WIDTH), 1)
    first_half = (lane & (A_HEAD_DIM // 2)) == 0
    cos = cos_ref[...]
    sin = sin_ref[...]

    def norm_rope(t, w):
        t2 = t * t
        hi = t2.astype(BF16)
        lo = (t2 - hi.astype(F32)).astype(BF16)
        ss = _dot(hi, hm) + _dot(lo, hm)
        tn = t * lax.rsqrt(ss * (1.0 / A_HEAD_DIM) + EPS) * w
        partner = jnp.where(first_half, pltpu.roll(tn, A_WIDTH - A_HEAD_DIM // 2, 1), pltpu.roll(tn, A_HEAD_DIM // 2, 1))
        return tn * cos + partner * sin

    q_ref[...] = norm_rope(a[:, :A_WIDTH], qnw_ref[...])
    k_ref[...] = norm_rope(a[:, A_WIDTH:2 * A_WIDTH], knw_ref[...])
    v_ref[...] = a[:, 2 * A_WIDTH:]
    raw_ref[...] = _dot(xn, wb_ref[...])
    z_ref[...] = _dot(xn, wz_ref[...])
    zg = _dot(xn, wg_ref[...])
    beta = _sigmoid(zg)
    t = zg + dtb_ref[...]
    softplus = jnp.maximum(t, 0.0) + jnp.log1p(jnp.exp(-jnp.abs(t)))
    g = -jnp.exp(alog_ref[...]) * softplus
    glane = lax.broadcasted_iota(jnp.int32, (1, B_HEADS * B_DIM), 1) & (B_DIM - 1)
    gb_ref[...] = jnp.where(glane == 0, beta, jnp.where(glane == 1, g, 0.0))


def _proj_in(x, pos_cos, pos_sin, pos_blocks, wts, tm):
    t = x.shape[0]
    n = t // tm
    row = lambda i: (i, 0)
    const = lambda i: (0, 0)
    posmap = lambda i: (i % pos_blocks, 0)
    full = lambda a: pl.BlockSpec(a.shape, const)
    outs = [(A_WIDTH, "q"), (A_WIDTH, "k"), (A_WIDTH, "v"), (B_QKV, "raw"), (B_HEADS * B_DIM, "z"), (B_HEADS * B_DIM, "gb")]
    return pl.pallas_call(
        _proj_in_kernel,
        grid=(n,),
        in_specs=[pl.BlockSpec((tm, D_MODEL), row), full(wts["norm_mix"]), full(wts["wa"]), full(wts["wb"]), full(wts["wz"]),
                  full(wts["wg"]), full(wts["qnw"]), full(wts["knw"]),
                  pl.BlockSpec((tm, A_WIDTH), posmap), pl.BlockSpec((tm, A_WIDTH), posmap),
                  full(wts["headmat"]), full(wts["alog"]), full(wts["dtb"])],
        out_specs=[pl.BlockSpec((tm, w), row) for w, _ in outs],
        out_shape=[jax.ShapeDtypeStruct((t, w), F32) for w, _ in outs],
        compiler_params=_params("parallel"),
        name="proj_in",
    )(x, wts["norm_mix"], wts["wa"], wts["wb"], wts["wz"], wts["wg"], wts["qnw"], wts["knw"], pos_cos, pos_sin,
      wts["headmat"], wts["alog"], wts["dtb"])


def _attn_prompt_kernel(q_ref, k_ref, v_ref, o_ref, ob_ref, lb_ref, *, seq):
    lane = lax.broadcasted_iota(jnp.int32, (BAND, 2 * A_HEAD_DIM), 1)
    head0 = lane < A_HEAD_DIM
    qi = lax.broadcasted_iota(jnp.int32, (BAND, 2 * BAND), 0)
    kj = lax.broadcasted_iota(jnp.int32, (BAND, 2 * BAND), 1)
    dist = BAND + qi - kj
    in_band = (dist >= 0) & (dist <= BAND)
    scale = 1.0 / math.sqrt(A_HEAD_DIM)

    for br, dil in enumerate(DILATIONS):
        nblk = seq // (BAND * dil)
        shift = dil.bit_length() - 1

        def rows(ref, start):
            if dil == 1:
                return ref[pl.ds(start, BAND), :]
            return ref[pl.ds(start, BAND, stride=dil), :]

        def load(u):
            r = u & (dil - 1)
            n = u >> shift
            start = r + n * (BAND * dil)
            qb = rows(q_ref, start)
            k_own = rows(k_ref, start).astype(BF16)
            v_own = rows(v_ref, start).astype(BF16)
            if nblk > 1:
                pstart = jnp.maximum(start - BAND * dil, r)
                kk = jnp.concatenate([rows(k_ref, pstart).astype(BF16), k_own], axis=0)
                vv = jnp.concatenate([rows(v_ref, pstart).astype(BF16), v_own], axis=0)
                mask = in_band & (kj >= jnp.where(n > 0, 0, BAND))
            else:
                kk = jnp.concatenate([k_own, k_own], axis=0)
                vv = jnp.concatenate([v_own, v_own], axis=0)
                mask = in_band & (kj >= BAND)
            return start, qb, kk, vv, mask

        def attend(qb, kk, vv, mask):
            outs = []
            lses = []
            for hmask in (head0, ~head0):
                qh = jnp.where(hmask, qb, 0.0).astype(BF16)
                s = _dot_nt(qh, kk) * scale
                s = jnp.where(mask, s, NEG_INF)
                mx = jnp.max(s, axis=-1, keepdims=True)
                p = jnp.exp(s - mx)
                l = jnp.sum(p, axis=-1, keepdims=True)
                outs.append(_dot(p.astype(BF16), vv) / l)
                lses.append(mx + jnp.log(l))
            return jnp.where(head0, outs[0], outs[1]), jnp.where(head0, lses[0], lses[1])

        per_group = 16

        def group(g, carry):
            loaded = [load(g * per_group + i) for i in range(per_group)]
            results = [attend(*item[1:]) for item in loaded]
            for (start, *_), (out, lse) in zip(loaded, results):
                if dil == 1:
                    ob_ref[br, pl.ds(start, BAND), :] = out
                    lb_ref[br, pl.ds(start, BAND), :] = lse
                else:
                    ob_ref[br, pl.ds(start, BAND, stride=dil), :] = out
                    lb_ref[br, pl.ds(start, BAND, stride=dil), :] = lse
            return carry

        lax.fori_loop(0, seq // BAND // per_group, group, 0)

    def merge(c, carry):
        rs = pl.ds(pl.multiple_of(c * 256, 256), 256)
        l0, l1, l2 = lb_ref[0, rs, :], lb_ref[1, rs, :], lb_ref[2, rs, :]
        m = jnp.maximum(jnp.maximum(l0, l1), l2)
        w0, w1, w2 = jnp.exp(l0 - m), jnp.exp(l1 - m), jnp.exp(l2 - m)
        o_ref[rs, :] = (w0 * ob_ref[0, rs, :] + w1 * ob_ref[1, rs, :] + w2 * ob_ref[2, rs, :]) / (w0 + w1 + w2)
        return carry

    lax.fori_loop(0, seq // 256, merge, 0)


def _attn_prompt(q, k, v):
    b, s, _ = q.shape
    spec = pl.BlockSpec((None, s, 2 * A_HEAD_DIM), lambda i, j: (i, 0, j))
    return pl.pallas_call(
        functools.partial(_attn_prompt_kernel, seq=s),
        grid=(b, A_HEADS // 2),
        in_specs=[spec, spec, spec],
        out_specs=spec,
        out_shape=jax.ShapeDtypeStruct((b, s, A_WIDTH), F32),
        scratch_shapes=[pltpu.VMEM((len(DILATIONS), s, 2 * A_HEAD_DIM), F32),
                        pltpu.VMEM((len(DILATIONS), s, 2 * A_HEAD_DIM), F32)],
        compiler_params=_params("parallel", "parallel"),
        name="attn_prompt",
    )(q, k, v)


SAMPLE_PAD = 16


def _attn_sample_kernel(q_ref, kn_ref, vn_ref, kc_ref, vc_ref, hs_ref, he_ref, o_ref, ko_ref, vo_ref, kbuf, vbuf,
                        *, win, new):
    nlb = A_WIDTH // 128
    for buf, cache, fresh, shifted in ((kbuf, kc_ref, kn_ref, ko_ref), (vbuf, vc_ref, vn_ref, vo_ref)):
        for j in range(nlb):
            ls = slice(j * 128, (j + 1) * 128)
            buf[j, 0:win, :] = cache[:, ls]
            buf[j, win:win + new, :] = fresh[:, ls]
            buf[j, win + new:win + SAMPLE_PAD, :] = jnp.zeros((SAMPLE_PAD - new, 128), F32)
            shifted[:, ls] = buf[j, new:new + win, :]

    def gather(buf, start, size, stride):
        if stride == 1:
            return jnp.concatenate([buf[j, pl.ds(start, size), :] for j in range(nlb)], axis=1)
        return jnp.concatenate([buf[j, pl.ds(start, size, stride=stride), :] for j in range(nlb)], axis=1)

    hs = hs_ref[...]
    he = he_ref[...]
    scale = 1.0 / math.sqrt(A_HEAD_DIM)
    for l in range(new):
        ql = q_ref[l:l + 1, :]
        outs, lses = [], []
        for dil in DILATIONS:
            start = win + l - BAND * dil
            ks = gather(kbuf, start, BAND, dil)
            vs = gather(vbuf, start, BAND, dil)
            kself = gather(kbuf, win + l, 8, 1)
            vself = gather(vbuf, win + l, 1, 1)
            s_all = _dot_x01(jnp.concatenate([ks, kself], axis=0) * ql, hs) * scale
            s = s_all[:BAND]
            sself = s_all[BAND:BAND + 1]
            mx = jnp.maximum(jnp.max(s, axis=0, keepdims=True), sself)
            p = jnp.exp(s - mx)
            pself = jnp.exp(sself - mx)
            lsum = jnp.sum(p, axis=0, keepdims=True) + pself
            pe = _dot_x01(p, he)
            rows8 = jnp.concatenate([pself, lsum, mx + jnp.log(lsum), jnp.zeros((5, 128), F32)], axis=0)
            exp8 = _dot_x01(rows8, he)
            o = jnp.sum(pe * vs, axis=0, keepdims=True) + exp8[0:1] * vself
            outs.append(o / exp8[1:2])
            lses.append(exp8[2:3])
        m = jnp.maximum(jnp.maximum(lses[0], lses[1]), lses[2])
        w = [jnp.exp(ls - m) for ls in lses]
        o_ref[l:l + 1, :] = (w[0] * outs[0] + w[1] * outs[1] + w[2] * outs[2]) / (w[0] + w[1] + w[2])


def _attn_sample(q, kn, vn, kc, vc, hs, he):
    b, new, _ = q.shape
    win = kc.shape[1]
    assert win >= BAND * max(DILATIONS)
    small = pl.BlockSpec((None, new, A_WIDTH), lambda i: (i, 0, 0))
    big = pl.BlockSpec((None, win, A_WIDTH), lambda i: (i, 0, 0))
    const = lambda a: pl.BlockSpec(a.shape, lambda i: (0, 0))
    return pl.pallas_call(
        functools.partial(_attn_sample_kernel, win=win, new=new),
        grid=(b,),
        in_specs=[small, small, small, big, big, const(hs), const(he)],
        out_specs=[small, big, big],
        out_shape=[jax.ShapeDtypeStruct((b, new, A_WIDTH), F32), jax.ShapeDtypeStruct((b, win, A_WIDTH), F32),
                   jax.ShapeDtypeStruct((b, win, A_WIDTH), F32)],
        scratch_shapes=[pltpu.VMEM((A_WIDTH // 128, win + SAMPLE_PAD, 128), F32),
                        pltpu.VMEM((A_WIDTH // 128, win + SAMPLE_PAD, 128), F32)],
        compiler_params=_params("parallel"),
        name="attn_sample",
    )(q, kn, vn, kc, vc, hs, he)


def _dot3(a, b):
    ah = a.astype(BF16)
    al = (a - ah.astype(F32)).astype(BF16)
    bh = b.astype(BF16)
    bl = (b - bh.astype(F32)).astype(BF16)
    return _dot(ah, bh) + (_dot(ah, bl) + _dot(al, bh))


def _dot_01(a01, b):
    a = a01.astype(BF16)
    b1 = b.astype(BF16)
    r1 = b - b1.astype(F32)
    b2 = r1.astype(BF16)
    b3 = (r1 - b2.astype(F32)).astype(BF16)
    return _dot(a, b1) + (_dot(a, b2) + _dot(a, b3))


def _delta_kernel(raw_ref, hist_ref, cw_ref, gb_ref, z_ref, dnw_ref, s0_ref, o_ref, sf_ref, ext, s_scr, *, blk, chunk):
    c = chunk
    nh = B_HEADS
    r = nh * c
    ncol = 3 * nh
    shift = c.bit_length() - 1
    j = pl.program_id(1)

    @pl.when(j == 0)
    def _():
        for col in range(ncol):
            ext[col, 0:8, :] = hist_ref[:, col * B_DIM:(col + 1) * B_DIM]
        for k in range(nh):
            s_scr[k * B_DIM:(k + 1) * B_DIM, :] = s0_ref[k]

    for col in range(ncol):
        ext[col, 8:8 + blk, :] = raw_ref[:, col * B_DIM:(col + 1) * B_DIM]

    ii = lax.broadcasted_iota(jnp.int32, (r, r), 0)
    jj = lax.broadcasted_iota(jnp.int32, (r, r), 1)
    same = (ii >> shift) == (jj >> shift)
    causal = same & (ii >= jj)
    strict = same & (ii > jj)
    upper = same & (ii <= jj)
    eye = (ii == jj).astype(F32)
    tri = causal.astype(F32)
    ones = jnp.ones((r, r), F32)
    row_head = lax.broadcasted_iota(jnp.int32, (r, nh * B_DIM), 0) >> shift
    lane_head = lax.broadcasted_iota(jnp.int32, (r, nh * B_DIM), 1) >> (B_DIM.bit_length() - 1)
    own_head = row_head == lane_head
    dnw = dnw_ref[...]

    def conv_silu(col, base):
        y = cw_ref[0:1, col * B_DIM:(col + 1) * B_DIM] * ext[col, pl.ds(base + 5, c), :]
        for t in range(1, CONV_WIDTH):
            y = y + cw_ref[t:t + 1, col * B_DIM:(col + 1) * B_DIM] * ext[col, pl.ds(base + 5 + t, c), :]
        return y * _sigmoid(y)

    def l2n(t):
        return t * lax.rsqrt(jnp.sum(t * t, axis=-1, keepdims=True) + EPS)

    def stack(parts):
        return jnp.concatenate(parts, axis=0)

    def block_diag(x):
        return jnp.where(own_head, jnp.concatenate([x] * nh, axis=1), 0.0)

    def body(n, carry):
        base = pl.multiple_of(n * c, c)
        q = stack([l2n(conv_silu(k, base)) for k in range(nh)]) * (B_DIM ** -0.5)
        kk = stack([l2n(conv_silu(nh + k, base)) for k in range(nh)])
        v = stack([conv_silu(2 * nh + k, base) for k in range(nh)])
        gbs = [gb_ref[pl.ds(base, c), k * B_DIM:(k + 1) * B_DIM] for k in range(nh)]
        beta = stack([t[:, 0:1] for t in gbs])
        g = stack([t[:, 1:2] for t in gbs])
        gsq = jnp.broadcast_to(g, (r, r))
        gcol = _dot_01(tri, gsq)
        grow = _dot_01(ones, jnp.where(upper, gsq, 0.0))
        gcum = gcol[:, 0:1]
        glast = stack([jnp.broadcast_to(gcol[(k + 1) * c - 1:(k + 1) * c, 0:1], (c, 1)) for k in range(nh)])
        decay = jnp.exp(jnp.where(causal, gcol - grow, NEG_INF))
        kb = kk * beta
        kbh, kh, qh = kb.astype(BF16), kk.astype(BF16), q.astype(BF16)
        lower = jnp.where(strict, _dot_nt(kbh, kh) * decay, 0.0)
        tinv = eye - lower
        lp = lower
        span = 2
        while span < c:
            lp = _dot3(lp, lp)
            tinv = tinv + _dot3(tinv, lp)
            span *= 2
        eg = jnp.exp(gcum)
        u = _dot3(tinv, v * beta)
        w = _dot3(tinv, kb * eg)
        intra = jnp.where(causal, _dot_nt(qh, kh) * decay, 0.0)
        state = s_scr[...]
        sh = state.astype(BF16)
        v_new = u - _dot(block_diag(w).astype(BF16), sh)
        vh = v_new.astype(BF16)
        out = _dot(block_diag(q * eg).astype(BF16), sh) + _dot(intra.astype(BF16), vh)
        kdec = kk * jnp.exp(glast - gcum)
        cd = stack([jnp.broadcast_to(jnp.exp(gcol[(k + 1) * c - 1:(k + 1) * c, 0:1]), (B_DIM, 1)) for k in range(nh)])
        s_scr[...] = state * cd + _dot_tn(block_diag(kdec).astype(BF16), vh)
        on = out * lax.rsqrt(jnp.mean(out * out, axis=-1, keepdims=True) + EPS) * dnw
        for k in range(nh):
            zc = z_ref[pl.ds(base, c), k * B_DIM:(k + 1) * B_DIM]
            o_ref[pl.ds(base, c), k * B_DIM:(k + 1) * B_DIM] = on[k * c:(k + 1) * c] * (zc * _sigmoid(zc))
        return carry

    lax.fori_loop(0, blk // c, body, 0, unroll=2 if blk // c >= 2 else 1)
    for col in range(ncol):
        ext[col, 0:8, :] = ext[col, blk:blk + 8, :]

    @pl.when(j == pl.num_programs(1) - 1)
    def _():
        for k in range(nh):
            sf_ref[k] = s_scr[k * B_DIM:(k + 1) * B_DIM, :]


def _delta(raw, hist, conv_w, gb, z, dnw, s0, chunk, blk):
    b, seq, _ = raw.shape
    hb = B_HEADS * B_DIM
    tok = lambda w: pl.BlockSpec((None, blk, w), lambda i, j: (i, j, 0))
    st = pl.BlockSpec((None, B_HEADS, B_DIM, B_DIM), lambda i, j: (i, 0, 0, 0))
    return pl.pallas_call(
        functools.partial(_delta_kernel, blk=blk, chunk=chunk),
        grid=(b, seq // blk),
        in_specs=[tok(B_QKV), pl.BlockSpec((None, 8, B_QKV), lambda i, j: (i, 0, 0)),
                  pl.BlockSpec((CONV_WIDTH, B_QKV), lambda i, j: (0, 0)), tok(hb), tok(hb),
                  pl.BlockSpec((1, B_DIM), lambda i, j: (0, 0)), st],
        out_specs=[tok(hb), st],
        out_shape=[jax.ShapeDtypeStruct((b, seq, hb), F32), jax.ShapeDtypeStruct((b, B_HEADS, B_DIM, B_DIM), F32)],
        scratch_shapes=[pltpu.VMEM((3 * B_HEADS, blk + 8, B_DIM), F32), pltpu.VMEM((B_HEADS * B_DIM, B_DIM), F32)],
        compiler_params=_params("parallel", "arbitrary"),
        name="delta",
    )(raw, hist, conv_w, gb, z, dnw, s0)


def _proj_out_kernel(x_ref, a_ref, b_ref, wa_ref, wb_ref, nw_ref, h_ref, hnt_ref):
    h = x_ref[...] + _dot(a_ref[...].astype(BF16), wa_ref[...]) + _dot(b_ref[...].astype(BF16), wb_ref[...])
    h_ref[...] = h
    hn = h * lax.rsqrt(jnp.mean(h * h, axis=-1, keepdims=True) + EPS) * nw_ref[...]
    hnt = hn.T.astype(BF16)
    for c in range(hnt_ref.shape[0]):
        hnt_ref[c] = hnt[:, c * 128:(c + 1) * 128]


def _proj_out(x, a, b, wts, tm):
    t = x.shape[0]
    row = lambda i: (i, 0)
    full = lambda arr: pl.BlockSpec(arr.shape, lambda i: (0, 0))
    return pl.pallas_call(
        _proj_out_kernel,
        grid=(t // tm,),
        in_specs=[pl.BlockSpec((tm, D_MODEL), row), pl.BlockSpec((tm, A_WIDTH), row), pl.BlockSpec((tm, A_WIDTH), row),
                  full(wts["wo_a"]), full(wts["wo_b"]), full(wts["norm_ffn"])],
        out_specs=[pl.BlockSpec((tm, D_MODEL), row), pl.BlockSpec((tm // 128, D_MODEL, 128), lambda i: (i, 0, 0))],
        out_shape=[jax.ShapeDtypeStruct((t, D_MODEL), F32), jax.ShapeDtypeStruct((t // 128, D_MODEL, 128), BF16)],
        compiler_params=_params("parallel"),
        name="proj_out",
    )(x, a, b, wts["wo_a"], wts["wo_b"], wts["norm_ffn"])


def _top16_ranked(x):
    t = x.shape[1]
    slot = lax.broadcasted_iota(jnp.int32, (PEER_TOPK, t), 0).astype(F32)
    vals = jnp.full((PEER_TOPK, t), NEG_INF, F32)
    rank = jnp.full(x.shape, float(PEER_TOPK), F32)
    filled = jnp.zeros((1, t), F32)
    for _ in range(PEER_TOPK):
        m = jnp.max(x, axis=0, keepdims=True)
        eq = x == m
        cnt = jnp.sum(jnp.where(eq, 1.0, 0.0), axis=0, keepdims=True)
        vals = jnp.where((slot >= filled) & (slot < filled + cnt), m, vals)
        rank = jnp.where(eq, jnp.minimum(filled, float(PEER_TOPK)), rank)
        filled = filled + cnt
        x = jnp.where(eq, NEG_INF, x)
    return vals, rank


def _kth_largest(x, k):
    t = x.shape[1]
    kth = jnp.full((1, t), NEG_INF, F32)
    filled = jnp.zeros((1, t), F32)
    for _ in range(k):
        m = jnp.max(x, axis=0, keepdims=True)
        eq = x == m
        cnt = jnp.sum(jnp.where(eq, 1.0, 0.0), axis=0, keepdims=True)
        kth = jnp.where((filled < k) & (filled + cnt >= k), m, kth)
        filled = filled + cnt
        x = jnp.where(eq, NEG_INF, x)
    return kth


def _peer_select_kernel(hnt_ref, wpq_ref, sk_ref, n1_ref, q1_ref, r2_ref, q2_ref, sub_scr, top_scr, rank_scr):
    nslab = hnt_ref.shape[0]
    hnt = jnp.concatenate([hnt_ref[c] for c in range(nslab)], axis=1)

    def half(hc, carry):
        row0 = pl.multiple_of(hc * PEER_KEYS, PEER_KEYS)
        qt = _dot(wpq_ref[pl.ds(row0, PEER_KEYS), :], hnt)
        sub = _dot(sk_ref[hc & 1], qt.astype(BF16))
        sub_scr[hc] = sub
        top_scr[hc], rank_scr[hc] = _top16_ranked(sub)
        return carry

    lax.fori_loop(0, 2 * PEER_HEADS, half, 0, unroll=2)

    def slabs(ref, h, x):
        for c in range(nslab):
            ref[h, c] = x[:, c * 128:(c + 1) * 128].astype(ref.dtype)

    def head(h, carry):
        a = top_scr[2 * h]
        b = top_scr[2 * h + 1]
        pieces = [a[0:1] + b] + [a[i:i + 1] + b[0:8] for i in range(1, 8)] + [a[8:16] + b[0:1]]
        cand = jnp.concatenate(pieces, axis=0)
        tau = _kth_largest(cand, PEER_TOPK)
        top = a[0:1] + b[0:1]
        zsum = jnp.sum(jnp.where(cand >= tau, jnp.exp(cand - top), 0.0), axis=0, keepdims=True)
        cnt = jnp.zeros_like(a)
        for j in range(PEER_TOPK):
            cnt = cnt + jnp.where(a + b[j:j + 1] >= tau, 1.0, 0.0)
        rank1 = rank_scr[2 * h]
        n1 = jnp.zeros_like(rank1)
        for i in range(PEER_TOPK):
            n1 = jnp.where(rank1 == float(i), cnt[i:i + 1], n1)
        slabs(n1_ref, h, n1)
        slabs(r2_ref, h, rank_scr[2 * h + 1])
        slabs(q1_ref, h, jnp.exp(sub_scr[2 * h] - a[0:1]) / zsum)
        slabs(q2_ref, h, jnp.exp(sub_scr[2 * h + 1] - b[0:1]))
        return carry

    lax.fori_loop(0, PEER_HEADS, head, 0, unroll=2)


def _peer_select(hnt, wts, tt):
    t = hnt.shape[0] * 128
    out = pl.BlockSpec((PEER_HEADS, tt // 128, PEER_KEYS, 128), lambda i: (0, i, 0, 0))
    shape = lambda dt: jax.ShapeDtypeStruct((PEER_HEADS, t // 128, PEER_KEYS, 128), dt)
    return pl.pallas_call(
        _peer_select_kernel,
        grid=(t // tt,),
        in_specs=[pl.BlockSpec((tt // 128, D_MODEL, 128), lambda i: (i, 0, 0)),
                  pl.BlockSpec(wts["wpq_t"].shape, lambda i: (0, 0)),
                  pl.BlockSpec(wts["sub_keys"].shape, lambda i: (0, 0, 0))],
        out_specs=[out, out, out, out],
        out_shape=[shape(F32), shape(F32), shape(BF16), shape(BF16)],
        scratch_shapes=[pltpu.VMEM((2 * PEER_HEADS, PEER_KEYS, tt), F32), pltpu.VMEM((2 * PEER_HEADS, PEER_TOPK, tt), F32),
                        pltpu.VMEM((2 * PEER_HEADS, PEER_KEYS, tt), F32)],
        compiler_params=_params("parallel"),
        name="peer_select",
    )(hnt, wts["wpq_t"], wts["sub_keys"])


A_PER_TILE = 8


def _rows_bf16(row):
    return jnp.broadcast_to(row, (PEER_KEYS, 128)).astype(BF16)


def _peer_dense_kernel(hnt_ref, u_ref, vt_ref, n1_ref, q1_ref, r2_ref, q2_ref, h_ref, y_ref, acc_ref, pre_ref, w_ref,
                       r2_scr, q2_scr, *, tt):
    j = pl.program_id(1)
    nslab = tt // 128

    @pl.when(j == 0)
    def _():
        acc_ref[...] = jnp.zeros_like(acc_ref)
        r2_scr[...] = r2_ref[...]
        q2_scr[...] = q2_ref[...]

    per_sub = min(nslab, 2)
    nsub = nslab // per_sub
    zero = jnp.zeros((PEER_KEYS, 128), BF16)
    group = 2

    def mm_u(s):
        hn = jnp.concatenate([hnt_ref[s * per_sub + c] for c in range(per_sub)], axis=1)
        pre = _dot(u_ref[...], hn).astype(BF16)
        for c in range(per_sub):
            pre_ref[s * per_sub + c] = pre[:, c * 128:(c + 1) * 128]

    def build(tc, carry):
        for ia0 in range(0, A_PER_TILE, group):
            gates = [zero] * group
            for h in range(PEER_HEADS):
                r2 = r2_scr[h, tc]
                q2 = q2_scr[h, tc]
                for k in range(group):
                    n1 = _rows_bf16(n1_ref[h, tc, ia0 + k:ia0 + k + 1, :])
                    q1 = _rows_bf16(q1_ref[h, tc, ia0 + k:ia0 + k + 1, :])
                    gates[k] = gates[k] + jnp.where(r2 < n1, q2, zero) * q1
            for k in range(group):
                rows = slice((ia0 + k) * PEER_KEYS, (ia0 + k + 1) * PEER_KEYS)
                p = pre_ref[tc, rows, :]
                act = (0.5 * p) * (1.0 + lax.erf(p * (1.0 / math.sqrt(2.0))))
                w_ref[tc, rows, :] = gates[k] * act
        return carry

    def mm_v(s):
        w = jnp.concatenate([w_ref[s * per_sub + c] for c in range(per_sub)], axis=1)
        out = _dot(vt_ref[...], w)
        for c in range(per_sub):
            acc_ref[s * per_sub + c] += out[:, c * 128:(c + 1) * 128]

    for s in range(nsub):
        mm_u(s)
    lax.fori_loop(0, nslab, build, 0)
    for s in range(nsub):
        mm_v(s)

    @pl.when(j == pl.num_programs(1) - 1)
    def _():
        for c in range(nslab):
            y_ref[c * 128:(c + 1) * 128, :] = h_ref[c * 128:(c + 1) * 128, :] + acc_ref[c].T


def _peer_dense(hnt, h, sel, wts, tt):
    t = hnt.shape[0] * 128
    n1, q1, r2, q2 = sel
    et = A_PER_TILE * PEER_KEYS
    nslab = tt // 128
    a_spec = pl.BlockSpec((PEER_HEADS, nslab, A_PER_TILE, 128), lambda i, j: (0, i, j, 0))
    b_spec = pl.BlockSpec((PEER_HEADS, nslab, PEER_KEYS, 128), lambda i, j: (0, i, 0, 0))
    return pl.pallas_call(
        functools.partial(_peer_dense_kernel, tt=tt),
        grid=(t // tt, PEER_KEYS // A_PER_TILE),
        in_specs=[pl.BlockSpec((nslab, D_MODEL, 128), lambda i, j: (i, 0, 0)),
                  pl.BlockSpec((et, D_MODEL), lambda i, j: (j, 0)),
                  pl.BlockSpec((D_MODEL, et), lambda i, j: (0, j)),
                  a_spec, a_spec, b_spec, b_spec,
                  pl.BlockSpec((tt, D_MODEL), lambda i, j: (i, 0))],
        out_specs=pl.BlockSpec((tt, D_MODEL), lambda i, j: (i, 0)),
        out_shape=jax.ShapeDtypeStruct((t, D_MODEL), F32),
        scratch_shapes=[pltpu.VMEM((nslab, D_MODEL, 128), F32), pltpu.VMEM((nslab, et, 128), BF16),
                        pltpu.VMEM((nslab, et, 128), BF16),
                        pltpu.VMEM((PEER_HEADS, nslab, PEER_KEYS, 128), BF16),
                        pltpu.VMEM((PEER_HEADS, nslab, PEER_KEYS, 128), BF16)],
        compiler_params=_params("parallel", "arbitrary"),
        name="peer_dense",
    )(hnt, wts["u"], wts["v_t"], n1, q1, r2, q2, h)


def _rope_tables(pos, reps):
    half = A_HEAD_DIM // 2
    inv_freq = ROPE_THETA ** (-jnp.arange(half, dtype=F32) / half)
    ang = pos.astype(F32)[:, None] * inv_freq[None, :]
    cos = jnp.cos(ang)
    sin = jnp.sin(ang)
    cos_t = jnp.tile(jnp.concatenate([cos, cos], axis=-1), (reps, A_HEADS))
    sin_t = jnp.tile(jnp.concatenate([-sin, sin], axis=-1), (reps, A_HEADS))
    return cos_t, sin_t


def _prep_weights(norm_mix_w, w_in, q_norm_w, k_norm_w, a_log, dt_bias, w_out, norm_ffn_w, w_pq, sub_keys, expert_u, expert_v):
    na = 3 * A_WIDTH
    hb = B_HEADS * B_DIM
    wgate = w_in[:, na + B_QKV + hb:]
    wg = jnp.zeros((D_MODEL, B_HEADS, B_DIM), F32)
    wg = wg.at[:, :, 0].set(wgate[:, :B_HEADS]).at[:, :, 1].set(wgate[:, B_HEADS:])
    lane_head = jnp.arange(A_WIDTH) // A_HEAD_DIM
    put1 = lambda vec: jnp.zeros((B_HEADS, B_DIM), F32).at[:, 1].set(vec).reshape(1, hb)
    return {
        "norm_mix": norm_mix_w.reshape(1, D_MODEL),
        "wa": w_in[:, :na].astype(BF16),
        "wb": w_in[:, na:na + B_QKV].astype(BF16),
        "wz": w_in[:, na + B_QKV:na + B_QKV + hb].astype(BF16),
        "wg": wg.reshape(D_MODEL, hb).astype(BF16),
        "qnw": jnp.tile(q_norm_w, A_HEADS).reshape(1, A_WIDTH),
        "knw": jnp.tile(k_norm_w, A_HEADS).reshape(1, A_WIDTH),
        "headmat": (lane_head[:, None] == lane_head[None, :]).astype(BF16),
        "alog": put1(a_log),
        "dtb": put1(dt_bias),
        "wo_a": w_out[:A_WIDTH].astype(BF16),
        "wo_b": w_out[A_WIDTH:].astype(BF16),
        "norm_ffn": norm_ffn_w.reshape(1, D_MODEL),
        "wpq_t": w_pq.T.astype(BF16),
        "sub_keys": sub_keys.astype(BF16),
        "u": expert_u.astype(BF16),
        "v_t": expert_v.T.astype(BF16),
    }


def _layer(x, pos_cos, pos_sin, pos_blocks, wts, conv_w, dnw, mixer_a, hist, s0, chunk, tm, tt_sel, tt_dense):
    b, t, _ = x.shape
    xf = x.reshape(b * t, D_MODEL)
    q, k, v, raw, z, gb = _proj_in(xf, pos_cos, pos_sin, pos_blocks, wts, tm)
    a_out, k_buf, v_buf = mixer_a(q.reshape(b, t, A_WIDTH), k.reshape(b, t, A_WIDTH), v.reshape(b, t, A_WIDTH))
    raw = raw.reshape(b, t, B_QKV)
    conv_state = raw[:, t - (CONV_WIDTH - 1):]
    pad = (-t) % chunk
    pad3 = lambda arr: jnp.pad(arr.reshape(b, t, -1), ((0, 0), (0, pad), (0, 0)))
    b_out, s_fin = _delta(pad3(raw), hist, conv_w, pad3(gb), pad3(z), dnw, s0, chunk, min(512, t + pad))
    b_out = b_out[:, :t].reshape(b * t, B_HEADS * B_DIM)
    h, hnt = _proj_out(xf, a_out.reshape(b * t, A_WIDTH), b_out, wts, tm)
    sel = _peer_select(hnt, wts, tt_sel)
    y = _peer_dense(hnt, h, sel, wts, tt_dense)
    return y.reshape(b, t, D_MODEL), k_buf, v_buf, conv_state, s_fin


def kernel(x_prompt, x_sample, cache_k_win, cache_v_win, state_conv, state_delta, norm_mix_w, w_in, q_norm_w, k_norm_w, conv_w, a_log, dt_bias, delta_norm_w, w_out, norm_ffn_w, w_pq, sub_keys, expert_u, expert_v):
    assert w_in.shape[0] == 1, "one layer"
    bp, sp, _ = x_prompt.shape
    bs, ss, _ = x_sample.shape
    wts = _prep_weights(norm_mix_w[0], w_in[0], q_norm_w[0], k_norm_w[0], a_log[0], dt_bias[0], w_out[0], norm_ffn_w[0],
                        w_pq[0], sub_keys[0], expert_u[0], expert_v[0])
    cw = conv_w[0]
    dnw = delta_norm_w[0].reshape(1, B_DIM)
    lane_head = jnp.arange(A_WIDTH) // A_HEAD_DIM
    head_sum = (lane_head[:, None] == jnp.arange(128)[None, :]).astype(F32)
    head_expand = head_sum.T

    tm_p = 512
    cos_p, sin_p = _rope_tables(jnp.arange(sp), 1)

    def mixer_prompt(q, k, v):
        return _attn_prompt(q, k, v), k, v

    yp, kp, vp, cp, dp = _layer(
        x_prompt, cos_p, sin_p, sp // tm_p, wts, cw, dnw, mixer_prompt,
        jnp.zeros((bp, 8, B_QKV), F32), jnp.zeros((bp, B_HEADS, B_DIM, B_DIM), F32),
        chunk=64, tm=tm_p, tt_sel=512, tt_dense=1024)

    n_s = bs * ss
    cos_s, sin_s = _rope_tables(PAST_LEN + jnp.arange(ss), bs)

    def mixer_sample(q, k, v):
        kc = cache_k_win[0].reshape(bs, -1, A_WIDTH)
        vc = cache_v_win[0].reshape(bs, -1, A_WIDTH)
        return _attn_sample(q, k, v, kc, vc, head_sum, head_expand)

    hist_s = jnp.pad(state_conv[0], ((0, 0), (8 - (CONV_WIDTH - 1), 0), (0, 0)))
    ys, ks, vs, cs, ds = _layer(
        x_sample, cos_s, sin_s, 1, wts, cw, dnw, mixer_sample, hist_s, state_delta[0],
        chunk=8, tm=n_s, tt_sel=n_s, tt_dense=n_s)

    win_p = min(BAND * max(DILATIONS), sp)
    shape_kv = lambda arr, b: arr.reshape(1, b, -1, A_HEADS, A_HEAD_DIM)
    return (yp, ys,
            shape_kv(kp[:, sp - win_p:], bp), shape_kv(vp[:, sp - win_p:], bp), cp[None], dp[None],
            shape_kv(ks, bs), shape_kv(vs, bs), cs[None], ds[None])
```

```python
import functools
import math

import jax
import jax.numpy as jnp
from jax import lax
from jax.experimental import pallas as pl
from jax.experimental.pallas import tpu as pltpu

F32 = jnp.float32
BF16 = jnp.bfloat16

D_MODEL = 1024
A_HEADS = 8
A_HEAD_DIM = 64
A_WIDTH = A_HEADS * A_HEAD_DIM
DILATIONS = (1, 4, 16)
BAND = 128
ROPE_THETA = 10000.0
PAST_LEN = 8192
B_HEADS = 4
B_DIM = 128
B_QKV = 3 * B_HEADS * B_DIM
CONV_WIDTH = 4
PEER_HEADS = 8
PEER_KEYS = 128
PEER_TOPK = 16
EPS = 1e-6
NEG_INF = float("-inf")
VMEM_LIMIT = 56 * 1024 * 1024


def _params(*sem):
    return pltpu.CompilerParams(dimension_semantics=sem, vmem_limit_bytes=VMEM_LIMIT)


def _dot(a, b, **kw):
    return jnp.dot(a, b, preferred_element_type=F32, **kw)


def _dot_nt(a, b, **kw):
    return lax.dot_general(a, b, (((1,), (1,)), ((), ())), preferred_element_type=F32, **kw)


def _dot_tn(a, b, **kw):
    return lax.dot_general(a, b, (((0,), (0,)), ((), ())), preferred_element_type=F32, **kw)


def _dot_x01(a, b01):
    b = b01.astype(BF16)
    a1 = a.astype(BF16)
    r1 = a - a1.astype(F32)
    a2 = r1.astype(BF16)
    a3 = (r1 - a2.astype(F32)).astype(BF16)
    return _dot(a1, b) + (_dot(a2, b) + _dot(a3, b))


def _sigmoid(x):
    return 1.0 / (1.0 + jnp.exp(-x))


def _proj_in_kernel(x_ref, nw_ref, wa_ref, wb_ref, wz_ref, wg_ref, qnw_ref, knw_ref, cos_ref, sin_ref,
                    hm_ref, alog_ref, dtb_ref, q_ref, k_ref, v_ref, raw_ref, z_ref, gb_ref):
    x = x_ref[...]
    ms = jnp.mean(x * x, axis=-1, keepdims=True)
    xn = (x * lax.rsqrt(ms + EPS) * nw_ref[...]).astype(BF16)
    a = _dot(xn, wa_ref[...])
    hm = hm_ref[...]
    lane = lax.broadcasted_iota(jnp.int32, (1, A_WIDTH), 1)
    first_half = (lane & (A_HEAD_DIM // 2)) == 0
    cos = cos_ref[...]
    sin = sin_ref[...]

    def norm_rope(t, w):
        t2 = t * t
        hi = t2.astype(BF16)
        lo = (t2 - hi.astype(F32)).astype(BF16)
        ss = _dot(hi, hm) + _dot(lo, hm)
        tn = t * lax.rsqrt(ss * (1.0 / A_HEAD_DIM) + EPS) * w
        partner = jnp.where(first_half, pltpu.roll(tn, A_WIDTH - A_HEAD_DIM // 2, 1), pltpu.roll(tn, A_HEAD_DIM // 2, 1))
        return tn * cos + partner * sin

    q_ref[...] = norm_rope(a[:, :A_WIDTH], qnw_ref[...])
    k_ref[...] = norm_rope(a[:, A_WIDTH:2 * A_WIDTH], knw_ref[...])
    v_ref[...] = a[:, 2 * A_WIDTH:]
    raw_ref[...] = _dot(xn, wb_ref[...])
    z_ref[...] = _dot(xn, wz_ref[...])
    zg = _dot(xn, wg_ref[...])
    beta = _sigmoid(zg)
    t = zg + dtb_ref[...]
    softplus = jnp.maximum(t, 0.0) + jnp.log1p(jnp.exp(-jnp.abs(t)))
    g = -jnp.exp(alog_ref[...]) * softplus
    glane = lax.broadcasted_iota(jnp.int32, (1, B_HEADS * B_DIM), 1) & (B_DIM - 1)
    gb_ref[...] = jnp.where(glane == 0, beta, jnp.where(glane == 1, g, 0.0))


def _proj_in(x, pos_cos, pos_sin, pos_blocks, wts, tm):
    t = x.shape[0]
    n = t // tm
    row = lambda i: (i, 0)
    const = lambda i: (0, 0)
    posmap = lambda i: (i % pos_blocks, 0)
    full = lambda a: pl.BlockSpec(a.shape, const)
    outs = [(A_WIDTH, "q"), (A_WIDTH, "k"), (A_WIDTH, "v"), (B_QKV, "raw"), (B_HEADS * B_DIM, "z"), (B_HEADS * B_DIM, "gb")]
    return pl.pallas_call(
        _proj_in_kernel,
        grid=(n,),
        in_specs=[pl.BlockSpec((tm, D_MODEL), row), full(wts["norm_mix"]), full(wts["wa"]), full(wts["wb"]), full(wts["wz"]),
                  full(wts["wg"]), full(wts["qnw"]), full(wts["knw"]),
                  pl.BlockSpec((tm, A_WIDTH), posmap), pl.BlockSpec((tm, A_WIDTH), posmap),
                  full(wts["headmat"]), full(wts["alog"]), full(wts["dtb"])],
        out_specs=[pl.BlockSpec((tm, w), row) for w, _ in outs],
        out_shape=[jax.ShapeDtypeStruct((t, w), F32) for w, _ in outs],
        compiler_params=_params("parallel"),
        name="proj_in",
    )(x, wts["norm_mix"], wts["wa"], wts["wb"], wts["wz"], wts["wg"], wts["qnw"], wts["knw"], pos_cos, pos_sin,
      wts["headmat"], wts["alog"], wts["dtb"])


def _attn_prompt_kernel(q_ref, k_ref, v_ref, o_ref, ob_ref, lb_ref, *, seq):
    lane = lax.broadcasted_iota(jnp.int32, (BAND, 2 * A_HEAD_DIM), 1)
    head0 = lane < A_HEAD_DIM
    qi = lax.broadcasted_iota(jnp.int32, (BAND, 2 * BAND), 0)
    kj = lax.broadcasted_iota(jnp.int32, (BAND, 2 * BAND), 1)
    dist = BAND + qi - kj
    in_band = (dist >= 0) & (dist <= BAND)
    scale = 1.0 / math.sqrt(A_HEAD_DIM)

    for br, dil in enumerate(DILATIONS):
        nblk = seq // (BAND * dil)
        shift = dil.bit_length() - 1

        def rows(ref, start):
            if dil == 1:
                return ref[pl.ds(start, BAND), :]
            return ref[pl.ds(start, BAND, stride=dil), :]

        def load(u):
            r = u & (dil - 1)
            n = u >> shift
            start = r + n * (BAND * dil)
            qb = rows(q_ref, start)
            k_own = rows(k_ref, start).astype(BF16)
            v_own = rows(v_ref, start).astype(BF16)
            if nblk > 1:
                pstart = jnp.maximum(start - BAND * dil, r)
                kk = jnp.concatenate([rows(k_ref, pstart).astype(BF16), k_own], axis=0)
                vv = jnp.concatenate([rows(v_ref, pstart).astype(BF16), v_own], axis=0)
                mask = in_band & (kj >= jnp.where(n > 0, 0, BAND))
            else:
                kk = jnp.concatenate([k_own, k_own], axis=0)
                vv = jnp.concatenate([v_own, v_own], axis=0)
                mask = in_band & (kj >= BAND)
            return start, qb, kk, vv, mask

        def attend(qb, kk, vv, mask):
            outs = []
            lses = []
            for hmask in (head0, ~head0):
                qh = jnp.where(hmask, qb, 0.0).astype(BF16)
                s = _dot_nt(qh, kk) * scale
                s = jnp.where(mask, s, NEG_INF)
                mx = jnp.max(s, axis=-1, keepdims=True)
                p = jnp.exp(s - mx)
                l = jnp.sum(p, axis=-1, keepdims=True)
                outs.append(_dot(p.astype(BF16), vv) / l)
                lses.append(mx + jnp.log(l))
            return jnp.where(head0, outs[0], outs[1]), jnp.where(head0, lses[0], lses[1])

        per_group = 16

        def group(g, carry):
            loaded = [load(g * per_group + i) for i in range(per_group)]
            results = [attend(*item[1:]) for item in loaded]
            for (start, *_), (out, lse) in zip(loaded, results):
                if dil == 1:
                    ob_ref[br, pl.ds(start, BAND), :] = out
                    lb_ref[br, pl.ds(start, BAND), :] = lse
                else:
                    ob_ref[br, pl.ds(start, BAND, stride=dil), :] = out
                    lb_ref[br, pl.ds(start, BAND, stride=dil), :] = lse
            return carry

        lax.fori_loop(0, seq // BAND // per_group, group, 0)

    def merge(c, carry):
        rs = pl.ds(pl.multiple_of(c * 256, 256), 256)
        l0, l1, l2 = lb_ref[0, rs, :], lb_ref[1, rs, :], lb_ref[2, rs, :]
        m = jnp.maximum(jnp.maximum(l0, l1), l2)
        w0, w1, w2 = jnp.exp(l0 - m), jnp.exp(l1 - m), jnp.exp(l2 - m)
        o_ref[rs, :] = (w0 * ob_ref[0, rs, :] + w1 * ob_ref[1, rs, :] + w2 * ob_ref[2, rs, :]) / (w0 + w1 + w2)
        return carry

    lax.fori_loop(0, seq // 256, merge, 0)


def _attn_prompt(q, k, v):
    b, s, _ = q.shape
    spec = pl.BlockSpec((None, s, 2 * A_HEAD_DIM), lambda i, j: (i, 0, j))
    return pl.pallas_call(
        functools.partial(_attn_prompt_kernel, seq=s),
        grid=(b, A_HEADS // 2),
        in_specs=[spec, spec, spec],
        out_specs=spec,
        out_shape=jax.ShapeDtypeStruct((b, s, A_WIDTH), F32),
        scratch_shapes=[pltpu.VMEM((len(DILATIONS), s, 2 * A_HEAD_DIM), F32),
                        pltpu.VMEM((len(DILATIONS), s, 2 * A_HEAD_DIM), F32)],
        compiler_params=_params("parallel", "parallel"),
        name="attn_prompt",
    )(q, k, v)


SAMPLE_PAD = 16


def _attn_sample_kernel(q_ref, kn_ref, vn_ref, kc_ref, vc_ref, hs_ref, he_ref, o_ref, ko_ref, vo_ref, kbuf, vbuf,
                        *, win, new):
    nlb = A_WIDTH // 128
    for buf, cache, fresh, shifted in ((kbuf, kc_ref, kn_ref, ko_ref), (vbuf, vc_ref, vn_ref, vo_ref)):
        for j in range(nlb):
            ls = slice(j * 128, (j + 1) * 128)
            buf[j, 0:win, :] = cache[:, ls]
            buf[j, win:win + new, :] = fresh[:, ls]
            buf[j, win + new:win + SAMPLE_PAD, :] = jnp.zeros((SAMPLE_PAD - new, 128), F32)
            shifted[:, ls] = buf[j, new:new + win, :]

    def gather(buf, start, size, stride):
        if stride == 1:
            return jnp.concatenate([buf[j, pl.ds(start, size), :] for j in range(nlb)], axis=1)
        return jnp.concatenate([buf[j, pl.ds(start, size, stride=stride), :] for j in range(nlb)], axis=1)

    hs = hs_ref[...]
    he = he_ref[...]
    scale = 1.0 / math.sqrt(A_HEAD_DIM)
    for l in range(new):
        ql = q_ref[l:l + 1, :]
        outs, lses = [], []
        for dil in DILATIONS:
            start = win + l - BAND * dil
            ks = gather(kbuf, start, BAND, dil)
            vs = gather(vbuf, start, BAND, dil)
            kself = gather(kbuf, win + l, 8, 1)
            vself = gather(vbuf, win + l, 1, 1)
            s_all = _dot_x01(jnp.concatenate([ks, kself], axis=0) * ql, hs) * scale
            s = s_all[:BAND]
            sself = s_all[BAND:BAND + 1]
            mx = jnp.maximum(jnp.max(s, axis=0, keepdims=True), sself)
            p = jnp.exp(s - mx)
            pself = jnp.exp(sself - mx)
            lsum = jnp.sum(p, axis=0, keepdims=True) + pself
            pe = _dot_x01(p, he)
            rows8 = jnp.concatenate([pself, lsum, mx + jnp.log(lsum), jnp.zeros((5, 128), F32)], axis=0)
            exp8 = _dot_x01(rows8, he)
            o = jnp.sum(pe * vs, axis=0, keepdims=True) + exp8[0:1] * vself
            outs.append(o / exp8[1:2])
            lses.append(exp8[2:3])
        m = jnp.maximum(jnp.maximum(lses[0], lses[1]), lses[2])
        w = [jnp.exp(ls - m) for ls in lses]
        o_ref[l:l + 1, :] = (w[0] * outs[0] + w[1] * outs[1] + w[2] * outs[2]) / (w[0] + w[1] + w[2])


def _attn_sample(q, kn, vn, kc, vc, hs, he):
    b, new, _ = q.shape
    win = kc.shape[1]
    assert win >= BAND * max(DILATIONS)
    small = pl.BlockSpec((None, new, A_WIDTH), lambda i: (i, 0, 0))
    big = pl.BlockSpec((None, win, A_WIDTH), lambda i: (i, 0, 0))
    const = lambda a: pl.BlockSpec(a.shape, lambda i: (0, 0))
    return pl.pallas_call(
        functools.partial(_attn_sample_kernel, win=win, new=new),
        grid=(b,),
        in_specs=[small, small, small, big, big, const(hs), const(he)],
        out_specs=[small, big, big],
        out_shape=[jax.ShapeDtypeStruct((b, new, A_WIDTH), F32), jax.ShapeDtypeStruct((b, win, A_WIDTH), F32),
                   jax.ShapeDtypeStruct((b, win, A_WIDTH), F32)],
        scratch_shapes=[pltpu.VMEM((A_WIDTH // 128, win + SAMPLE_PAD, 128), F32),
                        pltpu.VMEM((A_WIDTH // 128, win + SAMPLE_PAD, 128), F32)],
        compiler_params=_params("parallel"),
        name="attn_sample",
    )(q, kn, vn, kc, vc, hs, he)


def _dot3(a, b):
    ah = a.astype(BF16)
    al = (a - ah.astype(F32)).astype(BF16)
    bh = b.astype(BF16)
    bl = (b - bh.astype(F32)).astype(BF16)
    return _dot(ah, bh) + (_dot(ah, bl) + _dot(al, bh))


def _dot_01(a01, b):
    a = a01.astype(BF16)
    b1 = b.astype(BF16)
    r1 = b - b1.astype(F32)
    b2 = r1.astype(BF16)
    b3 = (r1 - b2.astype(F32)).astype(BF16)
    return _dot(a, b1) + (_dot(a, b2) + _dot(a, b3))


def _delta_kernel(raw_ref, hist_ref, cw_ref, gb_ref, z_ref, dnw_ref, s0_ref, o_ref, sf_ref, ext, s_scr, *, blk, chunk):
    c = chunk
    nh = B_HEADS
    r = nh * c
    ncol = 3 * nh
    shift = c.bit_length() - 1
    j = pl.program_id(1)

    @pl.when(j == 0)
    def _():
        for col in range(ncol):
            ext[col, 0:8, :] = hist_ref[:, col * B_DIM:(col + 1) * B_DIM]
        for k in range(nh):
            s_scr[k * B_DIM:(k + 1) * B_DIM, :] = s0_ref[k]

    for col in range(ncol):
        ext[col, 8:8 + blk, :] = raw_ref[:, col * B_DIM:(col + 1) * B_DIM]

    ii = lax.broadcasted_iota(jnp.int32, (r, r), 0)
    jj = lax.broadcasted_iota(jnp.int32, (r, r), 1)
    same = (ii >> shift) == (jj >> shift)
    causal = same & (ii >= jj)
    strict = same & (ii > jj)
    upper = same & (ii <= jj)
    eye = (ii == jj).astype(F32)
    tri = causal.astype(F32)
    ones = jnp.ones((r, r), F32)
    row_head = lax.broadcasted_iota(jnp.int32, (r, nh * B_DIM), 0) >> shift
    lane_head = lax.broadcasted_iota(jnp.int32, (r, nh * B_DIM), 1) >> (B_DIM.bit_length() - 1)
    own_head = row_head == lane_head
    dnw = dnw_ref[...]

    def conv_silu(col, base):
        y = cw_ref[0:1, col * B_DIM:(col + 1) * B_DIM] * ext[col, pl.ds(base + 5, c), :]
        for t in range(1, CONV_WIDTH):
            y = y + cw_ref[t:t + 1, col * B_DIM:(col + 1) * B_DIM] * ext[col, pl.ds(base + 5 + t, c), :]
        return y * _sigmoid(y)

    def l2n(t):
        return t * lax.rsqrt(jnp.sum(t * t, axis=-1, keepdims=True) + EPS)

    def stack(parts):
        return jnp.concatenate(parts, axis=0)

    def block_diag(x):
        return jnp.where(own_head, jnp.concatenate([x] * nh, axis=1), 0.0)

    def body(n, carry):
        base = pl.multiple_of(n * c, c)
        q = stack([l2n(conv_silu(k, base)) for k in range(nh)]) * (B_DIM ** -0.5)
        kk = stack([l2n(conv_silu(nh + k, base)) for k in range(nh)])
        v = stack([conv_silu(2 * nh + k, base) for k in range(nh)])
        gbs = [gb_ref[pl.ds(base, c), k * B_DIM:(k + 1) * B_DIM] for k in range(nh)]
        beta = stack([t[:, 0:1] for t in gbs])
        g = stack([t[:, 1:2] for t in gbs])
        gsq = jnp.broadcast_to(g, (r, r))
        gcol = _dot_01(tri, gsq)
        grow = gcol.T if r >= 128 else _dot_01(ones, jnp.where(upper, gsq, 0.0))
        gcum = gcol[:, 0:1]
        glast = stack([jnp.broadcast_to(gcol[(k + 1) * c - 1:(k + 1) * c, 0:1], (c, 1)) for k in range(nh)])
        decay = jnp.exp(jnp.where(causal, gcol - grow, NEG_INF))
        kb = kk * beta
        kbh, kh, qh = kb.astype(BF16), kk.astype(BF16), q.astype(BF16)
        lower = jnp.where(strict, _dot_nt(kbh, kh) * decay, 0.0)
        tinv = eye - lower
        lp = lower
        span = 2
        while span < c:
            lp = _dot3(lp, lp)
            tinv = tinv + _dot3(tinv, lp)
            span *= 2
        eg = jnp.exp(gcum)
        u = _dot3(tinv, v * beta)
        w = _dot3(tinv, kb * eg)
        intra = jnp.where(causal, _dot_nt(qh, kh) * decay, 0.0)
        state = s_scr[...]
        sh = state.astype(BF16)
        v_new = u - _dot(block_diag(w).astype(BF16), sh)
        vh = v_new.astype(BF16)
        out = _dot(block_diag(q * eg).astype(BF16), sh) + _dot(intra.astype(BF16), vh)
        kdec = kk * jnp.exp(glast - gcum)
        cd = stack([jnp.broadcast_to(jnp.exp(gcol[(k + 1) * c - 1:(k + 1) * c, 0:1]), (B_DIM, 1)) for k in range(nh)])
        s_scr[...] = state * cd + _dot_tn(block_diag(kdec).astype(BF16), vh)
        on = out * lax.rsqrt(jnp.mean(out * out, axis=-1, keepdims=True) + EPS) * dnw
        for k in range(nh):
            zc = z_ref[pl.ds(base, c), k * B_DIM:(k + 1) * B_DIM]
            o_ref[pl.ds(base, c), k * B_DIM:(k + 1) * B_DIM] = on[k * c:(k + 1) * c] * (zc * _sigmoid(zc))
        return carry

    lax.fori_loop(0, blk // c, body, 0, unroll=2 if blk // c >= 2 else 1)
    for col in range(ncol):
        ext[col, 0:8, :] = ext[col, blk:blk + 8, :]

    @pl.when(j == pl.num_programs(1) - 1)
    def _():
        for k in range(nh):
            sf_ref[k] = s_scr[k * B_DIM:(k + 1) * B_DIM, :]


def _delta(raw, hist, conv_w, gb, z, dnw, s0, chunk, blk):
    b, seq, _ = raw.shape
    hb = B_HEADS * B_DIM
    tok = lambda w: pl.BlockSpec((None, blk, w), lambda i, j: (i, j, 0))
    st = pl.BlockSpec((None, B_HEADS, B_DIM, B_DIM), lambda i, j: (i, 0, 0, 0))
    return pl.pallas_call(
        functools.partial(_delta_kernel, blk=blk, chunk=chunk),
        grid=(b, seq // blk),
        in_specs=[tok(B_QKV), pl.BlockSpec((None, 8, B_QKV), lambda i, j: (i, 0, 0)),
                  pl.BlockSpec((CONV_WIDTH, B_QKV), lambda i, j: (0, 0)), tok(hb), tok(hb),
                  pl.BlockSpec((1, B_DIM), lambda i, j: (0, 0)), st],
        out_specs=[tok(hb), st],
        out_shape=[jax.ShapeDtypeStruct((b, seq, hb), F32), jax.ShapeDtypeStruct((b, B_HEADS, B_DIM, B_DIM), F32)],
        scratch_shapes=[pltpu.VMEM((3 * B_HEADS, blk + 8, B_DIM), F32), pltpu.VMEM((B_HEADS * B_DIM, B_DIM), F32)],
        compiler_params=_params("parallel", "arbitrary"),
        name="delta",
    )(raw, hist, conv_w, gb, z, dnw, s0)


def _proj_out_kernel(x_ref, a_ref, b_ref, wa_ref, wb_ref, nw_ref, h_ref, hnt_ref):
    h = x_ref[...] + _dot(a_ref[...].astype(BF16), wa_ref[...]) + _dot(b_ref[...].astype(BF16), wb_ref[...])
    h_ref[...] = h
    hn = h * lax.rsqrt(jnp.mean(h * h, axis=-1, keepdims=True) + EPS) * nw_ref[...]
    hnt = hn.T.astype(BF16)
    for c in range(hnt_ref.shape[0]):
        hnt_ref[c] = hnt[:, c * 128:(c + 1) * 128]


def _proj_out(x, a, b, wts, tm):
    t = x.shape[0]
    row = lambda i: (i, 0)
    full = lambda arr: pl.BlockSpec(arr.shape, lambda i: (0, 0))
    return pl.pallas_call(
        _proj_out_kernel,
        grid=(t // tm,),
        in_specs=[pl.BlockSpec((tm, D_MODEL), row), pl.BlockSpec((tm, A_WIDTH), row), pl.BlockSpec((tm, A_WIDTH), row),
                  full(wts["wo_a"]), full(wts["wo_b"]), full(wts["norm_ffn"])],
        out_specs=[pl.BlockSpec((tm, D_MODEL), row), pl.BlockSpec((tm // 128, D_MODEL, 128), lambda i: (i, 0, 0))],
        out_shape=[jax.ShapeDtypeStruct((t, D_MODEL), F32), jax.ShapeDtypeStruct((t // 128, D_MODEL, 128), BF16)],
        compiler_params=_params("parallel"),
        name="proj_out",
    )(x, a, b, wts["wo_a"], wts["wo_b"], wts["norm_ffn"])


def _top16_ranked(x):
    t = x.shape[1]
    slot = lax.broadcasted_iota(jnp.int32, (PEER_TOPK, t), 0).astype(F32)
    vals = jnp.full((PEER_TOPK, t), NEG_INF, F32)
    rank = jnp.full(x.shape, float(PEER_TOPK), F32)
    filled = jnp.zeros((1, t), F32)
    for _ in range(PEER_TOPK):
        m = jnp.max(x, axis=0, keepdims=True)
        eq = x == m
        cnt = jnp.sum(jnp.where(eq, 1.0, 0.0), axis=0, keepdims=True)
        vals = jnp.where((slot >= filled) & (slot < filled + cnt), m, vals)
        rank = jnp.where(eq, jnp.minimum(filled, float(PEER_TOPK)), rank)
        filled = filled + cnt
        x = jnp.where(eq, NEG_INF, x)
    return vals, rank


def _kth_largest(x, k):
    t = x.shape[1]
    kth = jnp.full((1, t), NEG_INF, F32)
    filled = jnp.zeros((1, t), F32)
    for _ in range(k):
        m = jnp.max(x, axis=0, keepdims=True)
        eq = x == m
        cnt = jnp.sum(jnp.where(eq, 1.0, 0.0), axis=0, keepdims=True)
        kth = jnp.where((filled < k) & (filled + cnt >= k), m, kth)
        filled = filled + cnt
        x = jnp.where(eq, NEG_INF, x)
    return kth


def _peer_select_kernel(hnt_ref, wpq_ref, sk_ref, n1_ref, q1_ref, r2_ref, q2_ref, sub_scr, top_scr, rank_scr):
    nslab = hnt_ref.shape[0]
    hnt = jnp.concatenate([hnt_ref[c] for c in range(nslab)], axis=1)

    def half(hc, carry):
        row0 = pl.multiple_of(hc * PEER_KEYS, PEER_KEYS)
        qt = _dot(wpq_ref[pl.ds(row0, PEER_KEYS), :], hnt)
        sub = _dot(sk_ref[hc & 1], qt.astype(BF16))
        sub_scr[hc] = sub
        top_scr[hc], rank_scr[hc] = _top16_ranked(sub)
        return carry

    lax.fori_loop(0, 2 * PEER_HEADS, half, 0, unroll=2)

    def slabs(ref, h, x):
        for c in range(nslab):
            ref[h, c] = x[:, c * 128:(c + 1) * 128].astype(ref.dtype)

    def head(h, carry):
        a = top_scr[2 * h]
        b = top_scr[2 * h + 1]
        pieces = [a[0:1] + b] + [a[i:i + 1] + b[0:8] for i in range(1, 8)] + [a[8:16] + b[0:1]]
        cand = jnp.concatenate(pieces, axis=0)
        tau = _kth_largest(cand, PEER_TOPK)
        top = a[0:1] + b[0:1]
        zsum = jnp.sum(jnp.where(cand >= tau, jnp.exp(cand - top), 0.0), axis=0, keepdims=True)
        cnt = jnp.zeros_like(a)
        for j in range(PEER_TOPK):
            cnt = cnt + jnp.where(a + b[j:j + 1] >= tau, 1.0, 0.0)
        rank1 = rank_scr[2 * h]
        n1 = jnp.zeros_like(rank1)
        for i in range(PEER_TOPK):
            n1 = jnp.where(rank1 == float(i), cnt[i:i + 1], n1)
        slabs(n1_ref, h, n1)
        slabs(r2_ref, h, rank_scr[2 * h + 1])
        slabs(q1_ref, h, jnp.exp(sub_scr[2 * h] - a[0:1]) / zsum)
        slabs(q2_ref, h, jnp.exp(sub_scr[2 * h + 1] - b[0:1]))
        return carry

    lax.fori_loop(0, PEER_HEADS, head, 0, unroll=2)


def _peer_select(hnt, wts, tt):
    t = hnt.shape[0] * 128
    out = pl.BlockSpec((PEER_HEADS, tt // 128, PEER_KEYS, 128), lambda i: (0, i, 0, 0))
    shape = lambda dt: jax.ShapeDtypeStruct((PEER_HEADS, t // 128, PEER_KEYS, 128), dt)
    return pl.pallas_call(
        _peer_select_kernel,
        grid=(t // tt,),
        in_specs=[pl.BlockSpec((tt // 128, D_MODEL, 128), lambda i: (i, 0, 0)),
                  pl.BlockSpec(wts["wpq_t"].shape, lambda i: (0, 0)),
                  pl.BlockSpec(wts["sub_keys"].shape, lambda i: (0, 0, 0))],
        out_specs=[out, out, out, out],
        out_shape=[shape(F32), shape(F32), shape(BF16), shape(BF16)],
        scratch_shapes=[pltpu.VMEM((2 * PEER_HEADS, PEER_KEYS, tt), F32), pltpu.VMEM((2 * PEER_HEADS, PEER_TOPK, tt), F32),
                        pltpu.VMEM((2 * PEER_HEADS, PEER_KEYS, tt), F32)],
        compiler_params=_params("parallel"),
        name="peer_select",
    )(hnt, wts["wpq_t"], wts["sub_keys"])


A_PER_TILE = 8


def _rows_bf16(row):
    return jnp.broadcast_to(row, (PEER_KEYS, 128)).astype(BF16)


def _peer_dense_kernel(hnt_ref, u_ref, vt_ref, n1_ref, q1_ref, r2_ref, q2_ref, h_ref, y_ref, acc_ref, pre_ref, w_ref,
                       r2_scr, q2_scr, *, tt):
    j = pl.program_id(1)
    nslab = tt // 128

    @pl.when(j == 0)
    def _():
        acc_ref[...] = jnp.zeros_like(acc_ref)
        r2_scr[...] = r2_ref[...]
        q2_scr[...] = q2_ref[...]

    per_sub = min(nslab, 2)
    nsub = nslab // per_sub
    zero = jnp.zeros((PEER_KEYS, 128), BF16)
    group = 2

    def mm_u(s):
        hn = jnp.concatenate([hnt_ref[s * per_sub + c] for c in range(per_sub)], axis=1)
        pre = _dot(u_ref[...], hn).astype(BF16)
        for c in range(per_sub):
            pre_ref[s * per_sub + c] = pre[:, c * 128:(c + 1) * 128]

    def build(tc, carry):
        for ia0 in range(0, A_PER_TILE, group):
            gates = [zero] * group
            for h in range(PEER_HEADS):
                r2 = r2_scr[h, tc]
                q2 = q2_scr[h, tc]
                for k in range(group):
                    n1 = _rows_bf16(n1_ref[h, tc, ia0 + k:ia0 + k + 1, :])
                    q1 = _rows_bf16(q1_ref[h, tc, ia0 + k:ia0 + k + 1, :])
                    gates[k] = gates[k] + jnp.where(r2 < n1, q2, zero) * q1
            for k in range(group):
                rows = slice((ia0 + k) * PEER_KEYS, (ia0 + k + 1) * PEER_KEYS)
                p = pre_ref[tc, rows, :]
                act = (0.5 * p) * (1.0 + lax.erf(p * (1.0 / math.sqrt(2.0))))
                w_ref[tc, rows, :] = gates[k] * act
        return carry

    def mm_v(s):
        w = jnp.concatenate([w_ref[s * per_sub + c] for c in range(per_sub)], axis=1)
        out = _dot(vt_ref[...], w)
        for c in range(per_sub):
            acc_ref[s * per_sub + c] += out[:, c * 128:(c + 1) * 128]

    for s in range(nsub):
        mm_u(s)
    lax.fori_loop(0, nslab, build, 0)
    for s in range(nsub):
        mm_v(s)

    @pl.when(j == pl.num_programs(1) - 1)
    def _():
        for c in range(nslab):
            y_ref[c * 128:(c + 1) * 128, :] = h_ref[c * 128:(c + 1) * 128, :] + acc_ref[c].T


def _peer_dense(hnt, h, sel, wts, tt):
    t = hnt.shape[0] * 128
    n1, q1, r2, q2 = sel
    et = A_PER_TILE * PEER_KEYS
    nslab = tt // 128
    a_spec = pl.BlockSpec((PEER_HEADS, nslab, A_PER_TILE, 128), lambda i, j: (0, i, j, 0))
    b_spec = pl.BlockSpec((PEER_HEADS, nslab, PEER_KEYS, 128), lambda i, j: (0, i, 0, 0))
    return pl.pallas_call(
        functools.partial(_peer_dense_kernel, tt=tt),
        grid=(t // tt, PEER_KEYS // A_PER_TILE),
        in_specs=[pl.BlockSpec((nslab, D_MODEL, 128), lambda i, j: (i, 0, 0)),
                  pl.BlockSpec((et, D_MODEL), lambda i, j: (j, 0)),
                  pl.BlockSpec((D_MODEL, et), lambda i, j: (0, j)),
                  a_spec, a_spec, b_spec, b_spec,
                  pl.BlockSpec((tt, D_MODEL), lambda i, j: (i, 0))],
        out_specs=pl.BlockSpec((tt, D_MODEL), lambda i, j: (i, 0)),
        out_shape=jax.ShapeDtypeStruct((t, D_MODEL), F32),
        scratch_shapes=[pltpu.VMEM((nslab, D_MODEL, 128), F32), pltpu.VMEM((nslab, et, 128), BF16),
                        pltpu.VMEM((nslab, et, 128), BF16),
                        pltpu.VMEM((PEER_HEADS, nslab, PEER_KEYS, 128), BF16),
                        pltpu.VMEM((PEER_HEADS, nslab, PEER_KEYS, 128), BF16)],
        compiler_params=_params("parallel", "arbitrary"),
        name="peer_dense",
    )(hnt, wts["u"], wts["v_t"], n1, q1, r2, q2, h)


def _rope_tables(pos, reps):
    half = A_HEAD_DIM // 2
    inv_freq = ROPE_THETA ** (-jnp.arange(half, dtype=F32) / half)
    ang = pos.astype(F32)[:, None] * inv_freq[None, :]
    cos = jnp.cos(ang)
    sin = jnp.sin(ang)
    cos_t = jnp.tile(jnp.concatenate([cos, cos], axis=-1), (reps, A_HEADS))
    sin_t = jnp.tile(jnp.concatenate([-sin, sin], axis=-1), (reps, A_HEADS))
    return cos_t, sin_t


def _prep_weights(norm_mix_w, w_in, q_norm_w, k_norm_w, a_log, dt_bias, w_out, norm_ffn_w, w_pq, sub_keys, expert_u, expert_v):
    na = 3 * A_WIDTH
    hb = B_HEADS * B_DIM
    wgate = w_in[:, na + B_QKV + hb:]
    wg = jnp.zeros((D_MODEL, B_HEADS, B_DIM), F32)
    wg = wg.at[:, :, 0].set(wgate[:, :B_HEADS]).at[:, :, 1].set(wgate[:, B_HEADS:])
    lane_head = jnp.arange(A_WIDTH) // A_HEAD_DIM
    put1 = lambda vec: jnp.zeros((B_HEADS, B_DIM), F32).at[:, 1].set(vec).reshape(1, hb)
    return {
        "norm_mix": norm_mix_w.reshape(1, D_MODEL),
        "wa": w_in[:, :na].astype(BF16),
        "wb": w_in[:, na:na + B_QKV].astype(BF16),
        "wz": w_in[:, na + B_QKV:na + B_QKV + hb].astype(BF16),
        "wg": wg.reshape(D_MODEL, hb).astype(BF16),
        "qnw": jnp.tile(q_norm_w, A_HEADS).reshape(1, A_WIDTH),
        "knw": jnp.tile(k_norm_w, A_HEADS).reshape(1, A_WIDTH),
        "headmat": (lane_head[:, None] == lane_head[None, :]).astype(BF16),
        "alog": put1(a_log),
        "dtb": put1(dt_bias),
        "wo_a": w_out[:A_WIDTH].astype(BF16),
        "wo_b": w_out[A_WIDTH:].astype(BF16),
        "norm_ffn": norm_ffn_w.reshape(1, D_MODEL),
        "wpq_t": w_pq.T.astype(BF16),
        "sub_keys": sub_keys.astype(BF16),
        "u": expert_u.astype(BF16),
        "v_t": expert_v.T.astype(BF16),
    }


def _layer(x, pos_cos, pos_sin, pos_blocks, wts, conv_w, dnw, mixer_a, hist, s0, chunk, tm, tt_sel, tt_dense):
    b, t, _ = x.shape
    xf = x.reshape(b * t, D_MODEL)
    q, k, v, raw, z, gb = _proj_in(xf, pos_cos, pos_sin, pos_blocks, wts, tm)
    a_out, k_buf, v_buf = mixer_a(q.reshape(b, t, A_WIDTH), k.reshape(b, t, A_WIDTH), v.reshape(b, t, A_WIDTH))
    raw = raw.reshape(b, t, B_QKV)
    conv_state = raw[:, t - (CONV_WIDTH - 1):]
    pad = (-t) % chunk
    pad3 = lambda arr: jnp.pad(arr.reshape(b, t, -1), ((0, 0), (0, pad), (0, 0)))
    b_out, s_fin = _delta(pad3(raw), hist, conv_w, pad3(gb), pad3(z), dnw, s0, chunk, min(512, t + pad))
    b_out = b_out[:, :t].reshape(b * t, B_HEADS * B_DIM)
    h, hnt = _proj_out(xf, a_out.reshape(b * t, A_WIDTH), b_out, wts, tm)
    sel = _peer_select(hnt, wts, tt_sel)
    y = _peer_dense(hnt, h, sel, wts, tt_dense)
    return y.reshape(b, t, D_MODEL), k_buf, v_buf, conv_state, s_fin


def kernel(x_prompt, x_sample, cache_k_win, cache_v_win, state_conv, state_delta, norm_mix_w, w_in, q_norm_w, k_norm_w, conv_w, a_log, dt_bias, delta_norm_w, w_out, norm_ffn_w, w_pq, sub_keys, expert_u, expert_v):
    assert w_in.shape[0] == 1, "one layer"
    bp, sp, _ = x_prompt.shape
    bs, ss, _ = x_sample.shape
    wts = _prep_weights(norm_mix_w[0], w_in[0], q_norm_w[0], k_norm_w[0], a_log[0], dt_bias[0], w_out[0], norm_ffn_w[0],
                        w_pq[0], sub_keys[0], expert_u[0], expert_v[0])
    cw = conv_w[0]
    dnw = delta_norm_w[0].reshape(1, B_DIM)
    lane_head = jnp.arange(A_WIDTH) // A_HEAD_DIM
    head_sum = (lane_head[:, None] == jnp.arange(128)[None, :]).astype(F32)
    head_expand = head_sum.T

    tm_p = 512
    cos_p, sin_p = _rope_tables(jnp.arange(sp), 1)

    def mixer_prompt(q, k, v):
        return _attn_prompt(q, k, v), k, v

    yp, kp, vp, cp, dp = _layer(
        x_prompt, cos_p, sin_p, sp // tm_p, wts, cw, dnw, mixer_prompt,
        jnp.zeros((bp, 8, B_QKV), F32), jnp.zeros((bp, B_HEADS, B_DIM, B_DIM), F32),
        chunk=64, tm=tm_p, tt_sel=512, tt_dense=1024)

    n_s = bs * ss
    cos_s, sin_s = _rope_tables(PAST_LEN + jnp.arange(ss), bs)

    def mixer_sample(q, k, v):
        kc = cache_k_win[0].reshape(bs, -1, A_WIDTH)
        vc = cache_v_win[0].reshape(bs, -1, A_WIDTH)
        return _attn_sample(q, k, v, kc, vc, head_sum, head_expand)

    hist_s = jnp.pad(state_conv[0], ((0, 0), (8 - (CONV_WIDTH - 1), 0), (0, 0)))
    ys, ks, vs, cs, ds = _layer(
        x_sample, cos_s, sin_s, 1, wts, cw, dnw, mixer_sample, hist_s, state_delta[0],
        chunk=8, tm=n_s, tt_sel=n_s, tt_dense=n_s)

    win_p = min(BAND * max(DILATIONS), sp)
    shape_kv = lambda arr, b: arr.reshape(1, b, -1, A_HEADS, A_HEAD_DIM)
    return (yp, ys,
            shape_kv(kp[:, sp - win_p:], bp), shape_kv(vp[:, sp - win_p:], bp), cp[None], dp[None],
            shape_kv(ks, bs), shape_kv(vs, bs), cs[None], ds[None])
```
